```python
import math
import jax, jax.numpy as jnp
from jax import lax
import numpy as np


D_MODEL = 2048
BATCH = 2
SEQ = 16384
DEPTH = 4

CHUNK = 64
Q_BLOCK = 128
A_HEADS = 16
A_HEAD_DIM = 128
A_LATENT = 256
IDX_HEADS = 16
IDX_DIM = 128
TOPK_MAX = 256
REL_BUCKETS = 32
REL_MAX_DIST = 128
R_HEADS = 8
R_QK_DIM = 256
R_V_DIM = 512
ROPE_BASE = 10000.0
N_EXPERTS = 64
EXPERT_DIM = 256
TOP_K = 8
N_GROUPS = 8
TOPK_GROUPS = 4
ROUTED_SCALE = 2.5
MOE_BLOCK = 128
DEEPNORM_ALPHA = (2 * DEPTH) ** 0.25
DEEPNORM_BETA = (8 * DEPTH) ** -0.25
LN_EPS = 1e-5

A_WIDTH = A_HEADS * A_HEAD_DIM
R_QK_WIDTH = R_HEADS * R_QK_DIM
R_V_WIDTH = R_HEADS * R_V_DIM
IN_SPLITS = (A_WIDTH, A_LATENT, IDX_HEADS * IDX_DIM, IDX_DIM, IDX_HEADS,
             R_QK_WIDTH, R_QK_WIDTH, R_V_WIDTH, R_V_WIDTH, D_MODEL, D_MODEL)
IN_WIDTH = sum(IN_SPLITS)

kernel_name = 'hybrid_dsa_retention_moe_encoder'


def layer_norm(x, w, b):
    xf = x.astype(jnp.float32)
    mu = jnp.mean(xf, -1, keepdims=True)
    var = jnp.mean(jnp.square(xf - mu), -1, keepdims=True)
    return ((xf - mu) * lax.rsqrt(var + LN_EPS) * w + b).astype(x.dtype)


def rms_norm(x, w):
    xf = x.astype(jnp.float32)
    y = xf * lax.rsqrt(jnp.mean(xf * xf, -1, keepdims=True) + 1e-6)
    return (y * w).astype(x.dtype)


def head_group_norm(o, w, b, dtype):
    B, S = o.shape[:2]
    of = o.astype(jnp.float32)
    mu = jnp.mean(of, -1, keepdims=True)
    var = jnp.mean(jnp.square(of - mu), -1, keepdims=True)
    y = ((of - mu) * lax.rsqrt(var + LN_EPS)).reshape(B, S, -1)
    return (y * w + b).astype(dtype)


def rotary(x):
    S, d = x.shape[1], x.shape[-1]
    half = d // 2
    inv = ROPE_BASE ** (-jnp.arange(half, dtype=jnp.float32) / half)
    ang = jnp.arange(S, dtype=jnp.float32)[:, None] * inv[None, :]
    cos = jnp.cos(ang)[None, :, None, :]
    sin = jnp.sin(ang)[None, :, None, :]
    x1 = x[..., :half].astype(jnp.float32)
    x2 = x[..., half:].astype(jnp.float32)
    return jnp.concatenate([x1 * cos - x2 * sin, x1 * sin + x2 * cos], -1).astype(x.dtype)


def t5_bucket(rel):
    half = REL_BUCKETS // 2
    max_exact = half // 2
    ret = jnp.where(rel > 0, half, 0)
    n = jnp.abs(rel)
    nf = jnp.maximum(n, 1).astype(jnp.float32)
    large = max_exact + (jnp.log(nf / max_exact) / math.log(REL_MAX_DIST / max_exact)
                         * (half - max_exact)).astype(jnp.int32)
    large = jnp.minimum(large, half - 1)
    return ret + jnp.where(n < max_exact, n, large)


def dsa_attention(q_a, c_kv, q_idx, k_idx, w_idx, w_uk, w_uv, rel_bias):
    B, S = c_kv.shape[:2]
    topk = min(TOPK_MAX, S // 4)
    q_abs = jnp.einsum('bshd,hcd->bshc', q_a, w_uk)
    w_idx = w_idx * (IDX_HEADS ** -0.5 * IDX_DIM ** -0.5)
    scale = A_HEAD_DIM ** -0.5
    key_pos = jnp.arange(S)
    gather = jax.vmap(lambda src, idx: src[idx])

    def block(t0):
        qa = lax.dynamic_slice_in_dim(q_abs, t0, Q_BLOCK, 1)
        qi = lax.dynamic_slice_in_dim(q_idx, t0, Q_BLOCK, 1)
        wi = lax.dynamic_slice_in_dim(w_idx, t0, Q_BLOCK, 1)
        t = t0 + jnp.arange(Q_BLOCK)
        limit = (t // CHUNK + 1) * CHUNK
        score = jnp.einsum('bth,bhts->bts', wi,
                           jax.nn.relu(jnp.einsum('bthd,bsd->bhts', qi, k_idx))).astype(jnp.float32)
        score = jnp.where(key_pos[None, None, :] < limit[None, :, None], score, -jnp.inf)
        _, sel = lax.top_k(score, topk)
        c_sel = gather(c_kv, sel)
        logits = jnp.einsum('bthc,btkc->bhtk', qa, c_sel).astype(jnp.float32) * scale
        bias = rel_bias[t5_bucket(sel - t[None, :, None])]
        logits = logits + jnp.transpose(bias, (0, 3, 1, 2)).astype(jnp.float32)
        ok = sel < limit[None, :, None]
        logits = jnp.where(ok[:, None], logits, -jnp.inf)
        p = jax.nn.softmax(logits, axis=-1).astype(c_kv.dtype)
        o_lat = jnp.einsum('bhtk,btkc->bthc', p, c_sel)
        o = jnp.einsum('bthc,hcd->bthd', o_lat, w_uv)
        return o.reshape(B, Q_BLOCK, -1)

    out = lax.map(block, jnp.arange(S // Q_BLOCK) * Q_BLOCK)
    return jnp.transpose(out, (1, 0, 2, 3)).reshape(B, S, -1)


def retention(q, k, v):
    B, S, H, dk = q.shape
    n = S // CHUNK
    log_g = jnp.log1p(-jnp.exp2(-5.0 - jnp.arange(H, dtype=jnp.float32)))
    pos = jnp.arange(CHUNK, dtype=jnp.float32)
    d_intra = jnp.exp(log_g[:, None, None] * jnp.abs(pos[:, None] - pos[None, :]))
    xi = jnp.exp(log_g[None, :] * (pos[:, None] + 1.0))
    zeta = jnp.exp(log_g[None, :] * (CHUNK - 1.0 - pos[:, None]))
    g_chunk = jnp.exp(log_g * CHUNK)

    def to_chunks(a):
        return jnp.moveaxis(a.astype(jnp.float32).reshape(B, n, CHUNK, H, -1), 1, 0)

    def step(state, qkv):
        qc, kc, vc = qkv
        s = jnp.einsum('bihd,bjhd->bhij', qc, kc) * d_intra
        o = (jnp.einsum('bhij,bjhe->bihe', s, vc)
             + jnp.einsum('bihd,bhde->bihe', qc, state) * xi[..., None])
        state = state * g_chunk[:, None, None] + jnp.einsum('bjhd,bjhe->bhde', kc * zeta[..., None], vc)
        return state, o

    state0 = jnp.zeros((B, H, dk, v.shape[-1]), jnp.float32)
    _, o = lax.scan(step, state0, (to_chunks(q), to_chunks(k * dk ** -0.5), to_chunks(v)))
    return jnp.moveaxis(o, 0, 1).reshape(B, S, H, -1)


def moe_ffn(x2, w_router, b_router, we_gate, we_up, we_down, ws_gate, ws_up, ws_down):
    N, D = x2.shape
    scores = jax.nn.sigmoid((x2 @ w_router).astype(jnp.float32))
    biased = scores + b_router.astype(jnp.float32)
    grp = biased.reshape(N, N_GROUPS, N_EXPERTS // N_GROUPS)
    grp_score = jnp.sum(lax.top_k(grp, 2)[0], -1)
    _, gsel = lax.top_k(grp_score, TOPK_GROUPS)
    gmask = jnp.sum(jax.nn.one_hot(gsel, N_GROUPS, dtype=jnp.float32), 1) > 0
    emask = jnp.repeat(gmask, N_EXPERTS // N_GROUPS, axis=1)
    _, eidx = lax.top_k(jnp.where(emask, biased, -jnp.inf), TOP_K)
    s_sel = jnp.take_along_axis(scores, eidx, axis=1)
    gates = s_sel / jnp.sum(s_sel, -1, keepdims=True) * ROUTED_SCALE

    nk = N * TOP_K
    n_blocks = -(-nk // MOE_BLOCK) + N_EXPERTS
    flat_e = eidx.reshape(-1)
    flat_tok = jnp.repeat(jnp.arange(N, dtype=jnp.int32), TOP_K)
    flat_g = gates.reshape(-1)
    order = jnp.argsort(flat_e)
    se = flat_e[order]
    counts = jnp.bincount(flat_e, length=N_EXPERTS)
    starts = jnp.cumsum(counts) - counts
    pad_counts = (counts + MOE_BLOCK - 1) // MOE_BLOCK * MOE_BLOCK
    pad_ends = jnp.cumsum(pad_counts)
    pad_starts = pad_ends - pad_counts
    dest = pad_starts[se] + jnp.arange(nk, dtype=jnp.int32) - starts[se]
    tok_buf = jnp.zeros((n_blocks * MOE_BLOCK,), jnp.int32).at[dest].set(flat_tok[order])
    g_buf = jnp.zeros((n_blocks * MOE_BLOCK,), jnp.float32).at[dest].set(flat_g[order])
    blk_e = jnp.minimum(jnp.searchsorted(pad_ends, jnp.arange(n_blocks, dtype=jnp.int32) * MOE_BLOCK,
                                         side='right'), N_EXPERTS - 1)

    def step(acc, blk):
        tok, g, e = blk
        xb = x2[tok]
        h = jax.nn.silu(xb @ we_gate[e]) * (xb @ we_up[e])
        y = (h @ we_down[e]) * g[:, None].astype(x2.dtype)
        return acc.at[tok].add(y), None

    routed, _ = lax.scan(step, jnp.zeros_like(x2),
                         (tok_buf.reshape(n_blocks, MOE_BLOCK), g_buf.reshape(n_blocks, MOE_BLOCK), blk_e))
    shared = (jax.nn.silu(x2 @ ws_gate) * (x2 @ ws_up)) @ ws_down
    return routed + shared


def hybrid_layer(x, rel_bias, w_in, ckv_norm, kidx_norm, w_uk, w_uv, gn_w, gn_b, w_pa, w_pb, w_o,
                 ln1_w, ln1_b, w_router, b_router, we_gate, we_up, we_down, ws_gate, ws_up, ws_down,
                 ln2_w, ln2_b):
    B, S, D = x.shape
    proj = x @ w_in
    cuts = [int(c) for c in np.cumsum(IN_SPLITS)[:-1]]
    q_a, c_kv, q_idx, k_idx, w_idx, q_r, k_r, v_r, g_r, g_a, g_b = jnp.split(proj, cuts, axis=-1)

    y_a = dsa_attention(q_a.reshape(B, S, A_HEADS, A_HEAD_DIM), rms_norm(c_kv, ckv_norm),
                        q_idx.reshape(B, S, IDX_HEADS, IDX_DIM), rms_norm(k_idx, kidx_norm),
                        w_idx, w_uk, w_uv, rel_bias)

    o_r = retention(rotary(q_r.reshape(B, S, R_HEADS, R_QK_DIM)),
                    rotary(k_r.reshape(B, S, R_HEADS, R_QK_DIM)),
                    v_r.reshape(B, S, R_HEADS, R_V_DIM))
    y_r = jax.nn.silu(g_r) * head_group_norm(o_r, gn_w, gn_b, x.dtype)

    merged = jax.nn.sigmoid(g_a) * (y_a @ w_pa) + jax.nn.sigmoid(g_b) * (y_r @ w_pb)
    x = layer_norm(DEEPNORM_ALPHA * x + merged @ w_o, ln1_w, ln1_b)

    y = moe_ffn(x.reshape(B * S, D), w_router, b_router, we_gate, we_up, we_down,
                ws_gate, ws_up, ws_down).reshape(B, S, D)
    return layer_norm(DEEPNORM_ALPHA * x + y, ln2_w, ln2_b)


def setup_inputs(seed: int = 0) -> dict:
    key = jax.random.key(seed)
    ks = jax.random.split(key, 24)
    L, D = DEPTH, D_MODEL

    def nrm(k, shape, scale):
        return jax.random.normal(k, shape, jnp.float32) * scale

    def gain(k, shape):
        return 1.0 + 0.01 * jax.random.normal(k, shape, jnp.float32)

    def small(k, shape):
        return 0.01 * jax.random.normal(k, shape, jnp.float32)

    return {
        'x': nrm(ks[0], (BATCH, SEQ, D), 1.0),
        'rel_bias': nrm(ks[1], (REL_BUCKETS, A_HEADS), 0.5),
        'w_in': nrm(ks[2], (L, D, IN_WIDTH), D ** -0.5),
        'ckv_norm': gain(ks[3], (L, A_LATENT)),
        'kidx_norm': gain(ks[4], (L, IDX_DIM)),
        'w_uk': nrm(ks[5], (L, A_HEADS, A_LATENT, A_HEAD_DIM), A_LATENT ** -0.5),
        'w_uv': nrm(ks[6], (L, A_HEADS, A_LATENT, A_HEAD_DIM), A_LATENT ** -0.5),
        'gn_w': gain(ks[7], (L, R_V_WIDTH)),
        'gn_b': small(ks[8], (L, R_V_WIDTH)),
        'w_pa': nrm(ks[9], (L, A_WIDTH, D), A_WIDTH ** -0.5),
        'w_pb': nrm(ks[10], (L, R_V_WIDTH, D), R_V_WIDTH ** -0.5),
        'w_o': nrm(ks[11], (L, D, D), D ** -0.5 * DEEPNORM_BETA),
        'ln1_w': gain(ks[12], (L, D)),
        'ln1_b': small(ks[13], (L, D)),
        'w_router': nrm(ks[14], (L, D, N_EXPERTS), D ** -0.5),
        'b_router': small(ks[15], (L, N_EXPERTS)),
        'we_gate': nrm(ks[16], (L, N_EXPERTS, D, EXPERT_DIM), D ** -0.5),
        'we_up': nrm(ks[17], (L, N_EXPERTS, D, EXPERT_DIM), D ** -0.5),
        'we_down': nrm(ks[18], (L, N_EXPERTS, EXPERT_DIM, D), EXPERT_DIM ** -0.5 * DEEPNORM_BETA),
        'ws_gate': nrm(ks[19], (L, D, EXPERT_DIM), D ** -0.5),
        'ws_up': nrm(ks[20], (L, D, EXPERT_DIM), D ** -0.5),
        'ws_down': nrm(ks[21], (L, EXPERT_DIM, D), EXPERT_DIM ** -0.5 * DEEPNORM_BETA),
        'ln2_w': gain(ks[22], (L, D)),
        'ln2_b': small(ks[23], (L, D)),
    }


def reference(x, rel_bias, w_in, ckv_norm, kidx_norm, w_uk, w_uv, gn_w, gn_b, w_pa, w_pb, w_o,
              ln1_w, ln1_b, w_router, b_router, we_gate, we_up, we_down, ws_gate, ws_up, ws_down,
              ln2_w, ln2_b):
    for l in range(DEPTH):
        x = hybrid_layer(x, rel_bias, w_in[l], ckv_norm[l], kidx_norm[l], w_uk[l], w_uv[l],
                         gn_w[l], gn_b[l], w_pa[l], w_pb[l], w_o[l], ln1_w[l], ln1_b[l],
                         w_router[l], b_router[l], we_gate[l], we_up[l], we_down[l],
                         ws_gate[l], ws_up[l], ws_down[l], ln2_w[l], ln2_b[l])
    return x
```

```python
import functools
import math

import jax
import jax.numpy as jnp
import numpy as np
from jax import lax
from jax.experimental import pallas as pl
from jax.experimental.pallas import tpu as pltpu

CHUNK = 64
Q_BLOCK = 128
A_HEADS = 16
A_HEAD_DIM = 128
A_LATENT = 256
IDX_HEADS = 16
IDX_DIM = 128
TOPK_MAX = 256
REL_BUCKETS = 32
REL_MAX_DIST = 128
R_HEADS = 8
R_QK_DIM = 256
R_V_DIM = 512
ROPE_BASE = 10000.0
N_EXPERTS = 64
EXPERT_DIM = 256
TOP_K = 8
N_GROUPS = 8
TOPK_GROUPS = 4
ROUTED_SCALE = 2.5
LN_EPS = 1e-5
RMS_EPS = 1e-6

LANES = 128
FAR_TILE = 512
RET_GROUP = 256
VMEM_LIMIT = 56 * 1024 * 1024
INT_MIN = -2 ** 31
NEG_BIG = -1e30

F32 = jnp.float32
BF16 = jnp.bfloat16


def _cparams(*sem):
    return pltpu.CompilerParams(dimension_semantics=sem, vmem_limit_bytes=VMEM_LIMIT)


def _sigmoid(x):
    return 1.0 / (1.0 + jnp.exp(-x))


def _dot(a, b):
    return jnp.dot(a, b, preferred_element_type=F32)


def _dot_nt(a, b):
    return lax.dot_general(a, b, (((1,), (1,)), ((), ())), preferred_element_type=F32)


def _layer_norm_rows(v, w, b):
    mu = jnp.mean(v, axis=-1, keepdims=True)
    d = v - mu
    var = jnp.mean(d * d, axis=-1, keepdims=True)
    return d * lax.rsqrt(var + LN_EPS) * w + b


def _mm_kernel(*refs, epilogue, n_extra, n_out):
    a_ref, b_ref = refs[0], refs[1]
    extra = refs[2:2 + n_extra]
    outs = refs[2 + n_extra:2 + n_extra + n_out]
    b = b_ref[...]
    if b.ndim == 3:
        b = b[0]
    acc = _dot(a_ref[...], b)
    res = epilogue(acc, *extra)
    if not isinstance(res, tuple):
        res = (res,)
    for o_ref, r in zip(outs, res):
        o_ref[...] = r.astype(o_ref.dtype)


def _mm(a, b, *, tm, tn, out_dtypes, epilogue, extras=(), extra_specs=(), per_head=False, name):
    M = a.shape[0]
    if per_head:
        H, kh, tn = b.shape
        n_cols = H * tn
        grid = (M // tm, H)
        a_spec = pl.BlockSpec((tm, kh), lambda i, j: (i, j))
        b_spec = pl.BlockSpec((1, kh, tn), lambda i, j: (j, 0, 0))
    else:
        K, n_cols = b.shape
        grid = (M // tm, n_cols // tn)
        a_spec = pl.BlockSpec((tm, K), lambda i, j: (i, 0))
        b_spec = pl.BlockSpec((K, tn), lambda i, j: (0, j))
    out_shape = [jax.ShapeDtypeStruct((M, n_cols), dt) for dt in out_dtypes]
    out_specs = [pl.BlockSpec((tm, tn), lambda i, j: (i, j)) for _ in out_dtypes]
    res = pl.pallas_call(
        functools.partial(_mm_kernel, epilogue=epilogue, n_extra=len(extras), n_out=len(out_dtypes)),
        grid=grid,
        in_specs=[a_spec, b_spec, *extra_specs],
        out_specs=out_specs,
        out_shape=out_shape,
        compiler_params=_cparams("parallel", "arbitrary"),
        name=name,
    )(a, b, *extras)
    return res[0] if len(res) == 1 else res


def _epi_plain(acc):
    return acc


def _epi_scale(acc, *, scale):
    return acc * scale


def _epi_rms(acc, w_ref):
    return acc * lax.rsqrt(jnp.mean(acc * acc, axis=-1, keepdims=True) + RMS_EPS) * w_ref[...]


def _epi_rope(acc, cos_ref, sin_ref, *, scale):
    half = acc.shape[-1] // 2
    x1, x2 = acc[:, :half], acc[:, half:]
    c, s = cos_ref[...], sin_ref[...]
    return jnp.concatenate([x1 * c - x2 * s, x1 * s + x2 * c], axis=-1) * scale


def _epi_gate(acc, g_ref):
    return _sigmoid(g_ref[...]) * acc


def _epi_gate_add(acc, g_ref, add_ref):
    return _sigmoid(g_ref[...]) * acc + add_ref[...]


def _epi_ln(acc, res_ref, w_ref, b_ref, *, alpha):
    y = _layer_norm_rows(alpha * res_ref[...] + acc, w_ref[...], b_ref[...])
    return y, y


def _float_key(x):
    bits = pltpu.bitcast(x, jnp.int32)
    return bits ^ ((bits >> 31) & jnp.int32(0x7FFFFFFF))


def _dsa_kernel(bias15_ref, qabs_ref, qidx_ref, widx_ref, kidx_ref, ckv_ref, toep_ref, out_ref,
                scf_ref, scn_ref, tmp_ref, qa2_ref, qi2_ref, wb_ref, p_ref, acc_ref, m_ref, l_ref,
                al_ref, *, topk):
    i = pl.program_id(1)
    t0 = i * Q_BLOCK
    far_end = t0 - Q_BLOCK
    n_far = (jnp.maximum(i - 1, 0) * Q_BLOCK + FAR_TILE - 1) // FAR_TILE
    n_ch = FAR_TILE // LANES
    H = A_HEADS
    QB = Q_BLOCK

    for h in range(H):
        qa2_ref[h * QB:(h + 1) * QB, :] = qabs_ref[:, h * A_LATENT:(h + 1) * A_LATENT]
        qi2_ref[h * QB:(h + 1) * QB, :] = qidx_ref[:, h * IDX_DIM:(h + 1) * IDX_DIM]
        wb_ref[h] = jnp.broadcast_to(widx_ref[:, h:h + 1], (QB, LANES))

    lane = lax.broadcasted_iota(jnp.int32, (QB, LANES), 1)
    row = lax.broadcasted_iota(jnp.int32, (QB, LANES), 0)

    def head_sum(c):
        acc = jnp.zeros((QB, LANES), F32)
        for h in range(H):
            z = tmp_ref[h * QB:(h + 1) * QB, c * LANES:(c + 1) * LANES]
            acc = acc + wb_ref[h] * jnp.maximum(z, 0.0)
        return acc

    def far_scores(kt, carry):
        start = pl.multiple_of(Q_BLOCK + kt * FAR_TILE, LANES)
        tmp_ref[...] = _dot_nt(qi2_ref[...], kidx_ref[pl.ds(start, FAR_TILE), :])
        for c in range(n_ch):
            key = _float_key(head_sum(c))
            s_pos = kt * FAR_TILE + c * LANES + lane
            scf_ref[kt, :, c * LANES:(c + 1) * LANES] = jnp.where(s_pos < far_end, key, INT_MIN)
        return carry

    lax.fori_loop(0, n_far, far_scores, 0)

    near_start = pl.multiple_of(t0, LANES)
    tmp_ref[:, :2 * LANES] = _dot_nt(qi2_ref[...], kidx_ref[pl.ds(near_start, 2 * LANES), :])
    u_lo = jnp.where(i == 0, QB, 0)
    u_hi = jnp.where(row < CHUNK, QB + CHUNK, 2 * QB)
    for c in range(2):
        key = _float_key(head_sum(c))
        u = c * LANES + lane
        vis = jnp.logical_and(u >= u_lo, u < u_hi)
        scn_ref[:, c * LANES:(c + 1) * LANES] = jnp.where(vis, key, INT_MIN)

    def bit_body(bi, p):
        bit = jnp.left_shift(jnp.int32(1), 31 - bi)
        p_try = p | bit
        t_try = p_try ^ jnp.int32(INT_MIN)

        def cnt_body(kt, cnt):
            for c in range(n_ch):
                cnt = cnt + jnp.where(scf_ref[kt, :, c * LANES:(c + 1) * LANES] >= t_try, 1, 0)
            return cnt

        cnt = lax.fori_loop(0, n_far, cnt_body, jnp.zeros((QB, LANES), jnp.int32))
        for c in range(2):
            cnt = cnt + jnp.where(scn_ref[:, c * LANES:(c + 1) * LANES] >= t_try, 1, 0)
        total = jnp.sum(cnt, axis=1, keepdims=True)
        return jnp.where(total >= topk, p_try, p)

    p_fin = lax.fori_loop(0, 32, bit_body, jnp.zeros((QB, LANES), jnp.int32))
    thr = jnp.maximum(p_fin ^ jnp.int32(INT_MIN), jnp.int32(INT_MIN + 1))

    m_ref[...] = jnp.full(m_ref.shape, NEG_BIG, F32)
    l_ref[...] = jnp.zeros(l_ref.shape, F32)
    acc_ref[...] = jnp.zeros(acc_ref.shape, F32)

    def softmax_tile(width, sel_chunks, bias_of_head, c_tile):
        n_c = width // LANES
        for h in range(H):
            rows = slice(h * QB, (h + 1) * QB)
            bias = bias_of_head(h)
            masked = []
            for c in range(n_c):
                lg = tmp_ref[rows, c * LANES:(c + 1) * LANES]
                if not isinstance(bias, tuple):
                    masked.append(jnp.where(sel_chunks[c], lg, -jnp.inf))
                else:
                    masked.append(jnp.where(sel_chunks[c], lg + bias[0][:, c * LANES:(c + 1) * LANES], -jnp.inf))
            mx = masked[0]
            for c in range(1, n_c):
                mx = jnp.maximum(mx, masked[c])
            mx = jnp.max(mx, axis=1, keepdims=True)
            if not isinstance(bias, tuple):
                mx = mx + bias
            m_old = m_ref[rows, :]
            m_new = jnp.maximum(m_old, mx)
            shift = m_new if isinstance(bias, tuple) else m_new - bias
            psum = jnp.zeros((QB, LANES), F32)
            for c in range(n_c):
                p = jnp.exp(masked[c] - shift)
                psum = psum + p
                p_ref[rows, c * LANES:(c + 1) * LANES] = p.astype(BF16)
            alpha = jnp.exp(m_old - m_new)
            l_ref[rows, :] = alpha * l_ref[rows, :] + jnp.sum(psum, axis=1, keepdims=True)
            m_ref[rows, :] = m_new
            al_ref[rows, :] = alpha
        pv = _dot(p_ref[:, :width], c_tile)
        for c in range(A_LATENT // LANES):
            cols = slice(c * LANES, (c + 1) * LANES)
            acc_ref[:, cols] = acc_ref[:, cols] * al_ref[...] + pv[:, cols]

    def far_attn(kt, carry):
        start = pl.multiple_of(Q_BLOCK + kt * FAR_TILE, LANES)
        c_tile = ckv_ref[pl.ds(start, FAR_TILE), :]
        tmp_ref[...] = _dot_nt(qa2_ref[...], c_tile)
        sel = [scf_ref[kt, :, c * LANES:(c + 1) * LANES] >= thr for c in range(n_ch)]
        softmax_tile(FAR_TILE, sel, lambda h: bias15_ref[h], c_tile)
        return carry

    lax.fori_loop(0, n_far, far_attn, 0)

    c_near = ckv_ref[pl.ds(near_start, 2 * LANES), :]
    tmp_ref[:, :2 * LANES] = _dot_nt(qa2_ref[...], c_near)
    sel_near = [scn_ref[:, c * LANES:(c + 1) * LANES] >= thr for c in range(2)]
    softmax_tile(2 * LANES, sel_near, lambda h: (toep_ref[h],), c_near)

    for h in range(H):
        rows = slice(h * QB, (h + 1) * QB)
        inv = 1.0 / l_ref[rows, :]
        for c in range(A_LATENT // LANES):
            out_ref[:, h * A_LATENT + c * LANES:h * A_LATENT + (c + 1) * LANES] = (
                acc_ref[rows, c * LANES:(c + 1) * LANES] * inv).astype(out_ref.dtype)


def _dsa(qabs, qidx, widx, kidx_pad, ckv_pad, toep, bias15, *, B, S):
    nqb = S // Q_BLOCK
    topk = min(TOPK_MAX, S // 4)
    n_far_max = max(1, -(-(S - 2 * Q_BLOCK) // FAR_TILE))
    HQ = A_HEADS * Q_BLOCK
    resident = dict(pipeline_mode=pl.Buffered(1))
    return pl.pallas_call(
        functools.partial(_dsa_kernel, topk=topk),
        grid=(B, nqb),
        in_specs=[
            pl.BlockSpec(memory_space=pltpu.SMEM),
            pl.BlockSpec((Q_BLOCK, A_HEADS * A_LATENT), lambda b, i: (b * nqb + i, 0)),
            pl.BlockSpec((Q_BLOCK, IDX_HEADS * IDX_DIM), lambda b, i: (b * nqb + i, 0)),
            pl.BlockSpec((Q_BLOCK, LANES), lambda b, i: (b * nqb + i, 0)),
            pl.BlockSpec((None, S + Q_BLOCK, IDX_DIM), lambda b, i: (b, 0, 0), **resident),
            pl.BlockSpec((None, S + Q_BLOCK, A_LATENT), lambda b, i: (b, 0, 0), **resident),
            pl.BlockSpec((A_HEADS, Q_BLOCK, 2 * LANES), lambda b, i: (0, 0, 0), **resident),
        ],
        out_specs=pl.BlockSpec((Q_BLOCK, A_HEADS * A_LATENT), lambda b, i: (b * nqb + i, 0)),
        out_shape=jax.ShapeDtypeStruct((B * S, A_HEADS * A_LATENT), BF16),
        scratch_shapes=[
            pltpu.VMEM((n_far_max, Q_BLOCK, FAR_TILE), jnp.int32),
            pltpu.VMEM((Q_BLOCK, 2 * LANES), jnp.int32),
            pltpu.VMEM((HQ, FAR_TILE), F32),
            pltpu.VMEM((HQ, A_LATENT), BF16),
            pltpu.VMEM((HQ, IDX_DIM), BF16),
            pltpu.VMEM((IDX_HEADS, Q_BLOCK, LANES), F32),
            pltpu.VMEM((HQ, FAR_TILE), BF16),
            pltpu.VMEM((HQ, A_LATENT), F32),
            pltpu.VMEM((HQ, LANES), F32),
            pltpu.VMEM((HQ, LANES), F32),
            pltpu.VMEM((HQ, LANES), F32),
        ],
        compiler_params=_cparams("parallel", "arbitrary"),
        name="dsa",
    )(bias15, qabs, qidx, widx, kidx_pad, ckv_pad, toep)


def _ret_kernel(gtot_ref, q_ref, k_ref, v_ref, g_ref, d_ref, xi_ref, zeta_ref, gnw_ref, gnb_ref,
                o_ref, state_ref):
    @pl.when(pl.program_id(2) == 0)
    def _():
        state_ref[...] = jnp.zeros(state_ref.shape, F32)

    h = pl.program_id(1)
    q, k, v = q_ref[...], k_ref[...], v_ref[...]
    s = _dot_nt(q, k) * d_ref[0]
    o = _dot(s.astype(BF16), v) + _dot(q, state_ref[...].astype(BF16)) * xi_ref[0]
    kz = (k.astype(F32) * zeta_ref[0]).astype(BF16)
    upd = lax.dot_general(kz, v, (((0,), (0,)), ((), ())), preferred_element_type=F32)
    state_ref[...] = state_ref[...] * gtot_ref[h] + upd

    mu = jnp.mean(o, axis=-1, keepdims=True)
    d = o - mu
    var = jnp.mean(d * d, axis=-1, keepdims=True)
    y = d * lax.rsqrt(var + LN_EPS) * gnw_ref[...] + gnb_ref[...]
    g = g_ref[...]
    o_ref[...] = (g * _sigmoid(g) * y).astype(o_ref.dtype)


def _retention(q_rot, k_rot, v, g_r, gn_w, gn_b, consts, *, B, S):
    d_mat, xi, zeta, gtot = consts
    ng = S // RET_GROUP
    G = RET_GROUP
    return pl.pallas_call(
        _ret_kernel,
        grid=(B, R_HEADS, ng),
        in_specs=[
            pl.BlockSpec(memory_space=pltpu.SMEM),
            pl.BlockSpec((G, R_QK_DIM), lambda b, h, g: (b * ng + g, h)),
            pl.BlockSpec((G, R_QK_DIM), lambda b, h, g: (b * ng + g, h)),
            pl.BlockSpec((G, R_V_DIM), lambda b, h, g: (b * ng + g, h)),
            pl.BlockSpec((G, R_V_DIM), lambda b, h, g: (b * ng + g, h)),
            pl.BlockSpec((1, G, G), lambda b, h, g: (h, 0, 0)),
            pl.BlockSpec((1, G, 1), lambda b, h, g: (h, 0, 0)),
            pl.BlockSpec((1, G, 1), lambda b, h, g: (h, 0, 0)),
            pl.BlockSpec((1, R_V_DIM), lambda b, h, g: (0, h)),
            pl.BlockSpec((1, R_V_DIM), lambda b, h, g: (0, h)),
        ],
        out_specs=pl.BlockSpec((G, R_V_DIM), lambda b, h, g: (b * ng + g, h)),
        out_shape=jax.ShapeDtypeStruct((B * S, R_HEADS * R_V_DIM), BF16),
        scratch_shapes=[pltpu.VMEM((R_QK_DIM, R_V_DIM), F32)],
        compiler_params=_cparams("parallel", "parallel", "arbitrary"),
        name="retention",
    )(gtot, q_rot, k_rot, v, g_r, d_mat, xi, zeta, gn_w, gn_b)


def _retention_consts():
    G = RET_GROUP
    log_g = jnp.log1p(-jnp.exp2(-5.0 - jnp.arange(R_HEADS, dtype=F32)))
    pos = jnp.arange(G, dtype=F32)
    diff = pos[:, None] - pos[None, :]
    ci = jnp.arange(G)[:, None] // CHUNK
    cj = jnp.arange(G)[None, :] // CHUNK
    same = jnp.exp(log_g[:, None, None] * jnp.abs(diff))
    earlier = jnp.exp(log_g[:, None, None] * diff)
    d_mat = jnp.where(ci == cj, same, jnp.where(cj < ci, earlier, 0.0))
    xi = jnp.exp(log_g[:, None] * (pos[None, :] + 1.0))[..., None]
    zeta = jnp.exp(log_g[:, None] * (G - 1.0 - pos[None, :]))[..., None]
    gtot = jnp.exp(log_g * G)
    return d_mat, xi, zeta, gtot


def _rank_rows(v):
    n = v.shape[0]
    ridx = lax.broadcasted_iota(jnp.int32, v.shape, 0)
    rank = jnp.zeros(v.shape, jnp.int32)
    for j in range(n):
        rj = v[j:j + 1, :]
        rank = rank + jnp.where(ridx > j, jnp.where(rj >= v, 1, 0), jnp.where(rj > v, 1, 0))
    return rank


def _router_kernel(x_ref, wr_ref, b_ref, gt_ref):
    st = _dot_nt(wr_ref[...], x_ref[...])
    sig = _sigmoid(st)
    biased = sig + b_ref[...]
    per = N_EXPERTS // N_GROUPS
    blocks = [biased[g * per:(g + 1) * per, :] for g in range(N_GROUPS)]
    gscore = []
    for blk in blocks:
        top2 = jnp.where(_rank_rows(blk) < 2, blk, 0.0)
        gscore.append(jnp.sum(top2, axis=0, keepdims=True))
    masked = []
    for g in range(N_GROUPS):
        grank = jnp.zeros(gscore[g].shape, jnp.int32)
        for g2 in range(N_GROUPS):
            if g2 == g:
                continue
            beats = (gscore[g2] >= gscore[g]) if g2 < g else (gscore[g2] > gscore[g])
            grank = grank + jnp.where(beats, 1, 0)
        keep = jnp.broadcast_to(grank, blocks[g].shape) < TOPK_GROUPS
        masked.append(jnp.where(keep, blocks[g], -jnp.inf))
    cand = jnp.concatenate(masked, axis=0)
    sel = _rank_rows(cand) < TOP_K
    gates = jnp.where(sel, sig, 0.0)
    denom = jnp.sum(gates, axis=0, keepdims=True)
    gt_ref[...] = gates / denom * ROUTED_SCALE


def _router(xb, wr_t, b_col, *, tm):
    N = xb.shape[0]
    return pl.pallas_call(
        _router_kernel,
        grid=(N // tm,),
        in_specs=[
            pl.BlockSpec((tm, xb.shape[1]), lambda i: (i, 0)),
            pl.BlockSpec(wr_t.shape, lambda i: (0, 0)),
            pl.BlockSpec(b_col.shape, lambda i: (0, 0)),
        ],
        out_specs=pl.BlockSpec((N_EXPERTS, tm), lambda i: (0, i)),
        out_shape=jax.ShapeDtypeStruct((N_EXPERTS, N), F32),
        compiler_params=_cparams("parallel"),
        name="router",
    )(xb, wr_t, b_col)


def _moe_kernel(x_ref, g_ref, wg_ref, wu_ref, wd_ref, lnw_ref, lnb_ref, of_ref, ob_ref,
                xb_ref, acc_ref, *, alpha):
    e = pl.program_id(1)

    @pl.when(e == 0)
    def _():
        xb_ref[...] = x_ref[...].astype(BF16)
        acc_ref[...] = jnp.zeros(acc_ref.shape, F32)

    xb = xb_ref[...]
    hg = _dot(xb, wg_ref[0])
    hu = _dot(xb, wu_ref[0])
    hidden = (hg * _sigmoid(hg) * hu).astype(BF16)
    lane = lax.broadcasted_iota(jnp.int32, g_ref.shape, 1)
    gate = jnp.sum(jnp.where(lane == e, g_ref[...], 0.0), axis=1, keepdims=True)
    acc_ref[...] += _dot(hidden, wd_ref[0]) * gate

    @pl.when(e == pl.num_programs(1) - 1)
    def _():
        y = _layer_norm_rows(alpha * x_ref[...] + acc_ref[...], lnw_ref[...], lnb_ref[...])
        of_ref[...] = y
        ob_ref[...] = y.astype(BF16)


def _moe(x1, gates, wg, wu, wd, ln_w, ln_b, *, alpha, tm):
    N, D = x1.shape
    n_e = wg.shape[0]
    return pl.pallas_call(
        functools.partial(_moe_kernel, alpha=alpha),
        grid=(N // tm, n_e),
        in_specs=[
            pl.BlockSpec((tm, D), lambda i, e: (i, 0)),
            pl.BlockSpec((tm, LANES), lambda i, e: (i, 0)),
            pl.BlockSpec((1, D, EXPERT_DIM), lambda i, e: (e, 0, 0)),
            pl.BlockSpec((1, D, EXPERT_DIM), lambda i, e: (e, 0, 0)),
            pl.BlockSpec((1, EXPERT_DIM, D), lambda i, e: (e, 0, 0)),
            pl.BlockSpec((1, D), lambda i, e: (0, 0)),
            pl.BlockSpec((1, D), lambda i, e: (0, 0)),
        ],
        out_specs=[pl.BlockSpec((tm, D), lambda i, e: (i, 0)),
                   pl.BlockSpec((tm, D), lambda i, e: (i, 0))],
        out_shape=[jax.ShapeDtypeStruct((N, D), F32), jax.ShapeDtypeStruct((N, D), BF16)],
        scratch_shapes=[pltpu.VMEM((tm, D), BF16), pltpu.VMEM((tm, D), F32)],
        compiler_params=_cparams("parallel", "arbitrary"),
        name="moe",
    )(x1, gates, wg, wu, wd, ln_w, ln_b)


def _t5_bucket(rel):
    half = REL_BUCKETS // 2
    max_exact = half // 2
    ret = jnp.where(rel > 0, half, 0)
    n = jnp.abs(rel)
    nf = jnp.maximum(n, 1).astype(F32)
    large = max_exact + (jnp.log(nf / max_exact) / math.log(REL_MAX_DIST / max_exact)
                         * (half - max_exact)).astype(jnp.int32)
    large = jnp.minimum(large, half - 1)
    return ret + jnp.where(n < max_exact, n, large)


def _bias_tables(rel_bias):
    i = jnp.arange(Q_BLOCK)[:, None]
    u = jnp.arange(2 * Q_BLOCK)[None, :]
    toep = jnp.transpose(rel_bias[_t5_bucket(u - Q_BLOCK - i)], (2, 0, 1))
    far = rel_bias[_t5_bucket(jnp.asarray(-(Q_BLOCK + 1)))]
    return toep.astype(F32), far.astype(F32)


def _rope_tables(S):
    half = R_QK_DIM // 2
    inv = ROPE_BASE ** (-jnp.arange(half, dtype=F32) / half)
    ang = jnp.arange(S, dtype=F32)[:, None] * inv[None, :]
    return jnp.cos(ang), jnp.sin(ang)


def _layer(x, xb, tables, p, *, B, S, alpha):
    N, D = x.shape
    cos, sin, toep, bias15, ret_consts = tables
    TM = 512 if N % 512 == 0 else 256
    a_w = A_HEADS * A_HEAD_DIM
    cuts = np.cumsum([0, a_w, A_LATENT, IDX_HEADS * IDX_DIM, IDX_DIM, IDX_HEADS,
                      R_HEADS * R_QK_DIM, R_HEADS * R_QK_DIM, R_HEADS * R_V_DIM, R_HEADS * R_V_DIM, D, D])
    w_in = p['w_in']
    piece = lambda k: w_in[:, int(cuts[k]):int(cuts[k + 1])].astype(BF16)
    row_spec = lambda w: pl.BlockSpec((1, w), lambda i, j: (0, 0))

    q_a = _mm(xb, piece(0), tm=TM, tn=512, out_dtypes=[BF16], epilogue=_epi_plain, name="proj_qa")
    c_kv = _mm(xb, piece(1), tm=TM, tn=A_LATENT, out_dtypes=[BF16], epilogue=_epi_rms,
               extras=[p['ckv_norm'][None, :]], extra_specs=[row_spec(A_LATENT)], name="proj_ckv")
    q_idx = _mm(xb, piece(2), tm=TM, tn=512, out_dtypes=[BF16], epilogue=_epi_plain, name="proj_qidx")
    k_idx = _mm(xb, piece(3), tm=TM, tn=IDX_DIM, out_dtypes=[BF16], epilogue=_epi_rms,
                extras=[p['kidx_norm'][None, :]], extra_specs=[row_spec(IDX_DIM)], name="proj_kidx")
    w_widx = jnp.pad(piece(4), ((0, 0), (0, LANES - IDX_HEADS)))
    w_idx = _mm(xb, w_widx, tm=TM, tn=LANES, out_dtypes=[F32],
                epilogue=functools.partial(_epi_scale, scale=IDX_HEADS ** -0.5 * IDX_DIM ** -0.5),
                name="proj_widx")
    pos_spec = pl.BlockSpec((TM, R_QK_DIM // 2), lambda i, j: (i % (S // TM), 0))
    q_r = _mm(xb, piece(5), tm=TM, tn=R_QK_DIM, out_dtypes=[BF16],
              epilogue=functools.partial(_epi_rope, scale=1.0),
              extras=[cos, sin], extra_specs=[pos_spec, pos_spec], name="proj_qr")
    k_r = _mm(xb, piece(6), tm=TM, tn=R_QK_DIM, out_dtypes=[BF16],
              epilogue=functools.partial(_epi_rope, scale=R_QK_DIM ** -0.5),
              extras=[cos, sin], extra_specs=[pos_spec, pos_spec], name="proj_kr")
    v_r = _mm(xb, piece(7), tm=TM, tn=512, out_dtypes=[BF16], epilogue=_epi_plain, name="proj_vr")
    g_r = _mm(xb, piece(8), tm=TM, tn=512, out_dtypes=[F32], epilogue=_epi_plain, name="proj_gr")
    g_a = _mm(xb, piece(9), tm=TM, tn=512, out_dtypes=[F32], epilogue=_epi_plain, name="proj_ga")
    g_b = _mm(xb, piece(10), tm=TM, tn=512, out_dtypes=[F32], epilogue=_epi_plain, name="proj_gb")

    w_uk_t = jnp.transpose(p['w_uk'], (0, 2, 1)).astype(BF16)
    q_abs = _mm(q_a, w_uk_t, tm=TM, tn=A_LATENT, out_dtypes=[BF16], per_head=True,
                epilogue=functools.partial(_epi_scale, scale=A_HEAD_DIM ** -0.5), name="q_absorb")
    pad = lambda a: jnp.pad(a.reshape(B, S, -1), ((0, 0), (Q_BLOCK, 0), (0, 0)))
    o_lat = _dsa(q_abs, q_idx, w_idx, pad(k_idx), pad(c_kv), toep, bias15, B=B, S=S)
    y_a = _mm(o_lat, p['w_uv'].astype(BF16), tm=TM, tn=A_HEAD_DIM, out_dtypes=[BF16], per_head=True,
              epilogue=_epi_plain, name="uv_expand")

    y_r = _retention(q_r, k_r, v_r, g_r, p['gn_w'][None, :], p['gn_b'][None, :], ret_consts, B=B, S=S)

    tile_spec = pl.BlockSpec((TM, 512), lambda i, j: (i, j))
    m_a = _mm(y_a, p['w_pa'].astype(BF16), tm=TM, tn=512, out_dtypes=[F32], epilogue=_epi_gate,
              extras=[g_a], extra_specs=[tile_spec], name="proj_a")
    merged = _mm(y_r, p['w_pb'].astype(BF16), tm=TM, tn=512, out_dtypes=[BF16], epilogue=_epi_gate_add,
                 extras=[g_b, m_a], extra_specs=[tile_spec, tile_spec], name="proj_b")
    full_row = pl.BlockSpec((TM, D), lambda i, j: (i, 0))
    x1, x1b = _mm(merged, p['w_o'].astype(BF16), tm=TM, tn=D, out_dtypes=[F32, BF16],
                  epilogue=functools.partial(_epi_ln, alpha=alpha),
                  extras=[x, p['ln1_w'][None, :], p['ln1_b'][None, :]],
                  extra_specs=[full_row, row_spec(D), row_spec(D)], name="out_ln1")

    gt = _router(x1b, p['w_router'].T.astype(BF16), p['b_router'][:, None], tm=TM)
    gates = jnp.concatenate([gt.T, jnp.ones((N, 1), F32), jnp.zeros((N, LANES - N_EXPERTS - 1), F32)], axis=1)
    wg = jnp.concatenate([p['we_gate'], p['ws_gate'][None]], axis=0).astype(BF16)
    wu = jnp.concatenate([p['we_up'], p['ws_up'][None]], axis=0).astype(BF16)
    wd = jnp.concatenate([p['we_down'], p['ws_down'][None]], axis=0).astype(BF16)
    return _moe(x1, gates, wg, wu, wd, p['ln2_w'][None, :], p['ln2_b'][None, :], alpha=alpha, tm=TM)


def kernel(x, rel_bias, w_in, ckv_norm, kidx_norm, w_uk, w_uv, gn_w, gn_b, w_pa, w_pb, w_o, ln1_w, ln1_b,
           w_router, b_router, we_gate, we_up, we_down, ws_gate, ws_up, ws_down, ln2_w, ln2_b):
    B, S, D = x.shape
    depth = w_in.shape[0]
    alpha = (2 * depth) ** 0.25
    cos, sin = _rope_tables(S)
    toep, bias15 = _bias_tables(rel_bias)
    tables = (cos, sin, toep, bias15, _retention_consts())
    params = dict(w_in=w_in, ckv_norm=ckv_norm, kidx_norm=kidx_norm, w_uk=w_uk, w_uv=w_uv, gn_w=gn_w,
                  gn_b=gn_b, w_pa=w_pa, w_pb=w_pb, w_o=w_o, ln1_w=ln1_w, ln1_b=ln1_b, w_router=w_router,
                  b_router=b_router, we_gate=we_gate, we_up=we_up, we_down=we_down, ws_gate=ws_gate,
                  ws_up=ws_up, ws_down=ws_down, ln2_w=ln2_w, ln2_b=ln2_b)
    xf = x.reshape(B * S, D)
    xb = xf.astype(BF16)
    for l in range(depth):
        xf, xb = _layer(xf, xb, tables, {k: v[l] for k, v in params.items()}, B=B, S=S, alpha=alpha)
    return xf.reshape(B, S, D)
```

```python
import functools
import math

import jax
import jax.numpy as jnp
import numpy as np
from jax import lax
from jax.experimental import pallas as pl
from jax.experimental.pallas import tpu as pltpu

CHUNK = 64
Q_BLOCK = 128
A_HEADS = 16
A_HEAD_DIM = 128
A_LATENT = 256
IDX_HEADS = 16
IDX_DIM = 128
TOPK_MAX = 256
REL_BUCKETS = 32
REL_MAX_DIST = 128
R_HEADS = 8
R_QK_DIM = 256
R_V_DIM = 512
ROPE_BASE = 10000.0
N_EXPERTS = 64
EXPERT_DIM = 256
TOP_K = 8
N_GROUPS = 8
TOPK_GROUPS = 4
ROUTED_SCALE = 2.5
LN_EPS = 1e-5
RMS_EPS = 1e-6

LANES = 128
FAR_TILE = 512
RET_GROUP = 256
VMEM_LIMIT = 56 * 1024 * 1024
INT_MIN = -2 ** 31
NEG_BIG = -1e30
LOG2E = 1.4426950408889634

F32 = jnp.float32
BF16 = jnp.bfloat16


def _cparams(*sem):
    return pltpu.CompilerParams(dimension_semantics=sem, vmem_limit_bytes=VMEM_LIMIT)


def _sigmoid(x):
    return 1.0 / (1.0 + jnp.exp(-x))


def _dot(a, b):
    return jnp.dot(a, b, preferred_element_type=F32)


def _dot_nt(a, b):
    return lax.dot_general(a, b, (((1,), (1,)), ((), ())), preferred_element_type=F32)


def _layer_norm_rows(v, w, b):
    mu = jnp.mean(v, axis=-1, keepdims=True)
    d = v - mu
    var = jnp.mean(d * d, axis=-1, keepdims=True)
    return d * lax.rsqrt(var + LN_EPS) * w + b


def _mm_kernel(*refs, epilogue, n_extra, n_out):
    a_ref, b_ref = refs[0], refs[1]
    extra = refs[2:2 + n_extra]
    outs = refs[2 + n_extra:2 + n_extra + n_out]
    acc = _dot(a_ref[...], b_ref[...])
    res = epilogue(acc, *extra)
    if not isinstance(res, tuple):
        res = (res,)
    for o_ref, r in zip(outs, res):
        o_ref[...] = r.astype(o_ref.dtype)


def _mm(a, b, *, tm, tn, out_dtypes, epilogue, extras=(), extra_specs=(), name):
    M = a.shape[0]
    K, n_cols = b.shape
    grid = (M // tm, n_cols // tn)
    a_spec = pl.BlockSpec((tm, K), lambda i, j: (i, 0))
    b_spec = pl.BlockSpec((K, tn), lambda i, j: (0, j))
    out_shape = [jax.ShapeDtypeStruct((M, n_cols), dt) for dt in out_dtypes]
    out_specs = [pl.BlockSpec((tm, tn), lambda i, j: (i, j)) for _ in out_dtypes]
    res = pl.pallas_call(
        functools.partial(_mm_kernel, epilogue=epilogue, n_extra=len(extras), n_out=len(out_dtypes)),
        grid=grid,
        in_specs=[a_spec, b_spec, *extra_specs],
        out_specs=out_specs,
        out_shape=out_shape,
        compiler_params=_cparams("parallel", "arbitrary"),
        name=name,
    )(a, b, *extras)
    return res[0] if len(res) == 1 else res


def _epi_plain(acc):
    return acc


def _epi_scale(acc, *, scale):
    return acc * scale


def _epi_rms(acc, w_ref):
    return acc * lax.rsqrt(jnp.mean(acc * acc, axis=-1, keepdims=True) + RMS_EPS) * w_ref[...]


def _epi_rope(acc, cos_ref, sin_ref, *, scale):
    half = R_QK_DIM // 2
    c, s = cos_ref[...] * scale, sin_ref[...] * scale
    out = []
    for h in range(acc.shape[-1] // R_QK_DIM):
        x1 = acc[:, h * R_QK_DIM:h * R_QK_DIM + half]
        x2 = acc[:, h * R_QK_DIM + half:(h + 1) * R_QK_DIM]
        out += [x1 * c - x2 * s, x1 * s + x2 * c]
    return jnp.concatenate(out, axis=-1)


def _epi_gate(acc, g_ref):
    return _sigmoid(g_ref[...]) * acc


def _epi_gate_add(acc, g_ref, add_ref):
    return _sigmoid(g_ref[...]) * acc + add_ref[...]


def _epi_ln(acc, res_ref, w_ref, b_ref, *, alpha):
    y = _layer_norm_rows(alpha * res_ref[...] + acc, w_ref[...], b_ref[...])
    return y, y


def _float_key(x):
    bits = pltpu.bitcast(x, jnp.int32)
    return bits ^ ((bits >> 31) & jnp.int32(0x7FFFFFFF))


def _dsa_kernel(bias15_ref, qa_ref, qidx_ref, widx_ref, kidx_ref, ckv_ref, toep_ref, wuk_ref, wuv_ref,
                out_ref, scf_ref, scn_ref, tmp0_ref, tmp1_ref, madd_ref, qa2_ref, qi2_ref, wb_ref,
                p0_ref, p1_ref, acc_ref, m_ref, l_ref, al0_ref, al1_ref, *, topk):
    i = pl.program_id(1)
    t0 = i * Q_BLOCK
    far_end = t0 - Q_BLOCK
    n_far = (jnp.maximum(i - 1, 0) * Q_BLOCK + FAR_TILE - 1) // FAR_TILE
    n_ch = FAR_TILE // LANES
    H = A_HEADS
    QB = Q_BLOCK

    for h in range(H):
        q_abs = _dot(qa_ref[:, h * A_HEAD_DIM:(h + 1) * A_HEAD_DIM], wuk_ref[h])
        qa2_ref[h * QB:(h + 1) * QB, :] = (q_abs * (A_HEAD_DIM ** -0.5 * LOG2E)).astype(BF16)
        qi2_ref[h * QB:(h + 1) * QB, :] = qidx_ref[:, h * IDX_DIM:(h + 1) * IDX_DIM]
        wb_ref[h] = jnp.broadcast_to(widx_ref[:, h:h + 1], (QB, LANES))

    lane = lax.broadcasted_iota(jnp.int32, (QB, LANES), 1)
    row = lax.broadcasted_iota(jnp.int32, (QB, LANES), 0)

    tmp = (tmp0_ref, tmp1_ref)
    pbuf = (p0_ref, p1_ref)
    albuf = (al0_ref, al1_ref)
    n_pairs = (n_far + 1) // 2
    last_start = kidx_ref.shape[0] - FAR_TILE

    def far_rows(kt):
        return pl.ds(pl.multiple_of(jnp.clip(Q_BLOCK + kt * FAR_TILE, Q_BLOCK, last_start), LANES), FAR_TILE)

    def head_sum(z_ref, c):
        acc = jnp.zeros((QB, LANES), F32)
        for h in range(H):
            z = z_ref[h * QB:(h + 1) * QB, c * LANES:(c + 1) * LANES]
            acc = acc + wb_ref[h] * jnp.maximum(z, 0.0)
        return acc

    def far_scores(z_ref, kt):
        for c in range(n_ch):
            key = _float_key(head_sum(z_ref, c))
            s_pos = kt * FAR_TILE + c * LANES + lane
            scf_ref[kt, :, c * LANES:(c + 1) * LANES] = jnp.where(s_pos < far_end, key, INT_MIN)

    def idx_matmul(z_ref, kt):
        z_ref[...] = _dot_nt(qi2_ref[...], kidx_ref[far_rows(kt), :])

    idx_matmul(tmp[0], 0)

    def score_pair(j, carry):
        kt = 2 * j
        idx_matmul(tmp[1], kt + 1)
        far_scores(tmp[0], kt)
        idx_matmul(tmp[0], kt + 2)
        far_scores(tmp[1], kt + 1)
        return carry

    lax.fori_loop(0, n_pairs, score_pair, 0)

    near_start = pl.multiple_of(t0, LANES)
    tmp[0][:, :2 * LANES] = _dot_nt(qi2_ref[...], kidx_ref[pl.ds(near_start, 2 * LANES), :])
    u_lo = jnp.where(i == 0, QB, 0)
    u_hi = jnp.where(row < CHUNK, QB + CHUNK, 2 * QB)
    for c in range(2):
        key = _float_key(head_sum(tmp[0], c))
        u = c * LANES + lane
        vis = jnp.logical_and(u >= u_lo, u < u_hi)
        scn_ref[:, c * LANES:(c + 1) * LANES] = jnp.where(vis, key, INT_MIN)

    def bit_body(bi, p):
        bit = jnp.left_shift(jnp.int32(1), 31 - bi)
        p_try = p | bit
        t_try = p_try ^ jnp.int32(INT_MIN)

        def cnt_body(kt, cnt):
            for c in range(n_ch):
                cnt = cnt + jnp.where(scf_ref[kt, :, c * LANES:(c + 1) * LANES] >= t_try, 1, 0)
            return cnt

        cnt = lax.fori_loop(0, n_far, cnt_body, jnp.zeros((QB, LANES), jnp.int32))
        for c in range(2):
            cnt = cnt + jnp.where(scn_ref[:, c * LANES:(c + 1) * LANES] >= t_try, 1, 0)
        total = jnp.sum(cnt, axis=1, keepdims=True)
        return jnp.where(total >= topk, p_try, p)

    p_fin = lax.fori_loop(0, 32, bit_body, jnp.zeros((QB, LANES), jnp.int32))
    thr = jnp.maximum(p_fin ^ jnp.int32(INT_MIN), jnp.int32(INT_MIN + 1))

    m_ref[...] = jnp.full(m_ref.shape, NEG_BIG, F32)
    l_ref[...] = jnp.zeros(l_ref.shape, F32)
    acc_ref[...] = jnp.zeros(acc_ref.shape, F32)

    def softmax_tile(width, near, z_ref, p_ref, al_ref):
        n_c = width // LANES
        chunks = [slice(c * LANES, (c + 1) * LANES) for c in range(n_c)]
        for h in range(H):
            rows = slice(h * QB, (h + 1) * QB)

            def logits(ch):
                v = z_ref[rows, ch] + madd_ref[:, ch]
                return v + toep_ref[h, :, ch] if near else v

            mx = logits(chunks[0])
            for ch in chunks[1:]:
                mx = jnp.maximum(mx, logits(ch))
            mx = jnp.max(mx, axis=1, keepdims=True)
            bias = 0.0 if near else bias15_ref[h]
            m_old = m_ref[rows, :]
            m_new = jnp.maximum(m_old, mx + bias)
            shift = m_new - bias
            psum = jnp.zeros((QB, LANES), F32)
            for ch in chunks:
                p = jnp.exp2(logits(ch) - shift)
                psum = psum + p
                p_ref[rows, ch] = p.astype(BF16)
            alpha = jnp.exp2(m_old - m_new)
            l_ref[rows, :] = alpha * l_ref[rows, :] + jnp.sum(psum, axis=1, keepdims=True)
            m_ref[rows, :] = m_new
            al_ref[rows, :] = alpha

    def pv_update(p_ref, al_ref, c_tile, width):
        al = al_ref[...]
        acc_ref[...] = (acc_ref[...] * jnp.concatenate([al] * (A_LATENT // LANES), axis=1)
                        + _dot(p_ref[:, :width], c_tile))

    def far_logits(z_ref, kt):
        z_ref[...] = _dot_nt(qa2_ref[...], ckv_ref[far_rows(kt), :])

    def far_softmax(slot, kt):
        thr_k = jnp.where(kt < n_far, thr, jnp.int32(2 ** 31 - 1))
        for c in range(n_ch):
            cols = slice(c * LANES, (c + 1) * LANES)
            madd_ref[:, cols] = jnp.where(scf_ref[kt, :, cols] >= thr_k, 0.0, -jnp.inf)
        softmax_tile(FAR_TILE, False, tmp[slot], pbuf[slot], albuf[slot])

    def far_pv(slot, kt):
        pv_update(pbuf[slot], albuf[slot], ckv_ref[far_rows(kt), :], FAR_TILE)

    p1_ref[...] = jnp.zeros(p1_ref.shape, BF16)
    al1_ref[...] = jnp.ones(al1_ref.shape, F32)
    far_logits(tmp[0], 0)

    def attn_pair(j, carry):
        kt = 2 * j
        far_logits(tmp[1], kt + 1)
        far_softmax(0, kt)
        far_pv(1, kt - 1)
        far_logits(tmp[0], kt + 2)
        far_softmax(1, kt + 1)
        far_pv(0, kt)
        return carry

    lax.fori_loop(0, n_pairs, attn_pair, 0)
    far_pv(1, 2 * n_pairs - 1)

    c_near = ckv_ref[pl.ds(near_start, 2 * LANES), :]
    tmp[0][:, :2 * LANES] = _dot_nt(qa2_ref[...], c_near)
    for c in range(2):
        cols = slice(c * LANES, (c + 1) * LANES)
        madd_ref[:, cols] = jnp.where(scn_ref[:, cols] >= thr, 0.0, -jnp.inf)
    softmax_tile(2 * LANES, True, tmp[0], p0_ref, al0_ref)
    pv_update(p0_ref, al0_ref, c_near, 2 * LANES)

    for h in range(H):
        rows = slice(h * QB, (h + 1) * QB)
        inv = 1.0 / l_ref[rows, :]
        o_lat = jnp.concatenate([acc_ref[rows, c * LANES:(c + 1) * LANES] * inv
                                 for c in range(A_LATENT // LANES)], axis=1).astype(BF16)
        out_ref[:, h * A_HEAD_DIM:(h + 1) * A_HEAD_DIM] = _dot(o_lat, wuv_ref[h]).astype(out_ref.dtype)


def _dsa(q_a, qidx, widx, kidx_pad, ckv_pad, toep, bias15, w_uk_t, w_uv, *, B, S):
    nqb = S // Q_BLOCK
    topk = min(TOPK_MAX, S // 4)
    n_far_max = max(1, -(-(S - 2 * Q_BLOCK) // FAR_TILE))
    HQ = A_HEADS * Q_BLOCK
    resident = dict(pipeline_mode=pl.Buffered(1))
    return pl.pallas_call(
        functools.partial(_dsa_kernel, topk=topk),
        grid=(B, nqb),
        in_specs=[
            pl.BlockSpec(memory_space=pltpu.SMEM),
            pl.BlockSpec((Q_BLOCK, A_HEADS * A_HEAD_DIM), lambda b, i: (b * nqb + i, 0)),
            pl.BlockSpec((Q_BLOCK, IDX_HEADS * IDX_DIM), lambda b, i: (b * nqb + i, 0)),
            pl.BlockSpec((Q_BLOCK, LANES), lambda b, i: (b * nqb + i, 0)),
            pl.BlockSpec((None, S + Q_BLOCK, IDX_DIM), lambda b, i: (b, 0, 0), **resident),
            pl.BlockSpec((None, S + Q_BLOCK, A_LATENT), lambda b, i: (b, 0, 0), **resident),
            pl.BlockSpec((A_HEADS, Q_BLOCK, 2 * LANES), lambda b, i: (0, 0, 0), **resident),
            pl.BlockSpec((A_HEADS, A_HEAD_DIM, A_LATENT), lambda b, i: (0, 0, 0), **resident),
            pl.BlockSpec((A_HEADS, A_LATENT, A_HEAD_DIM), lambda b, i: (0, 0, 0), **resident),
        ],
        out_specs=pl.BlockSpec((Q_BLOCK, A_HEADS * A_HEAD_DIM), lambda b, i: (b * nqb + i, 0)),
        out_shape=jax.ShapeDtypeStruct((B * S, A_HEADS * A_HEAD_DIM), BF16),
        scratch_shapes=[
            pltpu.VMEM((n_far_max + 1, Q_BLOCK, FAR_TILE), jnp.int32),
            pltpu.VMEM((Q_BLOCK, 2 * LANES), jnp.int32),
            pltpu.VMEM((HQ, FAR_TILE), F32),
            pltpu.VMEM((HQ, FAR_TILE), F32),
            pltpu.VMEM((Q_BLOCK, FAR_TILE), F32),
            pltpu.VMEM((HQ, A_LATENT), BF16),
            pltpu.VMEM((HQ, IDX_DIM), BF16),
            pltpu.VMEM((IDX_HEADS, Q_BLOCK, LANES), F32),
            pltpu.VMEM((HQ, FAR_TILE), BF16),
            pltpu.VMEM((HQ, FAR_TILE), BF16),
            pltpu.VMEM((HQ, A_LATENT), F32),
            pltpu.VMEM((HQ, LANES), F32),
            pltpu.VMEM((HQ, LANES), F32),
            pltpu.VMEM((HQ, LANES), F32),
            pltpu.VMEM((HQ, LANES), F32),
        ],
        compiler_params=_cparams("parallel", "arbitrary"),
        name="dsa",
    )(bias15, q_a, qidx, widx, kidx_pad, ckv_pad, toep, w_uk_t, w_uv)


def _ret_kernel(gtot_ref, q_ref, k_ref, v_ref, g_ref, d_ref, xi_ref, zeta_ref, gnw_ref, gnb_ref,
                o_ref, state_ref):
    @pl.when(pl.program_id(2) == 0)
    def _():
        state_ref[...] = jnp.zeros(state_ref.shape, F32)

    h = pl.program_id(1)
    q, k, v = q_ref[...], k_ref[...], v_ref[...]
    s = _dot_nt(q, k) * d_ref[0]
    o = _dot(s.astype(BF16), v) + _dot(q, state_ref[...].astype(BF16)) * xi_ref[0]
    kz = (k.astype(F32) * zeta_ref[0]).astype(BF16)
    upd = lax.dot_general(kz, v, (((0,), (0,)), ((), ())), preferred_element_type=F32)
    state_ref[...] = state_ref[...] * gtot_ref[h] + upd

    mu = jnp.mean(o, axis=-1, keepdims=True)
    d = o - mu
    var = jnp.mean(d * d, axis=-1, keepdims=True)
    y = d * lax.rsqrt(var + LN_EPS) * gnw_ref[...] + gnb_ref[...]
    g = g_ref[...]
    o_ref[...] = (g * _sigmoid(g) * y).astype(o_ref.dtype)


def _retention(q_rot, k_rot, v, g_r, gn_w, gn_b, consts, *, B, S):
    d_mat, xi, zeta, gtot = consts
    ng = S // RET_GROUP
    G = RET_GROUP
    return pl.pallas_call(
        _ret_kernel,
        grid=(B, R_HEADS, ng),
        in_specs=[
            pl.BlockSpec(memory_space=pltpu.SMEM),
            pl.BlockSpec((G, R_QK_DIM), lambda b, h, g: (b * ng + g, h)),
            pl.BlockSpec((G, R_QK_DIM), lambda b, h, g: (b * ng + g, h)),
            pl.BlockSpec((G, R_V_DIM), lambda b, h, g: (b * ng + g, h)),
            pl.BlockSpec((G, R_V_DIM), lambda b, h, g: (b * ng + g, h)),
            pl.BlockSpec((1, G, G), lambda b, h, g: (h, 0, 0)),
            pl.BlockSpec((1, G, 1), lambda b, h, g: (h, 0, 0)),
            pl.BlockSpec((1, G, 1), lambda b, h, g: (h, 0, 0)),
            pl.BlockSpec((1, R_V_DIM), lambda b, h, g: (0, h)),
            pl.BlockSpec((1, R_V_DIM), lambda b, h, g: (0, h)),
        ],
        out_specs=pl.BlockSpec((G, R_V_DIM), lambda b, h, g: (b * ng + g, h)),
        out_shape=jax.ShapeDtypeStruct((B * S, R_HEADS * R_V_DIM), BF16),
        scratch_shapes=[pltpu.VMEM((R_QK_DIM, R_V_DIM), F32)],
        compiler_params=_cparams("parallel", "parallel", "arbitrary"),
        name="retention",
    )(gtot, q_rot, k_rot, v, g_r, d_mat, xi, zeta, gn_w, gn_b)


def _retention_consts():
    G = RET_GROUP
    log_g = jnp.log1p(-jnp.exp2(-5.0 - jnp.arange(R_HEADS, dtype=F32)))
    pos = jnp.arange(G, dtype=F32)
    diff = pos[:, None] - pos[None, :]
    ci = jnp.arange(G)[:, None] // CHUNK
    cj = jnp.arange(G)[None, :] // CHUNK
    same = jnp.exp(log_g[:, None, None] * jnp.abs(diff))
    earlier = jnp.exp(log_g[:, None, None] * diff)
    d_mat = jnp.where(ci == cj, same, jnp.where(cj < ci, earlier, 0.0))
    xi = jnp.exp(log_g[:, None] * (pos[None, :] + 1.0))[..., None]
    zeta = jnp.exp(log_g[:, None] * (G - 1.0 - pos[None, :]))[..., None]
    gtot = jnp.exp(log_g * G)
    return d_mat, xi, zeta, gtot


def _rank_rows(v):
    n = v.shape[0]
    ridx = lax.broadcasted_iota(jnp.int32, v.shape, 0)
    rank = jnp.zeros(v.shape, jnp.int32)
    for j in range(n):
        rj = v[j:j + 1, :]
        rank = rank + jnp.where(ridx > j, jnp.where(rj >= v, 1, 0), jnp.where(rj > v, 1, 0))
    return rank


def _router_kernel(x_ref, wr_ref, b_ref, gt_ref):
    st = _dot_nt(wr_ref[...], x_ref[...])
    sig = _sigmoid(st)
    biased = sig + b_ref[...]
    per = N_EXPERTS // N_GROUPS
    blocks = [biased[g * per:(g + 1) * per, :] for g in range(N_GROUPS)]
    gscore = []
    for blk in blocks:
        top2 = jnp.where(_rank_rows(blk) < 2, blk, 0.0)
        gscore.append(jnp.sum(top2, axis=0, keepdims=True))
    masked = []
    for g in range(N_GROUPS):
        grank = jnp.zeros(gscore[g].shape, jnp.int32)
        for g2 in range(N_GROUPS):
            if g2 == g:
                continue
            beats = (gscore[g2] >= gscore[g]) if g2 < g else (gscore[g2] > gscore[g])
            grank = grank + jnp.where(beats, 1, 0)
        keep = jnp.broadcast_to(grank, blocks[g].shape) < TOPK_GROUPS
        masked.append(jnp.where(keep, blocks[g], -jnp.inf))
    cand = jnp.concatenate(masked, axis=0)
    sel = _rank_rows(cand) < TOP_K
    gates = jnp.where(sel, sig, 0.0)
    denom = jnp.sum(gates, axis=0, keepdims=True)
    gt_ref[...] = gates / denom * ROUTED_SCALE


def _router(xb, wr_t, b_col, *, tm):
    N = xb.shape[0]
    return pl.pallas_call(
        _router_kernel,
        grid=(N // tm,),
        in_specs=[
            pl.BlockSpec((tm, xb.shape[1]), lambda i: (i, 0)),
            pl.BlockSpec(wr_t.shape, lambda i: (0, 0)),
            pl.BlockSpec(b_col.shape, lambda i: (0, 0)),
        ],
        out_specs=pl.BlockSpec((N_EXPERTS, tm), lambda i: (0, i)),
        out_shape=jax.ShapeDtypeStruct((N_EXPERTS, N), F32),
        compiler_params=_cparams("parallel"),
        name="router",
    )(xb, wr_t, b_col)


def _moe_kernel(x_ref, g_ref, wg_ref, wu_ref, wd_ref, lnw_ref, lnb_ref, of_ref, ob_ref,
                xb_ref, acc_ref, *, alpha):
    e = pl.program_id(1)

    @pl.when(e == 0)
    def _():
        xb_ref[...] = x_ref[...].astype(BF16)
        acc_ref[...] = jnp.zeros(acc_ref.shape, F32)

    xb = xb_ref[...]
    hg = _dot(xb, wg_ref[0])
    hu = _dot(xb, wu_ref[0])
    hidden = (hg * _sigmoid(hg) * hu).astype(BF16)
    lane = lax.broadcasted_iota(jnp.int32, g_ref.shape, 1)
    gate = jnp.sum(jnp.where(lane == e, g_ref[...], 0.0), axis=1, keepdims=True)
    acc_ref[...] += _dot(hidden, wd_ref[0]) * gate

    @pl.when(e == pl.num_programs(1) - 1)
    def _():
        y = _layer_norm_rows(alpha * x_ref[...] + acc_ref[...], lnw_ref[...], lnb_ref[...])
        of_ref[...] = y
        ob_ref[...] = y.astype(BF16)


def _moe(x1, gates, wg, wu, wd, ln_w, ln_b, *, alpha, tm):
    N, D = x1.shape
    n_e = wg.shape[0]
    return pl.pallas_call(
        functools.partial(_moe_kernel, alpha=alpha),
        grid=(N // tm, n_e),
        in_specs=[
            pl.BlockSpec((tm, D), lambda i, e: (i, 0)),
            pl.BlockSpec((tm, LANES), lambda i, e: (i, 0)),
            pl.BlockSpec((1, D, EXPERT_DIM), lambda i, e: (e, 0, 0)),
            pl.BlockSpec((1, D, EXPERT_DIM), lambda i, e: (e, 0, 0)),
            pl.BlockSpec((1, EXPERT_DIM, D), lambda i, e: (e, 0, 0)),
            pl.BlockSpec((1, D), lambda i, e: (0, 0)),
            pl.BlockSpec((1, D), lambda i, e: (0, 0)),
        ],
        out_specs=[pl.BlockSpec((tm, D), lambda i, e: (i, 0)),
                   pl.BlockSpec((tm, D), lambda i, e: (i, 0))],
        out_shape=[jax.ShapeDtypeStruct((N, D), F32), jax.ShapeDtypeStruct((N, D), BF16)],
        scratch_shapes=[pltpu.VMEM((tm, D), BF16), pltpu.VMEM((tm, D), F32)],
        compiler_params=_cparams("parallel", "arbitrary"),
        name="moe",
    )(x1, gates, wg, wu, wd, ln_w, ln_b)


def _t5_bucket(rel):
    half = REL_BUCKETS // 2
    max_exact = half // 2
    ret = jnp.where(rel > 0, half, 0)
    n = jnp.abs(rel)
    nf = jnp.maximum(n, 1).astype(F32)
    large = max_exact + (jnp.log(nf / max_exact) / math.log(REL_MAX_DIST / max_exact)
                         * (half - max_exact)).astype(jnp.int32)
    large = jnp.minimum(large, half - 1)
    return ret + jnp.where(n < max_exact, n, large)


def _bias_tables(rel_bias):
    i = jnp.arange(Q_BLOCK)[:, None]
    u = jnp.arange(2 * Q_BLOCK)[None, :]
    toep = jnp.transpose(rel_bias[_t5_bucket(u - Q_BLOCK - i)], (2, 0, 1))
    far = rel_bias[_t5_bucket(jnp.asarray(-(Q_BLOCK + 1)))]
    return toep.astype(F32) * LOG2E, far.astype(F32) * LOG2E


def _rope_tables(S):
    half = R_QK_DIM // 2
    inv = ROPE_BASE ** (-jnp.arange(half, dtype=F32) / half)
    ang = jnp.arange(S, dtype=F32)[:, None] * inv[None, :]
    return jnp.cos(ang), jnp.sin(ang)


def _layer(x, xb, tables, p, *, B, S, alpha):
    N, D = x.shape
    cos, sin, toep, bias15, ret_consts = tables
    TM = 512 if N % 512 == 0 else 256
    a_w = A_HEADS * A_HEAD_DIM
    cuts = np.cumsum([0, a_w, A_LATENT, IDX_HEADS * IDX_DIM, IDX_DIM, IDX_HEADS,
                      R_HEADS * R_QK_DIM, R_HEADS * R_QK_DIM, R_HEADS * R_V_DIM, R_HEADS * R_V_DIM, D, D])
    w_in = p['w_in']
    piece = lambda k: w_in[:, int(cuts[k]):int(cuts[k + 1])].astype(BF16)
    row_spec = lambda w: pl.BlockSpec((1, w), lambda i, j: (0, 0))

    TMB = 1024 if (N % 1024 == 0 and S % 1024 == 0) else TM
    big = dict(tm=TMB, tn=1024, epilogue=_epi_plain)
    q_a = _mm(xb, piece(0), out_dtypes=[BF16], name="proj_qa", **big)
    c_kv = _mm(xb, piece(1), tm=TMB, tn=A_LATENT, out_dtypes=[BF16], epilogue=_epi_rms,
               extras=[p['ckv_norm'][None, :]], extra_specs=[row_spec(A_LATENT)], name="proj_ckv")
    q_idx = _mm(xb, piece(2), out_dtypes=[BF16], name="proj_qidx", **big)
    k_idx = _mm(xb, piece(3), tm=TMB, tn=IDX_DIM, out_dtypes=[BF16], epilogue=_epi_rms,
                extras=[p['kidx_norm'][None, :]], extra_specs=[row_spec(IDX_DIM)], name="proj_kidx")
    w_widx = jnp.pad(piece(4), ((0, 0), (0, LANES - IDX_HEADS)))
    w_idx = _mm(xb, w_widx, tm=TMB, tn=LANES, out_dtypes=[F32],
                epilogue=functools.partial(_epi_scale, scale=IDX_HEADS ** -0.5 * IDX_DIM ** -0.5),
                name="proj_widx")
    pos_spec = pl.BlockSpec((TMB, R_QK_DIM // 2), lambda i, j: (i % (S // TMB), 0))
    q_r = _mm(xb, piece(5), tm=TMB, tn=2 * R_QK_DIM, out_dtypes=[BF16],
              epilogue=functools.partial(_epi_rope, scale=1.0),
              extras=[cos, sin], extra_specs=[pos_spec, pos_spec], name="proj_qr")
    k_r = _mm(xb, piece(6), tm=TMB, tn=2 * R_QK_DIM, out_dtypes=[BF16],
              epilogue=functools.partial(_epi_rope, scale=R_QK_DIM ** -0.5),
              extras=[cos, sin], extra_specs=[pos_spec, pos_spec], name="proj_kr")
    v_r = _mm(xb, piece(7), out_dtypes=[BF16], name="proj_vr", **big)
    g_r = _mm(xb, piece(8), out_dtypes=[F32], name="proj_gr", **big)
    g_a = _mm(xb, piece(9), out_dtypes=[F32], name="proj_ga", **big)
    g_b = _mm(xb, piece(10), out_dtypes=[F32], name="proj_gb", **big)

    w_uk_t = jnp.transpose(p['w_uk'], (0, 2, 1)).astype(BF16)
    pad = lambda a: jnp.pad(a.reshape(B, S, -1), ((0, 0), (Q_BLOCK, 0), (0, 0)))
    y_a = _dsa(q_a, q_idx, w_idx, pad(k_idx), pad(c_kv), toep, bias15, w_uk_t, p['w_uv'].astype(BF16),
               B=B, S=S)

    y_r = _retention(q_r, k_r, v_r, g_r, p['gn_w'][None, :], p['gn_b'][None, :], ret_consts, B=B, S=S)

    tile_spec = pl.BlockSpec((TM, 1024), lambda i, j: (i, j))
    m_a = _mm(y_a, p['w_pa'].astype(BF16), tm=TM, tn=1024, out_dtypes=[F32], epilogue=_epi_gate,
              extras=[g_a], extra_specs=[tile_spec], name="proj_a")
    merged = _mm(y_r, p['w_pb'].astype(BF16), tm=TM, tn=1024, out_dtypes=[BF16], epilogue=_epi_gate_add,
                 extras=[g_b, m_a], extra_specs=[tile_spec, tile_spec], name="proj_b")
    full_row = pl.BlockSpec((TM, D), lambda i, j: (i, 0))
    x1, x1b = _mm(merged, p['w_o'].astype(BF16), tm=TM, tn=D, out_dtypes=[F32, BF16],
                  epilogue=functools.partial(_epi_ln, alpha=alpha),
                  extras=[x, p['ln1_w'][None, :], p['ln1_b'][None, :]],
                  extra_specs=[full_row, row_spec(D), row_spec(D)], name="out_ln1")

    gt = _router(x1b, p['w_router'].T.astype(BF16), p['b_router'][:, None], tm=TM)
    gates = jnp.concatenate([gt.T, jnp.ones((N, 1), F32), jnp.zeros((N, LANES - N_EXPERTS - 1), F32)], axis=1)
    wg = jnp.concatenate([p['we_gate'], p['ws_gate'][None]], axis=0).astype(BF16)
    wu = jnp.concatenate([p['we_up'], p['ws_up'][None]], axis=0).astype(BF16)
    wd = jnp.concatenate([p['we_down'], p['ws_down'][None]], axis=0).astype(BF16)
    return _moe(x1, gates, wg, wu, wd, p['ln2_w'][None, :], p['ln2_b'][None, :], alpha=alpha, tm=TM)


def kernel(x, rel_bias, w_in, ckv_norm, kidx_norm, w_uk, w_uv, gn_w, gn_b, w_pa, w_pb, w_o, ln1_w, ln1_b,
           w_router, b_router, we_gate, we_up, we_down, ws_gate, ws_up, ws_down, ln2_w, ln2_b):
    B, S, D = x.shape
    depth = w_in.shape[0]
    alpha = (2 * depth) ** 0.25
    cos, sin = _rope_tables(S)
    toep, bias15 = _bias_tables(rel_bias)
    tables = (cos, sin, toep, bias15, _retention_consts())
    params = dict(w_in=w_in, ckv_norm=ckv_norm, kidx_norm=kidx_norm, w_uk=w_uk, w_uv=w_uv, gn_w=gn_w,
                  gn_b=gn_b, w_pa=w_pa, w_pb=w_pb, w_o=w_o, ln1_w=ln1_w, ln1_b=ln1_b, w_router=w_router,
                  b_router=b_router, we_gate=we_gate, we_up=we_up, we_down=we_down, ws_gate=ws_gate,
                  ws_up=ws_up, ws_down=ws_down, ln2_w=ln2_w, ln2_b=ln2_b)
    xf = x.reshape(B * S, D)
    xb = xf.astype(BF16)
    for l in range(depth):
        xf, xb = _layer(xf, xb, tables, {k: v[l] for k, v in params.items()}, B=B, S=S, alpha=alpha)
    return xf.reshape(B, S, D)
```

```python
import functools
import math

import jax
import jax.numpy as jnp
import numpy as np
from jax import lax
from jax.experimental import pallas as pl
from jax.experimental.pallas import tpu as pltpu

CHUNK = 64
Q_BLOCK = 128
A_HEADS = 16
A_HEAD_DIM = 128
A_LATENT = 256
IDX_HEADS = 16
IDX_DIM = 128
TOPK_MAX = 256
REL_BUCKETS = 32
REL_MAX_DIST = 128
R_HEADS = 8
R_QK_DIM = 256
R_V_DIM = 512
ROPE_BASE = 10000.0
N_EXPERTS = 64
EXPERT_DIM = 256
TOP_K = 8
N_GROUPS = 8
TOPK_GROUPS = 4
ROUTED_SCALE = 2.5
LN_EPS = 1e-5
RMS_EPS = 1e-6

LANES = 128
FAR_TILE = 512
HEAD_GROUP = 4
RET_GROUP = 512
VMEM_LIMIT = 56 * 1024 * 1024
INT_MIN = -2 ** 31
NEG_BIG = -1e30
LOG2E = 1.4426950408889634

F32 = jnp.float32
BF16 = jnp.bfloat16


def _cparams(*sem):
    return pltpu.CompilerParams(dimension_semantics=sem, vmem_limit_bytes=VMEM_LIMIT)


def _sigmoid(x):
    return 1.0 / (1.0 + jnp.exp(-x))


def _dot(a, b):
    return jnp.dot(a, b, preferred_element_type=F32)


def _dot_nt(a, b):
    return lax.dot_general(a, b, (((1,), (1,)), ((), ())), preferred_element_type=F32)


def _layer_norm_rows(v, w, b):
    mu = jnp.mean(v, axis=-1, keepdims=True)
    d = v - mu
    var = jnp.mean(d * d, axis=-1, keepdims=True)
    return d * lax.rsqrt(var + LN_EPS) * w + b


def _mm_kernel(*refs, epilogue, n_extra, n_out):
    a_ref, b_ref = refs[0], refs[1]
    extra = refs[2:2 + n_extra]
    outs = refs[2 + n_extra:2 + n_extra + n_out]
    acc = _dot(a_ref[...], b_ref[...])
    res = epilogue(acc, *extra)
    if not isinstance(res, tuple):
        res = (res,)
    for o_ref, r in zip(outs, res):
        o_ref[...] = r.astype(o_ref.dtype)


def _mm(a, b, *, tm, tn, out_dtypes, epilogue, extras=(), extra_specs=(), name):
    M = a.shape[0]
    K, n_cols = b.shape
    grid = (M // tm, n_cols // tn)
    a_spec = pl.BlockSpec((tm, K), lambda i, j: (i, 0))
    b_spec = pl.BlockSpec((K, tn), lambda i, j: (0, j))
    out_shape = [jax.ShapeDtypeStruct((M, n_cols), dt) for dt in out_dtypes]
    out_specs = [pl.BlockSpec((tm, tn), lambda i, j: (i, j)) for _ in out_dtypes]
    res = pl.pallas_call(
        functools.partial(_mm_kernel, epilogue=epilogue, n_extra=len(extras), n_out=len(out_dtypes)),
        grid=grid,
        in_specs=[a_spec, b_spec, *extra_specs],
        out_specs=out_specs,
        out_shape=out_shape,
        compiler_params=_cparams("parallel", "arbitrary"),
        name=name,
    )(a, b, *extras)
    return res[0] if len(res) == 1 else res


def _epi_plain(acc):
    return acc


def _epi_scale(acc, *, scale):
    return acc * scale


def _epi_rms(acc, w_ref):
    return acc * lax.rsqrt(jnp.mean(acc * acc, axis=-1, keepdims=True) + RMS_EPS) * w_ref[...]


def _epi_rope(acc, cos_ref, sin_ref, *, scale):
    half = R_QK_DIM // 2
    c, s = cos_ref[...] * scale, sin_ref[...] * scale
    out = []
    for h in range(acc.shape[-1] // R_QK_DIM):
        x1 = acc[:, h * R_QK_DIM:h * R_QK_DIM + half]
        x2 = acc[:, h * R_QK_DIM + half:(h + 1) * R_QK_DIM]
        out += [x1 * c - x2 * s, x1 * s + x2 * c]
    return jnp.concatenate(out, axis=-1)


def _epi_gate(acc, g_ref):
    return _sigmoid(g_ref[...]) * acc


def _epi_gate_add(acc, g_ref, add_ref):
    return _sigmoid(g_ref[...]) * acc + add_ref[...]


def _epi_ln(acc, res_ref, w_ref, b_ref, *, alpha):
    y = _layer_norm_rows(alpha * res_ref[...] + acc, w_ref[...], b_ref[...])
    return y, y


def _float_key(x):
    bits = pltpu.bitcast(x, jnp.int32)
    return bits ^ ((bits >> 31) & jnp.int32(0x7FFFFFFF))


def _dsa_kernel(bias15_ref, qa_ref, qidx_ref, widx_ref, kidx_ref, ckv_ref, toep_ref, wuk_ref, wuv_ref,
                out_ref, scf_ref, scn_ref, tmp0_ref, tmp1_ref, zb0_ref, zb1_ref, madd_ref, qa2_ref, qi2_ref,
                wb_ref, p0_ref, p1_ref, acc_ref, m_ref, l_ref, al0_ref, al1_ref, *, topk):
    i = pl.program_id(1)
    t0 = i * Q_BLOCK
    far_end = t0 - Q_BLOCK
    n_far = (jnp.maximum(i - 1, 0) * Q_BLOCK + FAR_TILE - 1) // FAR_TILE
    n_ch = FAR_TILE // LANES
    H = A_HEADS
    QB = Q_BLOCK

    for h in range(H):
        q_abs = _dot(qa_ref[:, h * A_HEAD_DIM:(h + 1) * A_HEAD_DIM], wuk_ref[h])
        qa2_ref[h * QB:(h + 1) * QB, :] = (q_abs * (A_HEAD_DIM ** -0.5 * LOG2E)).astype(BF16)
        qi2_ref[h * QB:(h + 1) * QB, :] = qidx_ref[:, h * IDX_DIM:(h + 1) * IDX_DIM]
        wb_ref[h] = jnp.broadcast_to(widx_ref[:, h:h + 1], (QB, LANES))

    lane = lax.broadcasted_iota(jnp.int32, (QB, LANES), 1)
    row = lax.broadcasted_iota(jnp.int32, (QB, LANES), 0)

    tmp = (tmp0_ref, tmp1_ref)
    zbuf = (zb0_ref, zb1_ref)
    pbuf = (p0_ref, p1_ref)
    albuf = (al0_ref, al1_ref)
    n_pairs = (n_far + 1) // 2
    last_start = kidx_ref.shape[0] - FAR_TILE

    def far_rows(kt):
        return pl.ds(pl.multiple_of(jnp.clip(Q_BLOCK + kt * FAR_TILE, Q_BLOCK, last_start), LANES), FAR_TILE)

    def head_sum(z_ref, c):
        acc = jnp.zeros((QB, LANES), F32)
        for h in range(H):
            z = z_ref[h * QB:(h + 1) * QB, c * LANES:(c + 1) * LANES]
            acc = acc + wb_ref[h] * jnp.maximum(z, 0.0)
        return acc

    def far_scores(z_ref, kt):
        for c in range(n_ch):
            key = _float_key(head_sum(z_ref, c))
            s_pos = kt * FAR_TILE + c * LANES + lane
            scf_ref[kt, :, c * LANES:(c + 1) * LANES] = jnp.where(s_pos < far_end, key, INT_MIN)

    def idx_matmul(z_ref, kt):
        z_ref[...] = _dot_nt(qi2_ref[...], kidx_ref[far_rows(kt), :])

    idx_matmul(tmp[0], 0)

    def score_pair(j, carry):
        kt = 2 * j
        idx_matmul(tmp[1], kt + 1)
        far_scores(tmp[0], kt)
        idx_matmul(tmp[0], kt + 2)
        far_scores(tmp[1], kt + 1)
        return carry

    lax.fori_loop(0, n_pairs, score_pair, 0)

    near_start = pl.multiple_of(t0, LANES)
    tmp[0][:, :2 * LANES] = _dot_nt(qi2_ref[...], kidx_ref[pl.ds(near_start, 2 * LANES), :])
    u_lo = jnp.where(i == 0, QB, 0)
    u_hi = jnp.where(row < CHUNK, QB + CHUNK, 2 * QB)
    for c in range(2):
        key = _float_key(head_sum(tmp[0], c))
        u = c * LANES + lane
        vis = jnp.logical_and(u >= u_lo, u < u_hi)
        scn_ref[:, c * LANES:(c + 1) * LANES] = jnp.where(vis, key, INT_MIN)

    def bit_body(bi, p):
        bit = jnp.left_shift(jnp.int32(1), 31 - bi)
        p_try = p | bit
        t_try = p_try ^ jnp.int32(INT_MIN)

        def cnt_body(kt, cnt):
            for c in range(n_ch):
                cnt = cnt + jnp.where(scf_ref[kt, :, c * LANES:(c + 1) * LANES] >= t_try, 1, 0)
            return cnt

        cnt = lax.fori_loop(0, n_far, cnt_body, jnp.zeros((QB, LANES), jnp.int32))
        for c in range(2):
            cnt = cnt + jnp.where(scn_ref[:, c * LANES:(c + 1) * LANES] >= t_try, 1, 0)
        total = jnp.sum(cnt, axis=1, keepdims=True)
        return jnp.where(total >= topk, p_try, p)

    p_fin = lax.fori_loop(0, 32, bit_body, jnp.zeros((QB, LANES), jnp.int32))
    thr = jnp.maximum(p_fin ^ jnp.int32(INT_MIN), jnp.int32(INT_MIN + 1))

    m_ref[...] = jnp.full(m_ref.shape, NEG_BIG, F32)
    l_ref[...] = jnp.zeros(l_ref.shape, F32)
    acc_ref[...] = jnp.zeros(acc_ref.shape, F32)

    def to_mask(kt, carry):
        for c in range(n_ch):
            cols = slice(c * LANES, (c + 1) * LANES)
            mask = jnp.where(scf_ref[kt, :, cols] >= thr, 0.0, -jnp.inf)
            scf_ref[kt, :, cols] = pltpu.bitcast(mask, jnp.int32)
        return carry

    lax.fori_loop(0, n_far, to_mask, 0)
    for c in range(2):
        cols = slice(c * LANES, (c + 1) * LANES)
        madd_ref[:, cols] = jnp.where(scn_ref[:, cols] >= thr, 0.0, -jnp.inf)

    GR = HEAD_GROUP * QB
    n_grp = H // HEAD_GROUP

    def aligned(start, size):
        return pl.ds(start if isinstance(start, int) else pl.multiple_of(start, size), size)

    def group_rows(g):
        return aligned((g % n_grp) * GR, GR)

    def softmax_group(width, near, z_ref, p_ref, al_ref, g):
        kt, hg = g // n_grp, g % n_grp
        chunks = [slice(c * LANES, (c + 1) * LANES) for c in range(width // LANES)]
        for hh in range(HEAD_GROUP):
            h = hg * HEAD_GROUP + hh
            rows = slice(hh * QB, (hh + 1) * QB)
            stat = aligned(h * QB, QB)

            def masked_logits(ch):
                if near:
                    v = z_ref[rows, ch] + madd_ref[:, ch] + toep_ref[h, :, ch]
                else:
                    v = z_ref[rows, ch] + pltpu.bitcast(scf_ref[kt, :, ch], F32)
                z_ref[rows, ch] = v
                return v

            mx = masked_logits(chunks[0])
            for ch in chunks[1:]:
                mx = jnp.maximum(mx, masked_logits(ch))
            mx = jnp.max(mx, axis=1, keepdims=True)
            bias = 0.0 if near else bias15_ref[h]
            m_old = m_ref[stat, :]
            m_new = jnp.maximum(m_old, mx + bias)
            shift = m_new - bias
            psum = jnp.zeros((QB, LANES), F32)
            for ch in chunks:
                p = jnp.exp2(z_ref[rows, ch] - shift)
                psum = psum + p
                p_ref[rows, ch] = p.astype(BF16)
            alpha = jnp.exp2(m_old - m_new)
            l_ref[stat, :] = alpha * l_ref[stat, :] + jnp.sum(psum, axis=1, keepdims=True)
            m_ref[stat, :] = m_new
            al_ref[rows, :] = alpha

    def pv_group(p_ref, al_ref, c_tile, width, g):
        al = al_ref[...]
        rows = group_rows(g)
        acc_ref[rows, :] = (acc_ref[rows, :] * jnp.concatenate([al] * (A_LATENT // LANES), axis=1)
                            + _dot(p_ref[:, :width], c_tile))

    def far_logits(z_ref, g):
        z_ref[...] = _dot_nt(qa2_ref[group_rows(g), :], ckv_ref[far_rows(g // n_grp), :])

    def far_pv(slot, g):
        pv_group(pbuf[slot], albuf[slot], ckv_ref[far_rows(g // n_grp), :], FAR_TILE, g)

    p1_ref[...] = jnp.zeros(p1_ref.shape, BF16)
    al1_ref[...] = jnp.ones(al1_ref.shape, F32)
    far_logits(zbuf[0], 0)

    def attn_pair(j, carry):
        g = 2 * j
        far_logits(zbuf[1], g + 1)
        softmax_group(FAR_TILE, False, zbuf[0], pbuf[0], albuf[0], g)
        far_pv(1, g - 1)
        far_logits(zbuf[0], g + 2)
        softmax_group(FAR_TILE, False, zbuf[1], pbuf[1], albuf[1], g + 1)
        far_pv(0, g)
        return carry

    n_units = n_far * n_grp
    lax.fori_loop(0, n_units // 2, attn_pair, 0)
    far_pv(1, n_units - 1)

    c_near = ckv_ref[pl.ds(near_start, 2 * LANES), :]

    for g in range(n_grp):
        s = g % 2
        zbuf[s][:, :2 * LANES] = _dot_nt(qa2_ref[g * GR:(g + 1) * GR, :], c_near)
        softmax_group(2 * LANES, True, zbuf[s], pbuf[s], albuf[s], g)
        pv_group(pbuf[s], albuf[s], c_near, 2 * LANES, g)

    for h in range(H):
        rows = slice(h * QB, (h + 1) * QB)
        inv = 1.0 / l_ref[rows, :]
        o_lat = jnp.concatenate([acc_ref[rows, c * LANES:(c + 1) * LANES] * inv
                                 for c in range(A_LATENT // LANES)], axis=1).astype(BF16)
        out_ref[:, h * A_HEAD_DIM:(h + 1) * A_HEAD_DIM] = _dot(o_lat, wuv_ref[h]).astype(out_ref.dtype)


def _dsa(q_a, qidx, widx, kidx_pad, ckv_pad, toep, bias15, w_uk_t, w_uv, *, B, S):
    nqb = S // Q_BLOCK
    topk = min(TOPK_MAX, S // 4)
    n_far_max = max(1, -(-(S - 2 * Q_BLOCK) // FAR_TILE))
    HQ = A_HEADS * Q_BLOCK
    GR = HEAD_GROUP * Q_BLOCK
    resident = dict(pipeline_mode=pl.Buffered(1))
    return pl.pallas_call(
        functools.partial(_dsa_kernel, topk=topk),
        grid=(B, nqb),
        in_specs=[
            pl.BlockSpec(memory_space=pltpu.SMEM),
            pl.BlockSpec((Q_BLOCK, A_HEADS * A_HEAD_DIM), lambda b, i: (b * nqb + i, 0)),
            pl.BlockSpec((Q_BLOCK, IDX_HEADS * IDX_DIM), lambda b, i: (b * nqb + i, 0)),
            pl.BlockSpec((Q_BLOCK, LANES), lambda b, i: (b * nqb + i, 0)),
            pl.BlockSpec((None, S + Q_BLOCK, IDX_DIM), lambda b, i: (b, 0, 0), **resident),
            pl.BlockSpec((None, S + Q_BLOCK, A_LATENT), lambda b, i: (b, 0, 0), **resident),
            pl.BlockSpec((A_HEADS, Q_BLOCK, 2 * LANES), lambda b, i: (0, 0, 0), **resident),
            pl.BlockSpec((A_HEADS, A_HEAD_DIM, A_LATENT), lambda b, i: (0, 0, 0), **resident),
            pl.BlockSpec((A_HEADS, A_LATENT, A_HEAD_DIM), lambda b, i: (0, 0, 0), **resident),
        ],
        out_specs=pl.BlockSpec((Q_BLOCK, A_HEADS * A_HEAD_DIM), lambda b, i: (b * nqb + i, 0)),
        out_shape=jax.ShapeDtypeStruct((B * S, A_HEADS * A_HEAD_DIM), BF16),
        scratch_shapes=[
            pltpu.VMEM((n_far_max + 1, Q_BLOCK, FAR_TILE), jnp.int32),
            pltpu.VMEM((Q_BLOCK, 2 * LANES), jnp.int32),
            pltpu.VMEM((HQ, FAR_TILE), F32),
            pltpu.VMEM((HQ, FAR_TILE), F32),
            pltpu.VMEM((GR, FAR_TILE), F32),
            pltpu.VMEM((GR, FAR_TILE), F32),
            pltpu.VMEM((Q_BLOCK, 2 * LANES), F32),
            pltpu.VMEM((HQ, A_LATENT), BF16),
            pltpu.VMEM((HQ, IDX_DIM), BF16),
            pltpu.VMEM((IDX_HEADS, Q_BLOCK, LANES), F32),
            pltpu.VMEM((GR, FAR_TILE), BF16),
            pltpu.VMEM((GR, FAR_TILE), BF16),
            pltpu.VMEM((HQ, A_LATENT), F32),
            pltpu.VMEM((HQ, LANES), F32),
            pltpu.VMEM((HQ, LANES), F32),
            pltpu.VMEM((GR, LANES), F32),
            pltpu.VMEM((GR, LANES), F32),
        ],
        compiler_params=_cparams("parallel", "arbitrary"),
        name="dsa",
    )(bias15, q_a, qidx, widx, kidx_pad, ckv_pad, toep, w_uk_t, w_uv)


def _ret_kernel(gtot_ref, q_ref, k_ref, v_ref, g_ref, d_ref, xi_ref, zeta_ref, gnw_ref, gnb_ref,
                o_ref, state_ref):
    @pl.when(pl.program_id(2) == 0)
    def _():
        state_ref[...] = jnp.zeros(state_ref.shape, F32)

    h = pl.program_id(1)
    q, k, v = q_ref[...], k_ref[...], v_ref[...]
    s = _dot_nt(q, k) * d_ref[0]
    o = _dot(s.astype(BF16), v) + _dot(q, state_ref[...].astype(BF16)) * xi_ref[0]
    kz = (k.astype(F32) * zeta_ref[0]).astype(BF16)
    upd = lax.dot_general(kz, v, (((0,), (0,)), ((), ())), preferred_element_type=F32)
    state_ref[...] = state_ref[...] * gtot_ref[h] + upd

    mu = jnp.mean(o, axis=-1, keepdims=True)
    d = o - mu
    var = jnp.mean(d * d, axis=-1, keepdims=True)
    y = d * lax.rsqrt(var + LN_EPS) * gnw_ref[...] + gnb_ref[...]
    g = g_ref[...]
    o_ref[...] = (g * _sigmoid(g) * y).astype(o_ref.dtype)


def _retention(q_rot, k_rot, v, g_r, gn_w, gn_b, consts, *, B, S):
    d_mat, xi, zeta, gtot = consts
    ng = S // RET_GROUP
    G = RET_GROUP
    return pl.pallas_call(
        _ret_kernel,
        grid=(B, R_HEADS, ng),
        in_specs=[
            pl.BlockSpec(memory_space=pltpu.SMEM),
            pl.BlockSpec((G, R_QK_DIM), lambda b, h, g: (b * ng + g, h)),
            pl.BlockSpec((G, R_QK_DIM), lambda b, h, g: (b * ng + g, h)),
            pl.BlockSpec((G, R_V_DIM), lambda b, h, g: (b * ng + g, h)),
            pl.BlockSpec((G, R_V_DIM), lambda b, h, g: (b * ng + g, h)),
            pl.BlockSpec((1, G, G), lambda b, h, g: (h, 0, 0)),
            pl.BlockSpec((1, G, 1), lambda b, h, g: (h, 0, 0)),
            pl.BlockSpec((1, G, 1), lambda b, h, g: (h, 0, 0)),
            pl.BlockSpec((1, R_V_DIM), lambda b, h, g: (0, h)),
            pl.BlockSpec((1, R_V_DIM), lambda b, h, g: (0, h)),
        ],
        out_specs=pl.BlockSpec((G, R_V_DIM), lambda b, h, g: (b * ng + g, h)),
        out_shape=jax.ShapeDtypeStruct((B * S, R_HEADS * R_V_DIM), BF16),
        scratch_shapes=[pltpu.VMEM((R_QK_DIM, R_V_DIM), F32)],
        compiler_params=_cparams("parallel", "parallel", "arbitrary"),
        name="retention",
    )(gtot, q_rot, k_rot, v, g_r, d_mat, xi, zeta, gn_w, gn_b)


def _retention_consts():
    G = RET_GROUP
    log_g = jnp.log1p(-jnp.exp2(-5.0 - jnp.arange(R_HEADS, dtype=F32)))
    pos = jnp.arange(G, dtype=F32)
    diff = pos[:, None] - pos[None, :]
    ci = jnp.arange(G)[:, None] // CHUNK
    cj = jnp.arange(G)[None, :] // CHUNK
    same = jnp.exp(log_g[:, None, None] * jnp.abs(diff))
    earlier = jnp.exp(log_g[:, None, None] * diff)
    d_mat = jnp.where(ci == cj, same, jnp.where(cj < ci, earlier, 0.0))
    xi = jnp.exp(log_g[:, None] * (pos[None, :] + 1.0))[..., None]
    zeta = jnp.exp(log_g[:, None] * (G - 1.0 - pos[None, :]))[..., None]
    gtot = jnp.exp(log_g * G)
    return d_mat, xi, zeta, gtot


def _rank_rows(v):
    n = v.shape[0]
    ridx = lax.broadcasted_iota(jnp.int32, v.shape, 0)
    rank = jnp.zeros(v.shape, jnp.int32)
    for j in range(n):
        rj = v[j:j + 1, :]
        rank = rank + jnp.where(ridx > j, jnp.where(rj >= v, 1, 0), jnp.where(rj > v, 1, 0))
    return rank


def _router_kernel(x_ref, wr_ref, b_ref, gt_ref):
    st = _dot_nt(wr_ref[...], x_ref[...])
    sig = _sigmoid(st)
    biased = sig + b_ref[...]
    per = N_EXPERTS // N_GROUPS
    blocks = [biased[g * per:(g + 1) * per, :] for g in range(N_GROUPS)]
    gscore = []
    for blk in blocks:
        top2 = jnp.where(_rank_rows(blk) < 2, blk, 0.0)
        gscore.append(jnp.sum(top2, axis=0, keepdims=True))
    masked = []
    for g in range(N_GROUPS):
        grank = jnp.zeros(gscore[g].shape, jnp.int32)
        for g2 in range(N_GROUPS):
            if g2 == g:
                continue
            beats = (gscore[g2] >= gscore[g]) if g2 < g else (gscore[g2] > gscore[g])
            grank = grank + jnp.where(beats, 1, 0)
        keep = jnp.broadcast_to(grank, blocks[g].shape) < TOPK_GROUPS
        masked.append(jnp.where(keep, blocks[g], -jnp.inf))
    cand = jnp.concatenate(masked, axis=0)
    sel = _rank_rows(cand) < TOP_K
    gates = jnp.where(sel, sig, 0.0)
    denom = jnp.sum(gates, axis=0, keepdims=True)
    gt_ref[...] = gates / denom * ROUTED_SCALE


def _router(xb, wr_t, b_col, *, tm):
    N = xb.shape[0]
    return pl.pallas_call(
        _router_kernel,
        grid=(N // tm,),
        in_specs=[
            pl.BlockSpec((tm, xb.shape[1]), lambda i: (i, 0)),
            pl.BlockSpec(wr_t.shape, lambda i: (0, 0)),
            pl.BlockSpec(b_col.shape, lambda i: (0, 0)),
        ],
        out_specs=pl.BlockSpec((N_EXPERTS, tm), lambda i: (0, i)),
        out_shape=jax.ShapeDtypeStruct((N_EXPERTS, N), F32),
        compiler_params=_cparams("parallel"),
        name="router",
    )(xb, wr_t, b_col)


def _moe_kernel(x_ref, g_ref, wg_ref, wu_ref, wd_ref, lnw_ref, lnb_ref, of_ref, ob_ref,
                xb_ref, acc_ref, *, alpha):
    e = pl.program_id(1)

    @pl.when(e == 0)
    def _():
        xb_ref[...] = x_ref[...].astype(BF16)
        acc_ref[...] = jnp.zeros(acc_ref.shape, F32)

    xb = xb_ref[...]
    hg = _dot(xb, wg_ref[0])
    hu = _dot(xb, wu_ref[0])
    hidden = (hg * _sigmoid(hg) * hu).astype(BF16)
    lane = lax.broadcasted_iota(jnp.int32, g_ref.shape, 1)
    gate = jnp.sum(jnp.where(lane == e, g_ref[...], 0.0), axis=1, keepdims=True)
    acc_ref[...] += _dot(hidden, wd_ref[0]) * gate

    @pl.when(e == pl.num_programs(1) - 1)
    def _():
        y = _layer_norm_rows(alpha * x_ref[...] + acc_ref[...], lnw_ref[...], lnb_ref[...])
        of_ref[...] = y
        ob_ref[...] = y.astype(BF16)


def _moe(x1, gates, wg, wu, wd, ln_w, ln_b, *, alpha, tm):
    N, D = x1.shape
    n_e = wg.shape[0]
    return pl.pallas_call(
        functools.partial(_moe_kernel, alpha=alpha),
        grid=(N // tm, n_e),
        in_specs=[
            pl.BlockSpec((tm, D), lambda i, e: (i, 0)),
            pl.BlockSpec((tm, LANES), lambda i, e: (i, 0)),
            pl.BlockSpec((1, D, EXPERT_DIM), lambda i, e: (e, 0, 0)),
            pl.BlockSpec((1, D, EXPERT_DIM), lambda i, e: (e, 0, 0)),
            pl.BlockSpec((1, EXPERT_DIM, D), lambda i, e: (e, 0, 0)),
            pl.BlockSpec((1, D), lambda i, e: (0, 0)),
            pl.BlockSpec((1, D), lambda i, e: (0, 0)),
        ],
        out_specs=[pl.BlockSpec((tm, D), lambda i, e: (i, 0)),
                   pl.BlockSpec((tm, D), lambda i, e: (i, 0))],
        out_shape=[jax.ShapeDtypeStruct((N, D), F32), jax.ShapeDtypeStruct((N, D), BF16)],
        scratch_shapes=[pltpu.VMEM((tm, D), BF16), pltpu.VMEM((tm, D), F32)],
        compiler_params=_cparams("parallel", "arbitrary"),
        name="moe",
    )(x1, gates, wg, wu, wd, ln_w, ln_b)


def _t5_bucket(rel):
    half = REL_BUCKETS // 2
    max_exact = half // 2
    ret = jnp.where(rel > 0, half, 0)
    n = jnp.abs(rel)
    nf = jnp.maximum(n, 1).astype(F32)
    large = max_exact + (jnp.log(nf / max_exact) / math.log(REL_MAX_DIST / max_exact)
                         * (half - max_exact)).astype(jnp.int32)
    large = jnp.minimum(large, half - 1)
    return ret + jnp.where(n < max_exact, n, large)


def _bias_tables(rel_bias):
    i = jnp.arange(Q_BLOCK)[:, None]
    u = jnp.arange(2 * Q_BLOCK)[None, :]
    toep = jnp.transpose(rel_bias[_t5_bucket(u - Q_BLOCK - i)], (2, 0, 1))
    far = rel_bias[_t5_bucket(jnp.asarray(-(Q_BLOCK + 1)))]
    return toep.astype(F32) * LOG2E, far.astype(F32) * LOG2E


def _rope_tables(S):
    half = R_QK_DIM // 2
    inv = ROPE_BASE ** (-jnp.arange(half, dtype=F32) / half)
    ang = jnp.arange(S, dtype=F32)[:, None] * inv[None, :]
    return jnp.cos(ang), jnp.sin(ang)


def _layer(x, xb, tables, p, *, B, S, alpha):
    N, D = x.shape
    cos, sin, toep, bias15, ret_consts = tables
    TM = 512 if N % 512 == 0 else 256
    a_w = A_HEADS * A_HEAD_DIM
    cuts = np.cumsum([0, a_w, A_LATENT, IDX_HEADS * IDX_DIM, IDX_DIM, IDX_HEADS,
                      R_HEADS * R_QK_DIM, R_HEADS * R_QK_DIM, R_HEADS * R_V_DIM, R_HEADS * R_V_DIM, D, D])
    w_in = p['w_in']
    piece = lambda k: w_in[:, int(cuts[k]):int(cuts[k + 1])].astype(BF16)
    row_spec = lambda w: pl.BlockSpec((1, w), lambda i, j: (0, 0))

    TMB = 1024 if (N % 1024 == 0 and S % 1024 == 0) else TM
    big = dict(tm=TMB, tn=1024, epilogue=_epi_plain)
    q_a = _mm(xb, piece(0), out_dtypes=[BF16], name="proj_qa", **big)
    c_kv = _mm(xb, piece(1), tm=TMB, tn=A_LATENT, out_dtypes=[BF16], epilogue=_epi_rms,
               extras=[p['ckv_norm'][None, :]], extra_specs=[row_spec(A_LATENT)], name="proj_ckv")
    q_idx = _mm(xb, piece(2), out_dtypes=[BF16], name="proj_qidx", **big)
    k_idx = _mm(xb, piece(3), tm=TMB, tn=IDX_DIM, out_dtypes=[BF16], epilogue=_epi_rms,
                extras=[p['kidx_norm'][None, :]], extra_specs=[row_spec(IDX_DIM)], name="proj_kidx")
    w_widx = jnp.pad(piece(4), ((0, 0), (0, LANES - IDX_HEADS)))
    w_idx = _mm(xb, w_widx, tm=TMB, tn=LANES, out_dtypes=[F32],
                epilogue=functools.partial(_epi_scale, scale=IDX_HEADS ** -0.5 * IDX_DIM ** -0.5),
                name="proj_widx")
    pos_spec = pl.BlockSpec((TMB, R_QK_DIM // 2), lambda i, j: (i % (S // TMB), 0))
    q_r = _mm(xb, piece(5), tm=TMB, tn=2 * R_QK_DIM, out_dtypes=[BF16],
              epilogue=functools.partial(_epi_rope, scale=1.0),
              extras=[cos, sin], extra_specs=[pos_spec, pos_spec], name="proj_qr")
    k_r = _mm(xb, piece(6), tm=TMB, tn=2 * R_QK_DIM, out_dtypes=[BF16],
              epilogue=functools.partial(_epi_rope, scale=R_QK_DIM ** -0.5),
              extras=[cos, sin], extra_specs=[pos_spec, pos_spec], name="proj_kr")
    v_r = _mm(xb, piece(7), out_dtypes=[BF16], name="proj_vr", **big)
    g_r = _mm(xb, piece(8), out_dtypes=[F32], name="proj_gr", **big)
    g_a = _mm(xb, piece(9), out_dtypes=[F32], name="proj_ga", **big)
    g_b = _mm(xb, piece(10), out_dtypes=[F32], name="proj_gb", **big)

    w_uk_t = jnp.transpose(p['w_uk'], (0, 2, 1)).astype(BF16)
    pad = lambda a: jnp.pad(a.reshape(B, S, -1), ((0, 0), (Q_BLOCK, 0), (0, 0)))
    y_a = _dsa(q_a, q_idx, w_idx, pad(k_idx), pad(c_kv), toep, bias15, w_uk_t, p['w_uv'].astype(BF16),
               B=B, S=S)

    y_r = _retention(q_r, k_r, v_r, g_r, p['gn_w'][None, :], p['gn_b'][None, :], ret_consts, B=B, S=S)

    tile_spec = pl.BlockSpec((TM, 1024), lambda i, j: (i, j))
    m_a = _mm(y_a, p['w_pa'].astype(BF16), tm=TM, tn=1024, out_dtypes=[F32], epilogue=_epi_gate,
              extras=[g_a], extra_specs=[tile_spec], name="proj_a")
    merged = _mm(y_r, p['w_pb'].astype(BF16), tm=TM, tn=1024, out_dtypes=[BF16], epilogue=_epi_gate_add,
                 extras=[g_b, m_a], extra_specs=[tile_spec, tile_spec], name="proj_b")
    full_row = pl.BlockSpec((TM, D), lambda i, j: (i, 0))
    x1, x1b = _mm(merged, p['w_o'].astype(BF16), tm=TM, tn=D, out_dtypes=[F32, BF16],
                  epilogue=functools.partial(_epi_ln, alpha=alpha),
                  extras=[x, p['ln1_w'][None, :], p['ln1_b'][None, :]],
                  extra_specs=[full_row, row_spec(D), row_spec(D)], name="out_ln1")

    gt = _router(x1b, p['w_router'].T.astype(BF16), p['b_router'][:, None], tm=TM)
    gates = jnp.concatenate([gt.T, jnp.ones((N, 1), F32), jnp.zeros((N, LANES - N_EXPERTS - 1), F32)], axis=1)
    wg = jnp.concatenate([p['we_gate'], p['ws_gate'][None]], axis=0).astype(BF16)
    wu = jnp.concatenate([p['we_up'], p['ws_up'][None]], axis=0).astype(BF16)
    wd = jnp.concatenate([p['we_down'], p['ws_down'][None]], axis=0).astype(BF16)
    return _moe(x1, gates, wg, wu, wd, p['ln2_w'][None, :], p['ln2_b'][None, :], alpha=alpha, tm=TM)


def kernel(x, rel_bias, w_in, ckv_norm, kidx_norm, w_uk, w_uv, gn_w, gn_b, w_pa, w_pb, w_o, ln1_w, ln1_b,
           w_router, b_router, we_gate, we_up, we_down, ws_gate, ws_up, ws_down, ln2_w, ln2_b):
    B, S, D = x.shape
    depth = w_in.shape[0]
    alpha = (2 * depth) ** 0.25
    cos, sin = _rope_tables(S)
    toep, bias15 = _bias_tables(rel_bias)
    tables = (cos, sin, toep, bias15, _retention_consts())
    params = dict(w_in=w_in, ckv_norm=ckv_norm, kidx_norm=kidx_norm, w_uk=w_uk, w_uv=w_uv, gn_w=gn_w,
                  gn_b=gn_b, w_pa=w_pa, w_pb=w_pb, w_o=w_o, ln1_w=ln1_w, ln1_b=ln1_b, w_router=w_router,
                  b_router=b_router, we_gate=we_gate, we_up=we_up, we_down=we_down, ws_gate=ws_gate,
                  ws_up=ws_up, ws_down=ws_down, ln2_w=ln2_w, ln2_b=ln2_b)
    xf = x.reshape(B * S, D)
    xb = xf.astype(BF16)
    for l in range(depth):
        xf, xb = _layer(xf, xb, tables, {k: v[l] for k, v in params.items()}, B=B, S=S, alpha=alpha)
    return xf.reshape(B, S, D)
```

```python
import functools
import math

import jax
import jax.numpy as jnp
import numpy as np
from jax import lax
from jax.experimental import pallas as pl
from jax.experimental.pallas import tpu as pltpu

CHUNK = 64
Q_BLOCK = 128
A_HEADS = 16
A_HEAD_DIM = 128
A_LATENT = 256
IDX_HEADS = 16
IDX_DIM = 128
TOPK_MAX = 256
REL_BUCKETS = 32
REL_MAX_DIST = 128
R_HEADS = 8
R_QK_DIM = 256
R_V_DIM = 512
ROPE_BASE = 10000.0
N_EXPERTS = 64
EXPERT_DIM = 256
TOP_K = 8
N_GROUPS = 8
TOPK_GROUPS = 4
ROUTED_SCALE = 2.5
LN_EPS = 1e-5
RMS_EPS = 1e-6

LANES = 128
FAR_TILE = 512
HEAD_GROUP = 4
RET_GROUP = 512
MOE_WINDOW = 512
MOE_ROWS = 128
VMEM_LIMIT = 56 * 1024 * 1024
INT_MIN = -2 ** 31
NEG_BIG = -1e30
LOG2E = 1.4426950408889634

F32 = jnp.float32
BF16 = jnp.bfloat16


def _cparams(*sem):
    return pltpu.CompilerParams(dimension_semantics=sem, vmem_limit_bytes=VMEM_LIMIT)


def _sigmoid(x):
    return 1.0 / (1.0 + jnp.exp(-x))


def _dot(a, b):
    return jnp.dot(a, b, preferred_element_type=F32)


def _dot_nt(a, b):
    return lax.dot_general(a, b, (((1,), (1,)), ((), ())), preferred_element_type=F32)


def _layer_norm_rows(v, w, b):
    mu = jnp.mean(v, axis=-1, keepdims=True)
    d = v - mu
    var = jnp.mean(d * d, axis=-1, keepdims=True)
    return d * lax.rsqrt(var + LN_EPS) * w + b


def _mm_kernel(*refs, epilogue, n_extra, n_out):
    a_ref, b_ref = refs[0], refs[1]
    extra = refs[2:2 + n_extra]
    outs = refs[2 + n_extra:2 + n_extra + n_out]
    acc = _dot(a_ref[...], b_ref[...])
    res = epilogue(acc, *extra)
    if not isinstance(res, tuple):
        res = (res,)
    for o_ref, r in zip(outs, res):
        o_ref[...] = r.astype(o_ref.dtype)


def _mm(a, b, *, tm, tn, out_dtypes, epilogue, extras=(), extra_specs=(), name):
    M = a.shape[0]
    K, n_cols = b.shape
    grid = (M // tm, n_cols // tn)
    a_spec = pl.BlockSpec((tm, K), lambda i, j: (i, 0))
    b_spec = pl.BlockSpec((K, tn), lambda i, j: (0, j))
    out_shape = [jax.ShapeDtypeStruct((M, n_cols), dt) for dt in out_dtypes]
    out_specs = [pl.BlockSpec((tm, tn), lambda i, j: (i, j)) for _ in out_dtypes]
    res = pl.pallas_call(
        functools.partial(_mm_kernel, epilogue=epilogue, n_extra=len(extras), n_out=len(out_dtypes)),
        grid=grid,
        in_specs=[a_spec, b_spec, *extra_specs],
        out_specs=out_specs,
        out_shape=out_shape,
        compiler_params=_cparams("parallel", "arbitrary"),
        name=name,
    )(a, b, *extras)
    return res[0] if len(res) == 1 else res


def _epi_plain(acc):
    return acc


def _epi_scale(acc, *, scale):
    return acc * scale


def _epi_rms(acc, w_ref):
    return acc * lax.rsqrt(jnp.mean(acc * acc, axis=-1, keepdims=True) + RMS_EPS) * w_ref[...]


def _epi_rope(acc, cos_ref, sin_ref, *, scale):
    half = R_QK_DIM // 2
    c, s = cos_ref[...] * scale, sin_ref[...] * scale
    out = []
    for h in range(acc.shape[-1] // R_QK_DIM):
        x1 = acc[:, h * R_QK_DIM:h * R_QK_DIM + half]
        x2 = acc[:, h * R_QK_DIM + half:(h + 1) * R_QK_DIM]
        out += [x1 * c - x2 * s, x1 * s + x2 * c]
    return jnp.concatenate(out, axis=-1)


def _epi_gate(acc, g_ref):
    return _sigmoid(g_ref[...]) * acc


def _epi_gate_add(acc, g_ref, add_ref):
    return _sigmoid(g_ref[...]) * acc + add_ref[...]


def _epi_ln(acc, res_ref, w_ref, b_ref, *, alpha):
    y = _layer_norm_rows(alpha * res_ref[...] + acc, w_ref[...], b_ref[...])
    return y, y


def _float_key(x):
    bits = pltpu.bitcast(x, jnp.int32)
    return bits ^ ((bits >> 31) & jnp.int32(0x7FFFFFFF))


def _dsa_kernel(bias15_ref, qa_ref, qidx_ref, widx_ref, kidx_ref, ckv_ref, toep_ref, wuk_ref, wuv_ref,
                out_ref, scf_ref, scn_ref, tmp0_ref, tmp1_ref, zb0_ref, zb1_ref, madd_ref, qa2_ref, qi2_ref,
                wb_ref, p0_ref, p1_ref, acc_ref, m_ref, l_ref, al0_ref, al1_ref, *, topk):
    i = pl.program_id(1)
    t0 = i * Q_BLOCK
    far_end = t0 - Q_BLOCK
    n_far = (jnp.maximum(i - 1, 0) * Q_BLOCK + FAR_TILE - 1) // FAR_TILE
    n_ch = FAR_TILE // LANES
    H = A_HEADS
    QB = Q_BLOCK

    for h in range(H):
        q_abs = _dot(qa_ref[:, h * A_HEAD_DIM:(h + 1) * A_HEAD_DIM], wuk_ref[h])
        qa2_ref[h * QB:(h + 1) * QB, :] = (q_abs * (A_HEAD_DIM ** -0.5 * LOG2E)).astype(BF16)
        qi2_ref[h * QB:(h + 1) * QB, :] = qidx_ref[:, h * IDX_DIM:(h + 1) * IDX_DIM]
        wb_ref[h] = jnp.broadcast_to(widx_ref[:, h:h + 1], (QB, LANES))

    lane = lax.broadcasted_iota(jnp.int32, (QB, LANES), 1)
    row = lax.broadcasted_iota(jnp.int32, (QB, LANES), 0)

    tmp = (tmp0_ref, tmp1_ref)
    zbuf = (zb0_ref, zb1_ref)
    pbuf = (p0_ref, p1_ref)
    albuf = (al0_ref, al1_ref)
    n_pairs = (n_far + 1) // 2
    last_start = kidx_ref.shape[0] - FAR_TILE

    def far_rows(kt):
        return pl.ds(pl.multiple_of(jnp.clip(Q_BLOCK + kt * FAR_TILE, Q_BLOCK, last_start), LANES), FAR_TILE)

    def head_sum(z_ref, c):
        acc = jnp.zeros((QB, LANES), F32)
        for h in range(H):
            z = z_ref[h * QB:(h + 1) * QB, c * LANES:(c + 1) * LANES]
            acc = acc + wb_ref[h] * jnp.maximum(z, 0.0)
        return acc

    def far_scores(z_ref, kt):
        for c in range(n_ch):
            key = _float_key(head_sum(z_ref, c))
            s_pos = kt * FAR_TILE + c * LANES + lane
            scf_ref[kt, :, c * LANES:(c + 1) * LANES] = jnp.where(s_pos < far_end, key, INT_MIN)

    def idx_matmul(z_ref, kt):
        z_ref[...] = _dot_nt(qi2_ref[...], kidx_ref[far_rows(kt), :])

    idx_matmul(tmp[0], 0)

    def score_pair(j, carry):
        kt = 2 * j
        idx_matmul(tmp[1], kt + 1)
        far_scores(tmp[0], kt)
        idx_matmul(tmp[0], kt + 2)
        far_scores(tmp[1], kt + 1)
        return carry

    lax.fori_loop(0, n_pairs, score_pair, 0)

    near_start = pl.multiple_of(t0, LANES)
    tmp[0][:, :2 * LANES] = _dot_nt(qi2_ref[...], kidx_ref[pl.ds(near_start, 2 * LANES), :])
    u_lo = jnp.where(i == 0, QB, 0)
    u_hi = jnp.where(row < CHUNK, QB + CHUNK, 2 * QB)
    for c in range(2):
        key = _float_key(head_sum(tmp[0], c))
        u = c * LANES + lane
        vis = jnp.logical_and(u >= u_lo, u < u_hi)
        scn_ref[:, c * LANES:(c + 1) * LANES] = jnp.where(vis, key, INT_MIN)

    def bit_body(bi, p):
        bit = jnp.left_shift(jnp.int32(1), 31 - bi)
        p_try = p | bit
        t_try = p_try ^ jnp.int32(INT_MIN)

        def cnt_body(kt, cnt):
            for c in range(n_ch):
                cnt = cnt + jnp.where(scf_ref[kt, :, c * LANES:(c + 1) * LANES] >= t_try, 1, 0)
            return cnt

        cnt = lax.fori_loop(0, n_far, cnt_body, jnp.zeros((QB, LANES), jnp.int32))
        for c in range(2):
            cnt = cnt + jnp.where(scn_ref[:, c * LANES:(c + 1) * LANES] >= t_try, 1, 0)
        total = jnp.sum(cnt, axis=1, keepdims=True)
        return jnp.where(total >= topk, p_try, p)

    p_fin = lax.fori_loop(0, 32, bit_body, jnp.zeros((QB, LANES), jnp.int32))
    thr = jnp.maximum(p_fin ^ jnp.int32(INT_MIN), jnp.int32(INT_MIN + 1))

    m_ref[...] = jnp.full(m_ref.shape, NEG_BIG, F32)
    l_ref[...] = jnp.zeros(l_ref.shape, F32)
    acc_ref[...] = jnp.zeros(acc_ref.shape, F32)

    def to_mask(kt, carry):
        for c in range(n_ch):
            cols = slice(c * LANES, (c + 1) * LANES)
            mask = jnp.where(scf_ref[kt, :, cols] >= thr, 0.0, -jnp.inf)
            scf_ref[kt, :, cols] = pltpu.bitcast(mask, jnp.int32)
        return carry

    lax.fori_loop(0, n_far, to_mask, 0)
    for c in range(2):
        cols = slice(c * LANES, (c + 1) * LANES)
        madd_ref[:, cols] = jnp.where(scn_ref[:, cols] >= thr, 0.0, -jnp.inf)

    GR = HEAD_GROUP * QB
    n_grp = H // HEAD_GROUP

    def aligned(start, size):
        return pl.ds(start if isinstance(start, int) else pl.multiple_of(start, size), size)

    def group_rows(g):
        return aligned((g % n_grp) * GR, GR)

    def softmax_group(width, near, z_ref, p_ref, al_ref, g):
        kt, hg = g // n_grp, g % n_grp
        chunks = [slice(c * LANES, (c + 1) * LANES) for c in range(width // LANES)]
        for hh in range(HEAD_GROUP):
            h = hg * HEAD_GROUP + hh
            rows = slice(hh * QB, (hh + 1) * QB)
            stat = aligned(h * QB, QB)

            def masked_logits(ch):
                if near:
                    v = z_ref[rows, ch] + madd_ref[:, ch] + toep_ref[h, :, ch]
                else:
                    v = z_ref[rows, ch] + pltpu.bitcast(scf_ref[kt, :, ch], F32)
                z_ref[rows, ch] = v
                return v

            mx = masked_logits(chunks[0])
            for ch in chunks[1:]:
                mx = jnp.maximum(mx, masked_logits(ch))
            mx = jnp.max(mx, axis=1, keepdims=True)
            bias = 0.0 if near else bias15_ref[h]
            m_old = m_ref[stat, :]
            m_new = jnp.maximum(m_old, mx + bias)
            shift = m_new - bias
            psum = jnp.zeros((QB, LANES), F32)
            for ch in chunks:
                p = jnp.exp2(z_ref[rows, ch] - shift)
                psum = psum + p
                p_ref[rows, ch] = p.astype(BF16)
            alpha = jnp.exp2(m_old - m_new)
            l_ref[stat, :] = alpha * l_ref[stat, :] + jnp.sum(psum, axis=1, keepdims=True)
            m_ref[stat, :] = m_new
            al_ref[rows, :] = alpha

    def pv_group(p_ref, al_ref, c_tile, width, g):
        al = al_ref[...]
        rows = group_rows(g)
        acc_ref[rows, :] = (acc_ref[rows, :] * jnp.concatenate([al] * (A_LATENT // LANES), axis=1)
                            + _dot(p_ref[:, :width], c_tile))

    def far_logits(z_ref, g):
        z_ref[...] = _dot_nt(qa2_ref[group_rows(g), :], ckv_ref[far_rows(g // n_grp), :])

    def far_pv(slot, g):
        pv_group(pbuf[slot], albuf[slot], ckv_ref[far_rows(g // n_grp), :], FAR_TILE, g)

    p1_ref[...] = jnp.zeros(p1_ref.shape, BF16)
    al1_ref[...] = jnp.ones(al1_ref.shape, F32)
    far_logits(zbuf[0], 0)

    def attn_pair(j, carry):
        g = 2 * j
        far_logits(zbuf[1], g + 1)
        softmax_group(FAR_TILE, False, zbuf[0], pbuf[0], albuf[0], g)
        far_pv(1, g - 1)
        far_logits(zbuf[0], g + 2)
        softmax_group(FAR_TILE, False, zbuf[1], pbuf[1], albuf[1], g + 1)
        far_pv(0, g)
        return carry

    n_units = n_far * n_grp
    lax.fori_loop(0, n_units // 2, attn_pair, 0)
    far_pv(1, n_units - 1)

    c_near = ckv_ref[pl.ds(near_start, 2 * LANES), :]

    for g in range(n_grp):
        s = g % 2
        zbuf[s][:, :2 * LANES] = _dot_nt(qa2_ref[g * GR:(g + 1) * GR, :], c_near)
        softmax_group(2 * LANES, True, zbuf[s], pbuf[s], albuf[s], g)
        pv_group(pbuf[s], albuf[s], c_near, 2 * LANES, g)

    for h in range(H):
        rows = slice(h * QB, (h + 1) * QB)
        inv = 1.0 / l_ref[rows, :]
        o_lat = jnp.concatenate([acc_ref[rows, c * LANES:(c + 1) * LANES] * inv
                                 for c in range(A_LATENT // LANES)], axis=1).astype(BF16)
        out_ref[:, h * A_HEAD_DIM:(h + 1) * A_HEAD_DIM] = _dot(o_lat, wuv_ref[h]).astype(out_ref.dtype)


def _dsa(q_a, qidx, widx, kidx_pad, ckv_pad, toep, bias15, w_uk_t, w_uv, *, B, S):
    nqb = S // Q_BLOCK
    topk = min(TOPK_MAX, S // 4)
    n_far_max = max(1, -(-(S - 2 * Q_BLOCK) // FAR_TILE))
    HQ = A_HEADS * Q_BLOCK
    GR = HEAD_GROUP * Q_BLOCK
    resident = dict(pipeline_mode=pl.Buffered(1))
    return pl.pallas_call(
        functools.partial(_dsa_kernel, topk=topk),
        grid=(B, nqb),
        in_specs=[
            pl.BlockSpec(memory_space=pltpu.SMEM),
            pl.BlockSpec((Q_BLOCK, A_HEADS * A_HEAD_DIM), lambda b, i: (b * nqb + i, 0)),
            pl.BlockSpec((Q_BLOCK, IDX_HEADS * IDX_DIM), lambda b, i: (b * nqb + i, 0)),
            pl.BlockSpec((Q_BLOCK, LANES), lambda b, i: (b * nqb + i, 0)),
            pl.BlockSpec((None, S + Q_BLOCK, IDX_DIM), lambda b, i: (b, 0, 0), **resident),
            pl.BlockSpec((None, S + Q_BLOCK, A_LATENT), lambda b, i: (b, 0, 0), **resident),
            pl.BlockSpec((A_HEADS, Q_BLOCK, 2 * LANES), lambda b, i: (0, 0, 0), **resident),
            pl.BlockSpec((A_HEADS, A_HEAD_DIM, A_LATENT), lambda b, i: (0, 0, 0), **resident),
            pl.BlockSpec((A_HEADS, A_LATENT, A_HEAD_DIM), lambda b, i: (0, 0, 0), **resident),
        ],
        out_specs=pl.BlockSpec((Q_BLOCK, A_HEADS * A_HEAD_DIM), lambda b, i: (b * nqb + i, 0)),
        out_shape=jax.ShapeDtypeStruct((B * S, A_HEADS * A_HEAD_DIM), BF16),
        scratch_shapes=[
            pltpu.VMEM((n_far_max + 1, Q_BLOCK, FAR_TILE), jnp.int32),
            pltpu.VMEM((Q_BLOCK, 2 * LANES), jnp.int32),
            pltpu.VMEM((HQ, FAR_TILE), F32),
            pltpu.VMEM((HQ, FAR_TILE), F32),
            pltpu.VMEM((GR, FAR_TILE), F32),
            pltpu.VMEM((GR, FAR_TILE), F32),
            pltpu.VMEM((Q_BLOCK, 2 * LANES), F32),
            pltpu.VMEM((HQ, A_LATENT), BF16),
            pltpu.VMEM((HQ, IDX_DIM), BF16),
            pltpu.VMEM((IDX_HEADS, Q_BLOCK, LANES), F32),
            pltpu.VMEM((GR, FAR_TILE), BF16),
            pltpu.VMEM((GR, FAR_TILE), BF16),
            pltpu.VMEM((HQ, A_LATENT), F32),
            pltpu.VMEM((HQ, LANES), F32),
            pltpu.VMEM((HQ, LANES), F32),
            pltpu.VMEM((GR, LANES), F32),
            pltpu.VMEM((GR, LANES), F32),
        ],
        compiler_params=_cparams("parallel", "arbitrary"),
        name="dsa",
    )(bias15, q_a, qidx, widx, kidx_pad, ckv_pad, toep, w_uk_t, w_uv)


def _ret_kernel(gtot_ref, q_ref, k_ref, v_ref, g_ref, d_ref, xi_ref, zeta_ref, gnw_ref, gnb_ref,
                o_ref, state_ref):
    @pl.when(pl.program_id(2) == 0)
    def _():
        state_ref[...] = jnp.zeros(state_ref.shape, F32)

    h = pl.program_id(1)
    q, k, v = q_ref[...], k_ref[...], v_ref[...]
    s = _dot_nt(q, k) * d_ref[0]
    o = _dot(s.astype(BF16), v) + _dot(q, state_ref[...].astype(BF16)) * xi_ref[0]
    kz = (k.astype(F32) * zeta_ref[0]).astype(BF16)
    upd = lax.dot_general(kz, v, (((0,), (0,)), ((), ())), preferred_element_type=F32)
    state_ref[...] = state_ref[...] * gtot_ref[h] + upd

    mu = jnp.mean(o, axis=-1, keepdims=True)
    d = o - mu
    var = jnp.mean(d * d, axis=-1, keepdims=True)
    y = d * lax.rsqrt(var + LN_EPS) * gnw_ref[...] + gnb_ref[...]
    g = g_ref[...]
    o_ref[...] = (g * _sigmoid(g) * y).astype(o_ref.dtype)


def _retention(q_rot, k_rot, v, g_r, gn_w, gn_b, consts, *, B, S):
    d_mat, xi, zeta, gtot = consts
    ng = S // RET_GROUP
    G = RET_GROUP
    return pl.pallas_call(
        _ret_kernel,
        grid=(B, R_HEADS, ng),
        in_specs=[
            pl.BlockSpec(memory_space=pltpu.SMEM),
            pl.BlockSpec((G, R_QK_DIM), lambda b, h, g: (b * ng + g, h)),
            pl.BlockSpec((G, R_QK_DIM), lambda b, h, g: (b * ng + g, h)),
            pl.BlockSpec((G, R_V_DIM), lambda b, h, g: (b * ng + g, h)),
            pl.BlockSpec((G, R_V_DIM), lambda b, h, g: (b * ng + g, h)),
            pl.BlockSpec((1, G, G), lambda b, h, g: (h, 0, 0)),
            pl.BlockSpec((1, G, 1), lambda b, h, g: (h, 0, 0)),
            pl.BlockSpec((1, G, 1), lambda b, h, g: (h, 0, 0)),
            pl.BlockSpec((1, R_V_DIM), lambda b, h, g: (0, h)),
            pl.BlockSpec((1, R_V_DIM), lambda b, h, g: (0, h)),
        ],
        out_specs=pl.BlockSpec((G, R_V_DIM), lambda b, h, g: (b * ng + g, h)),
        out_shape=jax.ShapeDtypeStruct((B * S, R_HEADS * R_V_DIM), BF16),
        scratch_shapes=[pltpu.VMEM((R_QK_DIM, R_V_DIM), F32)],
        compiler_params=_cparams("parallel", "parallel", "arbitrary"),
        name="retention",
    )(gtot, q_rot, k_rot, v, g_r, d_mat, xi, zeta, gn_w, gn_b)


def _retention_consts():
    G = RET_GROUP
    log_g = jnp.log1p(-jnp.exp2(-5.0 - jnp.arange(R_HEADS, dtype=F32)))
    pos = jnp.arange(G, dtype=F32)
    diff = pos[:, None] - pos[None, :]
    ci = jnp.arange(G)[:, None] // CHUNK
    cj = jnp.arange(G)[None, :] // CHUNK
    same = jnp.exp(log_g[:, None, None] * jnp.abs(diff))
    earlier = jnp.exp(log_g[:, None, None] * diff)
    d_mat = jnp.where(ci == cj, same, jnp.where(cj < ci, earlier, 0.0))
    xi = jnp.exp(log_g[:, None] * (pos[None, :] + 1.0))[..., None]
    zeta = jnp.exp(log_g[:, None] * (G - 1.0 - pos[None, :]))[..., None]
    gtot = jnp.exp(log_g * G)
    return d_mat, xi, zeta, gtot


def _rank_rows(v):
    n = v.shape[0]
    ridx = lax.broadcasted_iota(jnp.int32, v.shape, 0)
    rank = jnp.zeros(v.shape, jnp.int32)
    for j in range(n):
        rj = v[j:j + 1, :]
        rank = rank + jnp.where(ridx > j, jnp.where(rj >= v, 1, 0), jnp.where(rj > v, 1, 0))
    return rank


def _router_kernel(x_ref, wr_ref, b_ref, gt_ref):
    st = _dot_nt(wr_ref[...], x_ref[...])
    sig = _sigmoid(st)
    biased = sig + b_ref[...]
    per = N_EXPERTS // N_GROUPS
    blocks = [biased[g * per:(g + 1) * per, :] for g in range(N_GROUPS)]
    gscore = []
    for blk in blocks:
        top2 = jnp.where(_rank_rows(blk) < 2, blk, 0.0)
        gscore.append(jnp.sum(top2, axis=0, keepdims=True))
    masked = []
    for g in range(N_GROUPS):
        grank = jnp.zeros(gscore[g].shape, jnp.int32)
        for g2 in range(N_GROUPS):
            if g2 == g:
                continue
            beats = (gscore[g2] >= gscore[g]) if g2 < g else (gscore[g2] > gscore[g])
            grank = grank + jnp.where(beats, 1, 0)
        keep = jnp.broadcast_to(grank, blocks[g].shape) < TOPK_GROUPS
        masked.append(jnp.where(keep, blocks[g], -jnp.inf))
    cand = jnp.concatenate(masked, axis=0)
    sel = _rank_rows(cand) < TOP_K
    gates = jnp.where(sel, sig, 0.0)
    denom = jnp.sum(gates, axis=0, keepdims=True)
    gt_ref[...] = gates / denom * ROUTED_SCALE


def _router(xb, wr_t, b_col, *, tm):
    N = xb.shape[0]
    return pl.pallas_call(
        _router_kernel,
        grid=(N // tm,),
        in_specs=[
            pl.BlockSpec((tm, xb.shape[1]), lambda i: (i, 0)),
            pl.BlockSpec(wr_t.shape, lambda i: (0, 0)),
            pl.BlockSpec(b_col.shape, lambda i: (0, 0)),
        ],
        out_specs=pl.BlockSpec((N_EXPERTS, tm), lambda i: (0, i)),
        out_shape=jax.ShapeDtypeStruct((N_EXPERTS, N), F32),
        compiler_params=_cparams("parallel"),
        name="router",
    )(xb, wr_t, b_col)


def _swiglu(xb, wg, wu, wd):
    hg = _dot(xb, wg)
    hidden = (hg * _sigmoid(hg) * _dot(xb, wu)).astype(BF16)
    return _dot(hidden, wd)


def _moe_kernel(nsub_ref, x_ref, rk_ref, g_ref, wg_ref, wu_ref, wd_ref, sg_ref, su_ref, sd_ref,
                lnw_ref, lnb_ref, of_ref, ob_ref, xb_ref, acc_ref, *, alpha):
    i = pl.program_id(0)
    s = pl.program_id(1)
    n_pairs = pl.num_programs(1) - 1
    W = x_ref.shape[0]

    @pl.when(s == 0)
    def _():
        xb_ref[...] = x_ref[...].astype(BF16)
        acc_ref[...] = jnp.zeros(acc_ref.shape, F32)

    @pl.when(s < n_pairs)
    def _():
        row_id = lax.broadcasted_iota(jnp.int32, (MOE_ROWS, W), 0)

        def sub_tile(j, carry):
            picks, outs = [], []
            for q in range(2):
                hit = (row_id + j * MOE_ROWS) == rk_ref[q:q + 1, :]
                onehot = jnp.where(hit, 1.0, 0.0)
                pick = onehot.astype(BF16)
                xs = _dot(pick, xb_ref[...]).astype(BF16)
                y = _swiglu(xs, wg_ref[q], wu_ref[q], wd_ref[q])
                gate = jnp.sum(onehot * g_ref[q:q + 1, :], axis=1, keepdims=True)
                picks.append(pick)
                outs.append((y * gate).astype(BF16))
            acc_ref[...] += lax.dot_general(jnp.concatenate(picks, axis=0), jnp.concatenate(outs, axis=0),
                                            (((0,), (0,)), ((), ())), preferred_element_type=F32)
            return carry

        lax.fori_loop(0, nsub_ref[i * n_pairs + s], sub_tile, 0)

    @pl.when(s == n_pairs)
    def _():
        shared = _swiglu(xb_ref[...], sg_ref[...], su_ref[...], sd_ref[...])
        y = _layer_norm_rows(alpha * x_ref[...] + acc_ref[...] + shared, lnw_ref[...], lnb_ref[...])
        of_ref[...] = y
        ob_ref[...] = y.astype(BF16)


def _moe(x1, gt, p, *, alpha):
    N, D = x1.shape
    W = MOE_WINDOW
    nw, n_pairs = N // W, N_EXPERTS // 2
    sel = (gt != 0.0).reshape(N_EXPERTS, nw, W)
    rank = jnp.where(sel, jnp.cumsum(sel.astype(jnp.int32), axis=2) - 1, -1)
    per_pair = lambda a: jnp.transpose(a, (1, 0, 2)).reshape(nw, n_pairs, 2, W)
    n_sub = (jnp.sum(sel, axis=2, dtype=jnp.int32) + MOE_ROWS - 1) // MOE_ROWS
    n_sub = jnp.max(n_sub.T.reshape(nw, n_pairs, 2), axis=2).reshape(-1)
    pair = lambda i, s, n: (jnp.minimum(s, n_pairs - 1), 0, 0)
    resident = dict(pipeline_mode=pl.Buffered(1))
    const2 = lambda i, s, n: (0, 0)
    grid_spec = pltpu.PrefetchScalarGridSpec(
        num_scalar_prefetch=1,
        grid=(nw, n_pairs + 1),
        in_specs=[
            pl.BlockSpec((W, D), lambda i, s, n: (i, 0)),
            pl.BlockSpec((None, None, 2, W), lambda i, s, n: (i, jnp.minimum(s, n_pairs - 1), 0, 0)),
            pl.BlockSpec((None, None, 2, W), lambda i, s, n: (i, jnp.minimum(s, n_pairs - 1), 0, 0)),
            pl.BlockSpec((2, D, EXPERT_DIM), pair),
            pl.BlockSpec((2, D, EXPERT_DIM), pair),
            pl.BlockSpec((2, EXPERT_DIM, D), pair),
            pl.BlockSpec((D, EXPERT_DIM), const2, **resident),
            pl.BlockSpec((D, EXPERT_DIM), const2, **resident),
            pl.BlockSpec((EXPERT_DIM, D), const2, **resident),
            pl.BlockSpec((1, D), const2),
            pl.BlockSpec((1, D), const2),
        ],
        out_specs=[pl.BlockSpec((W, D), lambda i, s, n: (i, 0)),
                   pl.BlockSpec((W, D), lambda i, s, n: (i, 0))],
        scratch_shapes=[pltpu.VMEM((W, D), BF16), pltpu.VMEM((W, D), F32)],
    )
    return pl.pallas_call(
        functools.partial(_moe_kernel, alpha=alpha),
        grid_spec=grid_spec,
        out_shape=[jax.ShapeDtypeStruct((N, D), F32), jax.ShapeDtypeStruct((N, D), BF16)],
        compiler_params=_cparams("parallel", "arbitrary"),
        name="moe",
    )(n_sub, x1, per_pair(rank), per_pair(gt.reshape(N_EXPERTS, nw, W)),
      p['we_gate'].astype(BF16), p['we_up'].astype(BF16), p['we_down'].astype(BF16),
      p['ws_gate'].astype(BF16), p['ws_up'].astype(BF16), p['ws_down'].astype(BF16),
      p['ln2_w'][None, :], p['ln2_b'][None, :])


def _t5_bucket(rel):
    half = REL_BUCKETS // 2
    max_exact = half // 2
    ret = jnp.where(rel > 0, half, 0)
    n = jnp.abs(rel)
    nf = jnp.maximum(n, 1).astype(F32)
    large = max_exact + (jnp.log(nf / max_exact) / math.log(REL_MAX_DIST / max_exact)
                         * (half - max_exact)).astype(jnp.int32)
    large = jnp.minimum(large, half - 1)
    return ret + jnp.where(n < max_exact, n, large)


def _bias_tables(rel_bias):
    i = jnp.arange(Q_BLOCK)[:, None]
    u = jnp.arange(2 * Q_BLOCK)[None, :]
    toep = jnp.transpose(rel_bias[_t5_bucket(u - Q_BLOCK - i)], (2, 0, 1))
    far = rel_bias[_t5_bucket(jnp.asarray(-(Q_BLOCK + 1)))]
    return toep.astype(F32) * LOG2E, far.astype(F32) * LOG2E


def _rope_tables(S):
    half = R_QK_DIM // 2
    inv = ROPE_BASE ** (-jnp.arange(half, dtype=F32) / half)
    ang = jnp.arange(S, dtype=F32)[:, None] * inv[None, :]
    return jnp.cos(ang), jnp.sin(ang)


def _layer(x, xb, tables, p, *, B, S, alpha):
    N, D = x.shape
    cos, sin, toep, bias15, ret_consts = tables
    TM = 512 if N % 512 == 0 else 256
    a_w = A_HEADS * A_HEAD_DIM
    cuts = np.cumsum([0, a_w, A_LATENT, IDX_HEADS * IDX_DIM, IDX_DIM, IDX_HEADS,
                      R_HEADS * R_QK_DIM, R_HEADS * R_QK_DIM, R_HEADS * R_V_DIM, R_HEADS * R_V_DIM, D, D])
    w_in = p['w_in']
    piece = lambda k: w_in[:, int(cuts[k]):int(cuts[k + 1])].astype(BF16)
    row_spec = lambda w: pl.BlockSpec((1, w), lambda i, j: (0, 0))

    TMB = 1024 if (N % 1024 == 0 and S % 1024 == 0) else TM
    big = dict(tm=TMB, tn=1024, epilogue=_epi_plain)
    q_a = _mm(xb, piece(0), out_dtypes=[BF16], name="proj_qa", **big)
    c_kv = _mm(xb, piece(1), tm=TMB, tn=A_LATENT, out_dtypes=[BF16], epilogue=_epi_rms,
               extras=[p['ckv_norm'][None, :]], extra_specs=[row_spec(A_LATENT)], name="proj_ckv")
    q_idx = _mm(xb, piece(2), out_dtypes=[BF16], name="proj_qidx", **big)
    k_idx = _mm(xb, piece(3), tm=TMB, tn=IDX_DIM, out_dtypes=[BF16], epilogue=_epi_rms,
                extras=[p['kidx_norm'][None, :]], extra_specs=[row_spec(IDX_DIM)], name="proj_kidx")
    w_widx = jnp.pad(piece(4), ((0, 0), (0, LANES - IDX_HEADS)))
    w_idx = _mm(xb, w_widx, tm=TMB, tn=LANES, out_dtypes=[F32],
                epilogue=functools.partial(_epi_scale, scale=IDX_HEADS ** -0.5 * IDX_DIM ** -0.5),
                name="proj_widx")
    pos_spec = pl.BlockSpec((TMB, R_QK_DIM // 2), lambda i, j: (i % (S // TMB), 0))
    q_r = _mm(xb, piece(5), tm=TMB, tn=2 * R_QK_DIM, out_dtypes=[BF16],
              epilogue=functools.partial(_epi_rope, scale=1.0),
              extras=[cos, sin], extra_specs=[pos_spec, pos_spec], name="proj_qr")
    k_r = _mm(xb, piece(6), tm=TMB, tn=2 * R_QK_DIM, out_dtypes=[BF16],
              epilogue=functools.partial(_epi_rope, scale=R_QK_DIM ** -0.5),
              extras=[cos, sin], extra_specs=[pos_spec, pos_spec], name="proj_kr")
    v_r = _mm(xb, piece(7), out_dtypes=[BF16], name="proj_vr", **big)
    g_r = _mm(xb, piece(8), out_dtypes=[F32], name="proj_gr", **big)
    g_a = _mm(xb, piece(9), out_dtypes=[F32], name="proj_ga", **big)
    g_b = _mm(xb, piece(10), out_dtypes=[F32], name="proj_gb", **big)

    w_uk_t = jnp.transpose(p['w_uk'], (0, 2, 1)).astype(BF16)
    pad = lambda a: jnp.pad(a.reshape(B, S, -1), ((0, 0), (Q_BLOCK, 0), (0, 0)))
    y_a = _dsa(q_a, q_idx, w_idx, pad(k_idx), pad(c_kv), toep, bias15, w_uk_t, p['w_uv'].astype(BF16),
               B=B, S=S)

    y_r = _retention(q_r, k_r, v_r, g_r, p['gn_w'][None, :], p['gn_b'][None, :], ret_consts, B=B, S=S)

    tile_spec = pl.BlockSpec((TM, 1024), lambda i, j: (i, j))
    m_a = _mm(y_a, p['w_pa'].astype(BF16), tm=TM, tn=1024, out_dtypes=[F32], epilogue=_epi_gate,
              extras=[g_a], extra_specs=[tile_spec], name="proj_a")
    merged = _mm(y_r, p['w_pb'].astype(BF16), tm=TM, tn=1024, out_dtypes=[BF16], epilogue=_epi_gate_add,
                 extras=[g_b, m_a], extra_specs=[tile_spec, tile_spec], name="proj_b")
    full_row = pl.BlockSpec((TM, D), lambda i, j: (i, 0))
    x1, x1b = _mm(merged, p['w_o'].astype(BF16), tm=TM, tn=D, out_dtypes=[F32, BF16],
                  epilogue=functools.partial(_epi_ln, alpha=alpha),
                  extras=[x, p['ln1_w'][None, :], p['ln1_b'][None, :]],
                  extra_specs=[full_row, row_spec(D), row_spec(D)], name="out_ln1")

    gt = _router(x1b, p['w_router'].T.astype(BF16), p['b_router'][:, None], tm=TM)
    return _moe(x1, gt, p, alpha=alpha)


def kernel(x, rel_bias, w_in, ckv_norm, kidx_norm, w_uk, w_uv, gn_w, gn_b, w_pa, w_pb, w_o, ln1_w, ln1_b,
           w_router, b_router, we_gate, we_up, we_down, ws_gate, ws_up, ws_down, ln2_w, ln2_b):
    B, S, D = x.shape
    depth = w_in.shape[0]
    alpha = (2 * depth) ** 0.25
    cos, sin = _rope_tables(S)
    toep, bias15 = _bias_tables(rel_bias)
    tables = (cos, sin, toep, bias15, _retention_consts())
    params = dict(w_in=w_in, ckv_norm=ckv_norm, kidx_norm=kidx_norm, w_uk=w_uk, w_uv=w_uv, gn_w=gn_w,
                  gn_b=gn_b, w_pa=w_pa, w_pb=w_pb, w_o=w_o, ln1_w=ln1_w, ln1_b=ln1_b, w_router=w_router,
                  b_router=b_router, we_gate=we_gate, we_up=we_up, we_down=we_down, ws_gate=ws_gate,
                  ws_up=ws_up, ws_down=ws_down, ln2_w=ln2_w, ln2_b=ln2_b)
    xf = x.reshape(B * S, D)
    xb = xf.astype(BF16)
    for l in range(depth):
        xf, xb = _layer(xf, xb, tables, {k: v[l] for k, v in params.items()}, B=B, S=S, alpha=alpha)
    return xf.reshape(B, S, D)
```

```python
import functools
import math

import jax
import jax.numpy as jnp
import numpy as np
from jax import lax
from jax.experimental import pallas as pl
from jax.experimental.pallas import tpu as pltpu

CHUNK = 64
Q_BLOCK = 128
A_HEADS = 16
A_HEAD_DIM = 128
A_LATENT = 256
IDX_HEADS = 16
IDX_DIM = 128
TOPK_MAX = 256
REL_BUCKETS = 32
REL_MAX_DIST = 128
R_HEADS = 8
R_QK_DIM = 256
R_V_DIM = 512
ROPE_BASE = 10000.0
N_EXPERTS = 64
EXPERT_DIM = 256
TOP_K = 8
N_GROUPS = 8
TOPK_GROUPS = 4
ROUTED_SCALE = 2.5
LN_EPS = 1e-5
RMS_EPS = 1e-6

LANES = 128
FAR_TILE = 512
HEAD_GROUP = 4
RET_GROUP = 512
MOE_WINDOW = 512
MOE_SUPER = 2
MOE_ROWS = 128
VMEM_LIMIT = 56 * 1024 * 1024
INT_MIN = -2 ** 31
NEG_BIG = -1e30
LOG2E = 1.4426950408889634

F32 = jnp.float32
BF16 = jnp.bfloat16


def _cparams(*sem):
    return pltpu.CompilerParams(dimension_semantics=sem, vmem_limit_bytes=VMEM_LIMIT)


def _sigmoid(x):
    return 1.0 / (1.0 + jnp.exp(-x))


def _dot(a, b):
    return jnp.dot(a, b, preferred_element_type=F32)


def _dot_nt(a, b):
    return lax.dot_general(a, b, (((1,), (1,)), ((), ())), preferred_element_type=F32)


def _layer_norm_rows(v, w, b):
    mu = jnp.mean(v, axis=-1, keepdims=True)
    d = v - mu
    var = jnp.mean(d * d, axis=-1, keepdims=True)
    return d * lax.rsqrt(var + LN_EPS) * w + b


def _mm_kernel(*refs, epilogue, n_extra, n_out):
    a_ref, b_ref = refs[0], refs[1]
    extra = refs[2:2 + n_extra]
    outs = refs[2 + n_extra:2 + n_extra + n_out]
    acc = _dot(a_ref[...], b_ref[...])
    res = epilogue(acc, *extra)
    if not isinstance(res, tuple):
        res = (res,)
    for o_ref, r in zip(outs, res):
        o_ref[...] = r.astype(o_ref.dtype)


def _mm(a, b, *, tm, tn, out_dtypes, epilogue, extras=(), extra_specs=(), name):
    M = a.shape[0]
    K, n_cols = b.shape
    grid = (M // tm, n_cols // tn)
    a_spec = pl.BlockSpec((tm, K), lambda i, j: (i, 0))
    b_spec = pl.BlockSpec((K, tn), lambda i, j: (0, j))
    out_shape = [jax.ShapeDtypeStruct((M, n_cols), dt) for dt in out_dtypes]
    out_specs = [pl.BlockSpec((tm, tn), lambda i, j: (i, j)) for _ in out_dtypes]
    res = pl.pallas_call(
        functools.partial(_mm_kernel, epilogue=epilogue, n_extra=len(extras), n_out=len(out_dtypes)),
        grid=grid,
        in_specs=[a_spec, b_spec, *extra_specs],
        out_specs=out_specs,
        out_shape=out_shape,
        compiler_params=_cparams("parallel", "arbitrary"),
        name=name,
    )(a, b, *extras)
    return res[0] if len(res) == 1 else res


def _epi_plain(acc):
    return acc


def _epi_scale(acc, *, scale):
    return acc * scale


def _epi_rms(acc, w_ref):
    return acc * lax.rsqrt(jnp.mean(acc * acc, axis=-1, keepdims=True) + RMS_EPS) * w_ref[...]


def _epi_rope(acc, cos_ref, sin_ref, *, scale):
    half = R_QK_DIM // 2
    c, s = cos_ref[...] * scale, sin_ref[...] * scale
    out = []
    for h in range(acc.shape[-1] // R_QK_DIM):
        x1 = acc[:, h * R_QK_DIM:h * R_QK_DIM + half]
        x2 = acc[:, h * R_QK_DIM + half:(h + 1) * R_QK_DIM]
        out += [x1 * c - x2 * s, x1 * s + x2 * c]
    return jnp.concatenate(out, axis=-1)


def _epi_gate(acc, g_ref):
    return _sigmoid(g_ref[...]) * acc


def _epi_gate_add(acc, g_ref, add_ref):
    return _sigmoid(g_ref[...]) * acc + add_ref[...]


def _epi_ln(acc, res_ref, w_ref, b_ref, *, alpha):
    y = _layer_norm_rows(alpha * res_ref[...] + acc, w_ref[...], b_ref[...])
    return y, y


def _float_key(x):
    bits = pltpu.bitcast(x, jnp.int32)
    return bits ^ ((bits >> 31) & jnp.int32(0x7FFFFFFF))


def _dsa_kernel(bias15_ref, qa_ref, qidx_ref, widx_ref, kidx_ref, ckv_ref, toep_ref, wuk_ref, wuv_ref,
                out_ref, scf_ref, scn_ref, tmp0_ref, tmp1_ref, zb0_ref, zb1_ref, madd_ref, qa2_ref, qi2_ref,
                wb_ref, p0_ref, p1_ref, acc_ref, m_ref, l_ref, al0_ref, al1_ref, *, topk):
    i = pl.program_id(1)
    t0 = i * Q_BLOCK
    far_end = t0 - Q_BLOCK
    n_far = (jnp.maximum(i - 1, 0) * Q_BLOCK + FAR_TILE - 1) // FAR_TILE
    n_ch = FAR_TILE // LANES
    H = A_HEADS
    QB = Q_BLOCK

    for h in range(H):
        q_abs = _dot(qa_ref[:, h * A_HEAD_DIM:(h + 1) * A_HEAD_DIM], wuk_ref[h])
        qa2_ref[h * QB:(h + 1) * QB, :] = (q_abs * (A_HEAD_DIM ** -0.5 * LOG2E)).astype(BF16)
        qi2_ref[h * QB:(h + 1) * QB, :] = qidx_ref[:, h * IDX_DIM:(h + 1) * IDX_DIM]
        wb_ref[h] = jnp.broadcast_to(widx_ref[:, h:h + 1], (QB, LANES))

    lane = lax.broadcasted_iota(jnp.int32, (QB, LANES), 1)
    row = lax.broadcasted_iota(jnp.int32, (QB, LANES), 0)

    tmp = (tmp0_ref, tmp1_ref)
    zbuf = (zb0_ref, zb1_ref)
    pbuf = (p0_ref, p1_ref)
    albuf = (al0_ref, al1_ref)
    n_pairs = (n_far + 1) // 2
    last_start = kidx_ref.shape[0] - FAR_TILE

    def far_rows(kt):
        return pl.ds(pl.multiple_of(jnp.clip(Q_BLOCK + kt * FAR_TILE, Q_BLOCK, last_start), LANES), FAR_TILE)

    def head_sum(z_ref, c):
        acc = jnp.zeros((QB, LANES), F32)
        for h in range(H):
            z = z_ref[h * QB:(h + 1) * QB, c * LANES:(c + 1) * LANES]
            acc = acc + wb_ref[h] * jnp.maximum(z, 0.0)
        return acc

    def far_scores(z_ref, kt):
        for c in range(n_ch):
            key = _float_key(head_sum(z_ref, c))
            s_pos = kt * FAR_TILE + c * LANES + lane
            scf_ref[kt, :, c * LANES:(c + 1) * LANES] = jnp.where(s_pos < far_end, key, INT_MIN)

    def idx_matmul(z_ref, kt):
        z_ref[...] = _dot_nt(qi2_ref[...], kidx_ref[far_rows(kt), :])

    idx_matmul(tmp[0], 0)

    def score_pair(j, carry):
        kt = 2 * j
        idx_matmul(tmp[1], kt + 1)
        far_scores(tmp[0], kt)
        idx_matmul(tmp[0], kt + 2)
        far_scores(tmp[1], kt + 1)
        return carry

    lax.fori_loop(0, n_pairs, score_pair, 0)

    near_start = pl.multiple_of(t0, LANES)
    tmp[0][:, :2 * LANES] = _dot_nt(qi2_ref[...], kidx_ref[pl.ds(near_start, 2 * LANES), :])
    u_lo = jnp.where(i == 0, QB, 0)
    u_hi = jnp.where(row < CHUNK, QB + CHUNK, 2 * QB)
    for c in range(2):
        key = _float_key(head_sum(tmp[0], c))
        u = c * LANES + lane
        vis = jnp.logical_and(u >= u_lo, u < u_hi)
        scn_ref[:, c * LANES:(c + 1) * LANES] = jnp.where(vis, key, INT_MIN)

    def bit_body(bi, p):
        bit = jnp.left_shift(jnp.int32(1), 31 - bi)
        p_try = p | bit
        t_try = p_try ^ jnp.int32(INT_MIN)

        def cnt_body(j, cnt):
            for kt in (2 * j, 2 * j + 1):
                for c in range(n_ch):
                    cnt = cnt + jnp.where(scf_ref[kt, :, c * LANES:(c + 1) * LANES] >= t_try, 1, 0)
            return cnt

        cnt = lax.fori_loop(0, n_pairs, cnt_body, jnp.zeros((QB, LANES), jnp.int32))
        for c in range(2):
            cnt = cnt + jnp.where(scn_ref[:, c * LANES:(c + 1) * LANES] >= t_try, 1, 0)
        total = jnp.sum(cnt, axis=1, keepdims=True)
        return jnp.where(total >= topk, p_try, p)

    p_fin = lax.fori_loop(0, 32, bit_body, jnp.zeros((QB, LANES), jnp.int32))
    thr = jnp.maximum(p_fin ^ jnp.int32(INT_MIN), jnp.int32(INT_MIN + 1))

    m_ref[...] = jnp.full(m_ref.shape, NEG_BIG, F32)
    l_ref[...] = jnp.zeros(l_ref.shape, F32)
    acc_ref[...] = jnp.zeros(acc_ref.shape, F32)

    def to_mask(kt, carry):
        for c in range(n_ch):
            cols = slice(c * LANES, (c + 1) * LANES)
            mask = jnp.where(scf_ref[kt, :, cols] >= thr, 0.0, -jnp.inf)
            scf_ref[kt, :, cols] = pltpu.bitcast(mask, jnp.int32)
        return carry

    lax.fori_loop(0, n_far, to_mask, 0)
    for c in range(2):
        cols = slice(c * LANES, (c + 1) * LANES)
        madd_ref[:, cols] = jnp.where(scn_ref[:, cols] >= thr, 0.0, -jnp.inf)

    GR = HEAD_GROUP * QB
    n_grp = H // HEAD_GROUP

    def rows_at(start, size):
        return pl.ds(start if isinstance(start, int) else pl.multiple_of(start, size), size)

    def group_rows(hg):
        return rows_at(hg * GR, GR)

    def softmax_group(width, near, z_ref, p_ref, al_ref, kt, hg):
        chunks = [slice(c * LANES, (c + 1) * LANES) for c in range(width // LANES)]
        for hh in range(HEAD_GROUP):
            h = hg * HEAD_GROUP + hh
            rows = slice(hh * QB, (hh + 1) * QB)
            stat = rows_at(h * QB, QB)

            def masked_logits(ch):
                if near:
                    v = z_ref[rows, ch] + madd_ref[:, ch] + toep_ref[h, :, ch]
                else:
                    v = z_ref[rows, ch] + pltpu.bitcast(scf_ref[kt, :, ch], F32)
                z_ref[rows, ch] = v
                return v

            mx = masked_logits(chunks[0])
            for ch in chunks[1:]:
                mx = jnp.maximum(mx, masked_logits(ch))
            mx = jnp.max(mx, axis=1, keepdims=True)
            bias = 0.0 if near else bias15_ref[h]
            m_old = m_ref[stat, :]
            m_new = jnp.maximum(m_old, mx + bias)
            shift = m_new - bias
            psum = jnp.zeros((QB, LANES), F32)
            for ch in chunks:
                p = jnp.exp2(z_ref[rows, ch] - shift)
                psum = psum + p
                p_ref[rows, ch] = p.astype(BF16)
            alpha = jnp.exp2(m_old - m_new)
            l_ref[stat, :] = alpha * l_ref[stat, :] + jnp.sum(psum, axis=1, keepdims=True)
            m_ref[stat, :] = m_new
            al_ref[rows, :] = alpha

    def pv_group(p_ref, al_ref, c_tile, width, hg):
        al = al_ref[...]
        rows = group_rows(hg)
        acc_ref[rows, :] = (acc_ref[rows, :] * jnp.concatenate([al] * (A_LATENT // LANES), axis=1)
                            + _dot(p_ref[:, :width], c_tile))

    def far_logits(z_ref, kt, hg):
        z_ref[...] = _dot_nt(qa2_ref[group_rows(hg), :], ckv_ref[far_rows(kt), :])

    def far_pv(slot, kt, hg):
        pv_group(pbuf[slot], albuf[slot], ckv_ref[far_rows(kt), :], FAR_TILE, hg)

    p1_ref[...] = jnp.zeros(p1_ref.shape, BF16)
    al1_ref[...] = jnp.ones(al1_ref.shape, F32)
    far_logits(zbuf[0], 0, 0)

    def attn_pair(j, carry):
        unit = lambda g: (g // n_grp, g % n_grp)
        g = 2 * j
        far_logits(zbuf[1], *unit(g + 1))
        softmax_group(FAR_TILE, False, zbuf[0], pbuf[0], albuf[0], *unit(g))
        far_pv(1, *unit(g - 1))
        far_logits(zbuf[0], *unit(g + 2))
        softmax_group(FAR_TILE, False, zbuf[1], pbuf[1], albuf[1], *unit(g + 1))
        far_pv(0, *unit(g))
        return carry

    lax.fori_loop(0, n_far * (n_grp // 2), attn_pair, 0)
    far_pv(1, n_far - 1, n_grp - 1)

    c_near = ckv_ref[pl.ds(near_start, 2 * LANES), :]

    for hg in range(n_grp):
        s = hg % 2
        zbuf[s][:, :2 * LANES] = _dot_nt(qa2_ref[group_rows(hg), :], c_near)
        softmax_group(2 * LANES, True, zbuf[s], pbuf[s], albuf[s], 0, hg)
        pv_group(pbuf[s], albuf[s], c_near, 2 * LANES, hg)

    for h in range(H):
        rows = slice(h * QB, (h + 1) * QB)
        inv = 1.0 / l_ref[rows, :]
        o_lat = jnp.concatenate([acc_ref[rows, c * LANES:(c + 1) * LANES] * inv
                                 for c in range(A_LATENT // LANES)], axis=1).astype(BF16)
        out_ref[:, h * A_HEAD_DIM:(h + 1) * A_HEAD_DIM] = _dot(o_lat, wuv_ref[h]).astype(out_ref.dtype)


def _dsa(q_a, qidx, widx, kidx_pad, ckv_pad, toep, bias15, w_uk_t, w_uv, *, B, S):
    nqb = S // Q_BLOCK
    topk = min(TOPK_MAX, S // 4)
    n_far_max = max(1, -(-(S - 2 * Q_BLOCK) // FAR_TILE))
    HQ = A_HEADS * Q_BLOCK
    GR = HEAD_GROUP * Q_BLOCK
    resident = dict(pipeline_mode=pl.Buffered(1))
    return pl.pallas_call(
        functools.partial(_dsa_kernel, topk=topk),
        grid=(B, nqb),
        in_specs=[
            pl.BlockSpec(memory_space=pltpu.SMEM),
            pl.BlockSpec((Q_BLOCK, A_HEADS * A_HEAD_DIM), lambda b, i: (b * nqb + i, 0)),
            pl.BlockSpec((Q_BLOCK, IDX_HEADS * IDX_DIM), lambda b, i: (b * nqb + i, 0)),
            pl.BlockSpec((Q_BLOCK, LANES), lambda b, i: (b * nqb + i, 0)),
            pl.BlockSpec((None, S + Q_BLOCK, IDX_DIM), lambda b, i: (b, 0, 0), **resident),
            pl.BlockSpec((None, S + Q_BLOCK, A_LATENT), lambda b, i: (b, 0, 0), **resident),
            pl.BlockSpec((A_HEADS, Q_BLOCK, 2 * LANES), lambda b, i: (0, 0, 0), **resident),
            pl.BlockSpec((A_HEADS, A_HEAD_DIM, A_LATENT), lambda b, i: (0, 0, 0), **resident),
            pl.BlockSpec((A_HEADS, A_LATENT, A_HEAD_DIM), lambda b, i: (0, 0, 0), **resident),
        ],
        out_specs=pl.BlockSpec((Q_BLOCK, A_HEADS * A_HEAD_DIM), lambda b, i: (b * nqb + i, 0)),
        out_shape=jax.ShapeDtypeStruct((B * S, A_HEADS * A_HEAD_DIM), BF16),
        scratch_shapes=[
            pltpu.VMEM((n_far_max + 1, Q_BLOCK, FAR_TILE), jnp.int32),
            pltpu.VMEM((Q_BLOCK, 2 * LANES), jnp.int32),
            pltpu.VMEM((HQ, FAR_TILE), F32),
            pltpu.VMEM((HQ, FAR_TILE), F32),
            pltpu.VMEM((GR, FAR_TILE), F32),
            pltpu.VMEM((GR, FAR_TILE), F32),
            pltpu.VMEM((Q_BLOCK, 2 * LANES), F32),
            pltpu.VMEM((HQ, A_LATENT), BF16),
            pltpu.VMEM((HQ, IDX_DIM), BF16),
            pltpu.VMEM((IDX_HEADS, Q_BLOCK, LANES), F32),
            pltpu.VMEM((GR, FAR_TILE), BF16),
            pltpu.VMEM((GR, FAR_TILE), BF16),
            pltpu.VMEM((HQ, A_LATENT), F32),
            pltpu.VMEM((HQ, LANES), F32),
            pltpu.VMEM((HQ, LANES), F32),
            pltpu.VMEM((GR, LANES), F32),
            pltpu.VMEM((GR, LANES), F32),
        ],
        compiler_params=_cparams("parallel", "arbitrary"),
        name="dsa",
    )(bias15, q_a, qidx, widx, kidx_pad, ckv_pad, toep, w_uk_t, w_uv)


def _ret_kernel(gtot_ref, q_ref, k_ref, v_ref, g_ref, d_ref, xi_ref, zeta_ref, gnw_ref, gnb_ref,
                o_ref, state_ref):
    @pl.when(pl.program_id(2) == 0)
    def _():
        state_ref[...] = jnp.zeros(state_ref.shape, F32)

    h = pl.program_id(1)
    q, k, v = q_ref[...], k_ref[...], v_ref[...]
    s = _dot_nt(q, k) * d_ref[0]
    o = _dot(s.astype(BF16), v) + _dot(q, state_ref[...].astype(BF16)) * xi_ref[0]
    kz = (k.astype(F32) * zeta_ref[0]).astype(BF16)
    upd = lax.dot_general(kz, v, (((0,), (0,)), ((), ())), preferred_element_type=F32)
    state_ref[...] = state_ref[...] * gtot_ref[h] + upd

    mu = jnp.mean(o, axis=-1, keepdims=True)
    d = o - mu
    var = jnp.mean(d * d, axis=-1, keepdims=True)
    y = d * lax.rsqrt(var + LN_EPS) * gnw_ref[...] + gnb_ref[...]
    g = g_ref[...]
    o_ref[...] = (g * _sigmoid(g) * y).astype(o_ref.dtype)


def _retention(q_rot, k_rot, v, g_r, gn_w, gn_b, consts, *, B, S):
    d_mat, xi, zeta, gtot = consts
    ng = S // RET_GROUP
    G = RET_GROUP
    return pl.pallas_call(
        _ret_kernel,
        grid=(B, R_HEADS, ng),
        in_specs=[
            pl.BlockSpec(memory_space=pltpu.SMEM),
            pl.BlockSpec((G, R_QK_DIM), lambda b, h, g: (b * ng + g, h)),
            pl.BlockSpec((G, R_QK_DIM), lambda b, h, g: (b * ng + g, h)),
            pl.BlockSpec((G, R_V_DIM), lambda b, h, g: (b * ng + g, h)),
            pl.BlockSpec((G, R_V_DIM), lambda b, h, g: (b * ng + g, h)),
            pl.BlockSpec((1, G, G), lambda b, h, g: (h, 0, 0)),
            pl.BlockSpec((1, G, 1), lambda b, h, g: (h, 0, 0)),
            pl.BlockSpec((1, G, 1), lambda b, h, g: (h, 0, 0)),
            pl.BlockSpec((1, R_V_DIM), lambda b, h, g: (0, h)),
            pl.BlockSpec((1, R_V_DIM), lambda b, h, g: (0, h)),
        ],
        out_specs=pl.BlockSpec((G, R_V_DIM), lambda b, h, g: (b * ng + g, h)),
        out_shape=jax.ShapeDtypeStruct((B * S, R_HEADS * R_V_DIM), BF16),
        scratch_shapes=[pltpu.VMEM((R_QK_DIM, R_V_DIM), F32)],
        compiler_params=_cparams("parallel", "parallel", "arbitrary"),
        name="retention",
    )(gtot, q_rot, k_rot, v, g_r, d_mat, xi, zeta, gn_w, gn_b)


def _retention_consts():
    G = RET_GROUP
    log_g = jnp.log1p(-jnp.exp2(-5.0 - jnp.arange(R_HEADS, dtype=F32)))
    pos = jnp.arange(G, dtype=F32)
    diff = pos[:, None] - pos[None, :]
    ci = jnp.arange(G)[:, None] // CHUNK
    cj = jnp.arange(G)[None, :] // CHUNK
    same = jnp.exp(log_g[:, None, None] * jnp.abs(diff))
    earlier = jnp.exp(log_g[:, None, None] * diff)
    d_mat = jnp.where(ci == cj, same, jnp.where(cj < ci, earlier, 0.0))
    xi = jnp.exp(log_g[:, None] * (pos[None, :] + 1.0))[..., None]
    zeta = jnp.exp(log_g[:, None] * (G - 1.0 - pos[None, :]))[..., None]
    gtot = jnp.exp(log_g * G)
    return d_mat, xi, zeta, gtot


def _rank_rows(v):
    n = v.shape[0]
    ridx = lax.broadcasted_iota(jnp.int32, v.shape, 0)
    rank = jnp.zeros(v.shape, jnp.int32)
    for j in range(n):
        rj = v[j:j + 1, :]
        rank = rank + jnp.where(ridx > j, jnp.where(rj >= v, 1, 0), jnp.where(rj > v, 1, 0))
    return rank


def _router_kernel(x_ref, wr_ref, b_ref, gt_ref):
    st = _dot_nt(wr_ref[...], x_ref[...])
    sig = _sigmoid(st)
    biased = sig + b_ref[...]
    per = N_EXPERTS // N_GROUPS
    blocks = [biased[g * per:(g + 1) * per, :] for g in range(N_GROUPS)]
    gscore = []
    for blk in blocks:
        top2 = jnp.where(_rank_rows(blk) < 2, blk, 0.0)
        gscore.append(jnp.sum(top2, axis=0, keepdims=True))
    masked = []
    for g in range(N_GROUPS):
        grank = jnp.zeros(gscore[g].shape, jnp.int32)
        for g2 in range(N_GROUPS):
            if g2 == g:
                continue
            beats = (gscore[g2] >= gscore[g]) if g2 < g else (gscore[g2] > gscore[g])
            grank = grank + jnp.where(beats, 1, 0)
        keep = jnp.broadcast_to(grank, blocks[g].shape) < TOPK_GROUPS
        masked.append(jnp.where(keep, blocks[g], -jnp.inf))
    cand = jnp.concatenate(masked, axis=0)
    sel = _rank_rows(cand) < TOP_K
    gates = jnp.where(sel, sig, 0.0)
    denom = jnp.sum(gates, axis=0, keepdims=True)
    gt_ref[...] = gates / denom * ROUTED_SCALE


def _router(xb, wr_t, b_col, *, tm):
    N = xb.shape[0]
    return pl.pallas_call(
        _router_kernel,
        grid=(N // tm,),
        in_specs=[
            pl.BlockSpec((tm, xb.shape[1]), lambda i: (i, 0)),
            pl.BlockSpec(wr_t.shape, lambda i: (0, 0)),
            pl.BlockSpec(b_col.shape, lambda i: (0, 0)),
        ],
        out_specs=pl.BlockSpec((N_EXPERTS, tm), lambda i: (0, i)),
        out_shape=jax.ShapeDtypeStruct((N_EXPERTS, N), F32),
        compiler_params=_cparams("parallel"),
        name="router",
    )(xb, wr_t, b_col)


def _swiglu(xb, wg, wu, wd):
    hg = _dot(xb, wg)
    hidden = (hg * _sigmoid(hg) * _dot(xb, wu)).astype(BF16)
    return _dot(hidden, wd)


def _moe_kernel(nsub_ref, x_ref, rk_ref, g_ref, wg_ref, wu_ref, wd_ref, sg_ref, su_ref, sd_ref,
                lnw_ref, lnb_ref, of_ref, ob_ref, xb_ref, acc_ref, *, alpha):
    i = pl.program_id(0)
    s = pl.program_id(1)
    n_pairs = pl.num_programs(1) - MOE_SUPER
    W = MOE_WINDOW

    @pl.when(s == 0)
    def _():
        xb_ref[...] = x_ref[...].astype(BF16)
        acc_ref[...] = jnp.zeros(acc_ref.shape, F32)

    @pl.when(s < n_pairs)
    def _():
        row_id = lax.broadcasted_iota(jnp.int32, (MOE_ROWS, W), 0)
        for k in range(MOE_SUPER):
            win = slice(k * W, (k + 1) * W)

            def sub_tile(j, carry, k=k, win=win):
                picks, outs = [], []
                for q in range(2):
                    hit = (row_id + j * MOE_ROWS) == rk_ref[k, q:q + 1, :]
                    onehot = jnp.where(hit, 1.0, 0.0)
                    pick = onehot.astype(BF16)
                    xs = _dot(pick, xb_ref[win, :]).astype(BF16)
                    y = _swiglu(xs, wg_ref[q], wu_ref[q], wd_ref[q])
                    gate = jnp.sum(onehot * g_ref[k, q:q + 1, :], axis=1, keepdims=True)
                    picks.append(pick)
                    outs.append((y * gate).astype(BF16))
                acc_ref[win, :] += lax.dot_general(
                    jnp.concatenate(picks, axis=0), jnp.concatenate(outs, axis=0),
                    (((0,), (0,)), ((), ())), preferred_element_type=F32)
                return carry

            lax.fori_loop(0, nsub_ref[(i * MOE_SUPER + k) * n_pairs + s], sub_tile, 0)

    for k in range(MOE_SUPER):
        @pl.when(s == n_pairs + k)
        def _(k=k):
            win = slice(k * W, (k + 1) * W)
            shared = _swiglu(xb_ref[win, :], sg_ref[...], su_ref[...], sd_ref[...])
            y = _layer_norm_rows(alpha * x_ref[win, :] + acc_ref[win, :] + shared, lnw_ref[...], lnb_ref[...])
            of_ref[...] = y
            ob_ref[...] = y.astype(BF16)


def _moe(x1, gt, p, *, alpha):
    N, D = x1.shape
    W = MOE_WINDOW
    nw, n_pairs = N // W, N_EXPERTS // 2
    sel = (gt != 0.0).reshape(N_EXPERTS, nw, W)
    rank = jnp.where(sel, jnp.cumsum(sel.astype(jnp.int32), axis=2) - 1, -1)
    SW = MOE_SUPER
    per_pair = lambda a: jnp.transpose(a.reshape(n_pairs, 2, nw // SW, SW, W), (2, 0, 3, 1, 4))
    n_sub = (jnp.sum(sel, axis=2, dtype=jnp.int32) + MOE_ROWS - 1) // MOE_ROWS
    n_sub = jnp.max(n_sub.T.reshape(nw, n_pairs, 2), axis=2).reshape(-1)
    pair = lambda i, s, n: (jnp.minimum(s, n_pairs - 1), 0, 0)
    resident = dict(pipeline_mode=pl.Buffered(1))
    const2 = lambda i, s, n: (0, 0)
    routed = lambda i, s, n: (i, jnp.minimum(s, n_pairs - 1), 0, 0, 0)
    out_win = lambda i, s, n: (i * SW + jnp.clip(s - n_pairs, 0, SW - 1), 0)
    grid_spec = pltpu.PrefetchScalarGridSpec(
        num_scalar_prefetch=1,
        grid=(nw // SW, n_pairs + SW),
        in_specs=[
            pl.BlockSpec((SW * W, D), lambda i, s, n: (i, 0), **resident),
            pl.BlockSpec((None, None, SW, 2, W), routed),
            pl.BlockSpec((None, None, SW, 2, W), routed),
            pl.BlockSpec((2, D, EXPERT_DIM), pair),
            pl.BlockSpec((2, D, EXPERT_DIM), pair),
            pl.BlockSpec((2, EXPERT_DIM, D), pair),
            pl.BlockSpec((D, EXPERT_DIM), const2, **resident),
            pl.BlockSpec((D, EXPERT_DIM), const2, **resident),
            pl.BlockSpec((EXPERT_DIM, D), const2, **resident),
            pl.BlockSpec((1, D), const2),
            pl.BlockSpec((1, D), const2),
        ],
        out_specs=[pl.BlockSpec((W, D), out_win), pl.BlockSpec((W, D), out_win)],
        scratch_shapes=[pltpu.VMEM((SW * W, D), BF16), pltpu.VMEM((SW * W, D), F32)],
    )
    return pl.pallas_call(
        functools.partial(_moe_kernel, alpha=alpha),
        grid_spec=grid_spec,
        out_shape=[jax.ShapeDtypeStruct((N, D), F32), jax.ShapeDtypeStruct((N, D), BF16)],
        compiler_params=_cparams("parallel", "arbitrary"),
        name="moe",
    )(n_sub, x1, per_pair(rank), per_pair(gt.reshape(N_EXPERTS, nw, W)),
      p['we_gate'].astype(BF16), p['we_up'].astype(BF16), p['we_down'].astype(BF16),
      p['ws_gate'].astype(BF16), p['ws_up'].astype(BF16), p['ws_down'].astype(BF16),
      p['ln2_w'][None, :], p['ln2_b'][None, :])


def _t5_bucket(rel):
    half = REL_BUCKETS // 2
    max_exact = half // 2
    ret = jnp.where(rel > 0, half, 0)
    n = jnp.abs(rel)
    nf = jnp.maximum(n, 1).astype(F32)
    large = max_exact + (jnp.log(nf / max_exact) / math.log(REL_MAX_DIST / max_exact)
                         * (half - max_exact)).astype(jnp.int32)
    large = jnp.minimum(large, half - 1)
    return ret + jnp.where(n < max_exact, n, large)


def _bias_tables(rel_bias):
    i = jnp.arange(Q_BLOCK)[:, None]
    u = jnp.arange(2 * Q_BLOCK)[None, :]
    toep = jnp.transpose(rel_bias[_t5_bucket(u - Q_BLOCK - i)], (2, 0, 1))
    far = rel_bias[_t5_bucket(jnp.asarray(-(Q_BLOCK + 1)))]
    return toep.astype(F32) * LOG2E, far.astype(F32) * LOG2E


def _rope_tables(S):
    half = R_QK_DIM // 2
    inv = ROPE_BASE ** (-jnp.arange(half, dtype=F32) / half)
    ang = jnp.arange(S, dtype=F32)[:, None] * inv[None, :]
    return jnp.cos(ang), jnp.sin(ang)


def _layer(x, xb, tables, p, *, B, S, alpha):
    N, D = x.shape
    cos, sin, toep, bias15, ret_consts = tables
    TM = 512 if N % 512 == 0 else 256
    a_w = A_HEADS * A_HEAD_DIM
    cuts = np.cumsum([0, a_w, A_LATENT, IDX_HEADS * IDX_DIM, IDX_DIM, IDX_HEADS,
                      R_HEADS * R_QK_DIM, R_HEADS * R_QK_DIM, R_HEADS * R_V_DIM, R_HEADS * R_V_DIM, D, D])
    w_in = p['w_in']
    piece = lambda k: w_in[:, int(cuts[k]):int(cuts[k + 1])].astype(BF16)
    row_spec = lambda w: pl.BlockSpec((1, w), lambda i, j: (0, 0))

    TMB = 1024 if (N % 1024 == 0 and S % 1024 == 0) else TM
    big = dict(tm=TMB, tn=1024, epilogue=_epi_plain)
    q_a = _mm(xb, piece(0), out_dtypes=[BF16], name="proj_qa", **big)
    c_kv = _mm(xb, piece(1), tm=TMB, tn=A_LATENT, out_dtypes=[BF16], epilogue=_epi_rms,
               extras=[p['ckv_norm'][None, :]], extra_specs=[row_spec(A_LATENT)], name="proj_ckv")
    q_idx = _mm(xb, piece(2), out_dtypes=[BF16], name="proj_qidx", **big)
    k_idx = _mm(xb, piece(3), tm=TMB, tn=IDX_DIM, out_dtypes=[BF16], epilogue=_epi_rms,
                extras=[p['kidx_norm'][None, :]], extra_specs=[row_spec(IDX_DIM)], name="proj_kidx")
    w_widx = jnp.pad(piece(4), ((0, 0), (0, LANES - IDX_HEADS)))
    w_idx = _mm(xb, w_widx, tm=TMB, tn=LANES, out_dtypes=[F32],
                epilogue=functools.partial(_epi_scale, scale=IDX_HEADS ** -0.5 * IDX_DIM ** -0.5),
                name="proj_widx")
    pos_spec = pl.BlockSpec((TMB, R_QK_DIM // 2), lambda i, j: (i % (S // TMB), 0))
    q_r = _mm(xb, piece(5), tm=TMB, tn=2 * R_QK_DIM, out_dtypes=[BF16],
              epilogue=functools.partial(_epi_rope, scale=1.0),
              extras=[cos, sin], extra_specs=[pos_spec, pos_spec], name="proj_qr")
    k_r = _mm(xb, piece(6), tm=TMB, tn=2 * R_QK_DIM, out_dtypes=[BF16],
              epilogue=functools.partial(_epi_rope, scale=R_QK_DIM ** -0.5),
              extras=[cos, sin], extra_specs=[pos_spec, pos_spec], name="proj_kr")
    v_r = _mm(xb, piece(7), out_dtypes=[BF16], name="proj_vr", **big)
    g_r = _mm(xb, piece(8), out_dtypes=[F32], name="proj_gr", **big)
    g_a = _mm(xb, piece(9), out_dtypes=[F32], name="proj_ga", **big)
    g_b = _mm(xb, piece(10), out_dtypes=[F32], name="proj_gb", **big)

    w_uk_t = jnp.transpose(p['w_uk'], (0, 2, 1)).astype(BF16)
    pad = lambda a: jnp.pad(a.reshape(B, S, -1), ((0, 0), (Q_BLOCK, 0), (0, 0)))
    y_a = _dsa(q_a, q_idx, w_idx, pad(k_idx), pad(c_kv), toep, bias15, w_uk_t, p['w_uv'].astype(BF16),
               B=B, S=S)

    y_r = _retention(q_r, k_r, v_r, g_r, p['gn_w'][None, :], p['gn_b'][None, :], ret_consts, B=B, S=S)

    tile_spec = pl.BlockSpec((TM, 1024), lambda i, j: (i, j))
    m_a = _mm(y_a, p['w_pa'].astype(BF16), tm=TM, tn=1024, out_dtypes=[F32], epilogue=_epi_gate,
              extras=[g_a], extra_specs=[tile_spec], name="proj_a")
    merged = _mm(y_r, p['w_pb'].astype(BF16), tm=TM, tn=1024, out_dtypes=[BF16], epilogue=_epi_gate_add,
                 extras=[g_b, m_a], extra_specs=[tile_spec, tile_spec], name="proj_b")
    full_row = pl.BlockSpec((TM, D), lambda i, j: (i, 0))
    x1, x1b = _mm(merged, p['w_o'].astype(BF16), tm=TM, tn=D, out_dtypes=[F32, BF16],
                  epilogue=functools.partial(_epi_ln, alpha=alpha),
                  extras=[x, p['ln1_w'][None, :], p['ln1_b'][None, :]],
                  extra_specs=[full_row, row_spec(D), row_spec(D)], name="out_ln1")

    gt = _router(x1b, p['w_router'].T.astype(BF16), p['b_router'][:, None], tm=TM)
    return _moe(x1, gt, p, alpha=alpha)


def kernel(x, rel_bias, w_in, ckv_norm, kidx_norm, w_uk, w_uv, gn_w, gn_b, w_pa, w_pb, w_o, ln1_w, ln1_b,
           w_router, b_router, we_gate, we_up, we_down, ws_gate, ws_up, ws_down, ln2_w, ln2_b):
    B, S, D = x.shape
    depth = w_in.shape[0]
    alpha = (2 * depth) ** 0.25
    cos, sin = _rope_tables(S)
    toep, bias15 = _bias_tables(rel_bias)
    tables = (cos, sin, toep, bias15, _retention_consts())
    params = dict(w_in=w_in, ckv_norm=ckv_norm, kidx_norm=kidx_norm, w_uk=w_uk, w_uv=w_uv, gn_w=gn_w,
                  gn_b=gn_b, w_pa=w_pa, w_pb=w_pb, w_o=w_o, ln1_w=ln1_w, ln1_b=ln1_b, w_router=w_router,
                  b_router=b_router, we_gate=we_gate, we_up=we_up, we_down=we_down, ws_gate=ws_gate,
                  ws_up=ws_up, ws_down=ws_down, ln2_w=ln2_w, ln2_b=ln2_b)
    xf = x.reshape(B * S, D)
    xb = xf.astype(BF16)
    for l in range(depth):
        xf, xb = _layer(xf, xb, tables, {k: v[l] for k, v in params.items()}, B=B, S=S, alpha=alpha)
    return xf.reshape(B, S, D)
```

```python
import functools
import math

import jax
import jax.numpy as jnp
import numpy as np
from jax import lax
from jax.experimental import pallas as pl
from jax.experimental.pallas import tpu as pltpu

CHUNK = 64
Q_BLOCK = 128
A_HEADS = 16
A_HEAD_DIM = 128
A_LATENT = 256
IDX_HEADS = 16
IDX_DIM = 128
TOPK_MAX = 256
REL_BUCKETS = 32
REL_MAX_DIST = 128
R_HEADS = 8
R_QK_DIM = 256
R_V_DIM = 512
ROPE_BASE = 10000.0
N_EXPERTS = 64
EXPERT_DIM = 256
TOP_K = 8
N_GROUPS = 8
TOPK_GROUPS = 4
ROUTED_SCALE = 2.5
LN_EPS = 1e-5
RMS_EPS = 1e-6

LANES = 128
FAR_TILE = 512
HEAD_GROUP = 4
ATTN_TILES = 1
RET_GROUP = 512
MOE_WINDOW = 512
MOE_SUPER = 2
MOE_ROWS = 128
VMEM_LIMIT = 56 * 1024 * 1024
INT_MIN = -2 ** 31
NEG_BIG = -1e30
LOG2E = 1.4426950408889634

F32 = jnp.float32
BF16 = jnp.bfloat16


def _cparams(*sem):
    return pltpu.CompilerParams(dimension_semantics=sem, vmem_limit_bytes=VMEM_LIMIT)


def _sigmoid(x):
    return 1.0 / (1.0 + jnp.exp(-x))


def _dot(a, b):
    return jnp.dot(a, b, preferred_element_type=F32)


def _dot_nt(a, b):
    return lax.dot_general(a, b, (((1,), (1,)), ((), ())), preferred_element_type=F32)


def _layer_norm_rows(v, w, b):
    mu = jnp.mean(v, axis=-1, keepdims=True)
    d = v - mu
    var = jnp.mean(d * d, axis=-1, keepdims=True)
    return d * lax.rsqrt(var + LN_EPS) * w + b


def _mm_kernel(*refs, epilogue, n_extra, n_out):
    a_ref, b_ref = refs[0], refs[1]
    extra = refs[2:2 + n_extra]
    outs = refs[2 + n_extra:2 + n_extra + n_out]
    acc = _dot(a_ref[...], b_ref[...])
    res = epilogue(acc, *extra)
    if not isinstance(res, tuple):
        res = (res,)
    for o_ref, r in zip(outs, res):
        o_ref[...] = r.astype(o_ref.dtype)


def _mm(a, b, *, tm, tn, out_dtypes, epilogue, extras=(), extra_specs=(), name):
    M = a.shape[0]
    K, n_cols = b.shape
    grid = (M // tm, n_cols // tn)
    a_spec = pl.BlockSpec((tm, K), lambda i, j: (i, 0))
    b_spec = pl.BlockSpec((K, tn), lambda i, j: (0, j))
    out_shape = [jax.ShapeDtypeStruct((M, n_cols), dt) for dt in out_dtypes]
    out_specs = [pl.BlockSpec((tm, tn), lambda i, j: (i, j)) for _ in out_dtypes]
    res = pl.pallas_call(
        functools.partial(_mm_kernel, epilogue=epilogue, n_extra=len(extras), n_out=len(out_dtypes)),
        grid=grid,
        in_specs=[a_spec, b_spec, *extra_specs],
        out_specs=out_specs,
        out_shape=out_shape,
        compiler_params=_cparams("parallel", "arbitrary"),
        name=name,
    )(a, b, *extras)
    return res[0] if len(res) == 1 else res


def _epi_plain(acc):
    return acc


def _epi_scale(acc, *, scale):
    return acc * scale


def _epi_rms(acc, w_ref):
    return acc * lax.rsqrt(jnp.mean(acc * acc, axis=-1, keepdims=True) + RMS_EPS) * w_ref[...]


def _epi_rope(acc, cos_ref, sin_ref, *, scale):
    half = R_QK_DIM // 2
    c, s = cos_ref[...] * scale, sin_ref[...] * scale
    out = []
    for h in range(acc.shape[-1] // R_QK_DIM):
        x1 = acc[:, h * R_QK_DIM:h * R_QK_DIM + half]
        x2 = acc[:, h * R_QK_DIM + half:(h + 1) * R_QK_DIM]
        out += [x1 * c - x2 * s, x1 * s + x2 * c]
    return jnp.concatenate(out, axis=-1)


def _epi_gate(acc, g_ref):
    return _sigmoid(g_ref[...]) * acc


def _epi_gate_add(acc, g_ref, add_ref):
    return _sigmoid(g_ref[...]) * acc + add_ref[...]


def _epi_ln(acc, res_ref, w_ref, b_ref, *, alpha):
    y = _layer_norm_rows(alpha * res_ref[...] + acc, w_ref[...], b_ref[...])
    return y, y


def _float_key(x):
    bits = pltpu.bitcast(x, jnp.int32)
    return bits ^ ((bits >> 31) & jnp.int32(0x7FFFFFFF))


def _dsa_kernel(bias15_ref, qa_ref, qidx_ref, widx_ref, kidx_ref, ckv_ref, toep_ref, wuk_ref, wuv_ref,
                out_ref, scf_ref, scn_ref, hbf_ref, hbn_ref, tmp0_ref, tmp1_ref, zb0_ref, zb1_ref, madd_ref, qa2_ref, qi2_ref,
                wb_ref, p0_ref, p1_ref, acc_ref, m_ref, l_ref, al0_ref, al1_ref, *, topk):
    i = pl.program_id(1)
    t0 = i * Q_BLOCK
    far_end = t0 - Q_BLOCK
    n_far = (jnp.maximum(i - 1, 0) * Q_BLOCK + FAR_TILE - 1) // FAR_TILE
    n_ch = FAR_TILE // LANES
    H = A_HEADS
    QB = Q_BLOCK

    for h in range(H):
        q_abs = _dot(qa_ref[:, h * A_HEAD_DIM:(h + 1) * A_HEAD_DIM], wuk_ref[h])
        qa2_ref[h * QB:(h + 1) * QB, :] = (q_abs * (A_HEAD_DIM ** -0.5 * LOG2E)).astype(BF16)
        qi2_ref[h * QB:(h + 1) * QB, :] = qidx_ref[:, h * IDX_DIM:(h + 1) * IDX_DIM]
        wb_ref[h] = jnp.broadcast_to(widx_ref[:, h:h + 1], (QB, LANES))

    lane = lax.broadcasted_iota(jnp.int32, (QB, LANES), 1)
    row = lax.broadcasted_iota(jnp.int32, (QB, LANES), 0)

    tmp = (tmp0_ref, tmp1_ref)
    zbuf = (zb0_ref, zb1_ref)
    pbuf = (p0_ref, p1_ref)
    albuf = (al0_ref, al1_ref)
    n_pairs = (n_far + 1) // 2
    last_start = kidx_ref.shape[0] - FAR_TILE

    def far_rows(kt):
        return pl.ds(pl.multiple_of(jnp.clip(Q_BLOCK + kt * FAR_TILE, Q_BLOCK, last_start), LANES), FAR_TILE)

    def head_sum(z_ref, c):
        acc = jnp.zeros((QB, LANES), F32)
        for h in range(H):
            z = z_ref[h * QB:(h + 1) * QB, c * LANES:(c + 1) * LANES]
            acc = acc + wb_ref[h] * jnp.maximum(z, 0.0)
        return acc

    def store_keys(score, visible, key_dst, top_dst):
        key_dst[...] = jnp.where(visible, _float_key(score), INT_MIN)
        top = pltpu.bitcast(pltpu.bitcast(score, jnp.int32) & jnp.int32(-65536), F32)
        top_dst[...] = jnp.where(visible, top, -jnp.inf).astype(BF16)

    def far_scores(z_ref, kt):
        for c in range(n_ch):
            cols = slice(c * LANES, (c + 1) * LANES)
            s_pos = kt * FAR_TILE + c * LANES + lane
            store_keys(head_sum(z_ref, c), s_pos < far_end, scf_ref.at[kt, :, cols], hbf_ref.at[kt, :, cols])

    def idx_matmul(z_ref, kt):
        z_ref[...] = _dot_nt(qi2_ref[...], kidx_ref[far_rows(kt), :])

    idx_matmul(tmp[0], 0)

    def score_pair(j, carry):
        kt = 2 * j
        idx_matmul(tmp[1], kt + 1)
        far_scores(tmp[0], kt)
        idx_matmul(tmp[0], kt + 2)
        far_scores(tmp[1], kt + 1)
        return carry

    lax.fori_loop(0, n_pairs, score_pair, 0)

    near_start = pl.multiple_of(t0, LANES)
    tmp[0][:, :2 * LANES] = _dot_nt(qi2_ref[...], kidx_ref[pl.ds(near_start, 2 * LANES), :])
    u_lo = jnp.where(i == 0, QB, 0)
    u_hi = jnp.where(row < CHUNK, QB + CHUNK, 2 * QB)
    for c in range(2):
        cols = slice(c * LANES, (c + 1) * LANES)
        u = c * LANES + lane
        vis = jnp.logical_and(u >= u_lo, u < u_hi)
        store_keys(head_sum(tmp[0], c), vis, scn_ref.at[:, cols], hbn_ref.at[:, cols])

    one16, zero16 = jnp.ones((QB, LANES), BF16), jnp.zeros((QB, LANES), BF16)

    def top_bit_body(bi, p):
        bit = jnp.left_shift(jnp.int32(1), 31 - bi)
        p_try = p | bit
        t16 = (p_try ^ jnp.int32(INT_MIN)) >> 16
        pattern = t16 ^ ((t16 >> 15) & jnp.int32(0x7FFF))
        t_top = pltpu.bitcast(jnp.left_shift(pattern, 16), F32).astype(BF16)

        def cnt_body(j, cnt):
            for kt in (2 * j, 2 * j + 1):
                for c in range(n_ch):
                    cnt = cnt + jnp.where(hbf_ref[kt, :, c * LANES:(c + 1) * LANES] >= t_top, one16, zero16)
            return cnt

        cnt = lax.fori_loop(0, n_pairs, cnt_body, zero16)
        for c in range(2):
            cnt = cnt + jnp.where(hbn_ref[:, c * LANES:(c + 1) * LANES] >= t_top, one16, zero16)
        total = jnp.sum(cnt.astype(F32), axis=1, keepdims=True)
        return jnp.where(total >= topk, p_try, p)

    p_top = lax.fori_loop(0, 16, top_bit_body, jnp.zeros((QB, LANES), jnp.int32))

    def bit_body(bi, p):
        bit = jnp.left_shift(jnp.int32(1), 31 - bi)
        p_try = p | bit
        t_try = p_try ^ jnp.int32(INT_MIN)

        def cnt_body(j, cnt):
            for kt in (2 * j, 2 * j + 1):
                for c in range(n_ch):
                    cnt = cnt + jnp.where(scf_ref[kt, :, c * LANES:(c + 1) * LANES] >= t_try, 1, 0)
            return cnt

        cnt = lax.fori_loop(0, n_pairs, cnt_body, jnp.zeros((QB, LANES), jnp.int32))
        for c in range(2):
            cnt = cnt + jnp.where(scn_ref[:, c * LANES:(c + 1) * LANES] >= t_try, 1, 0)
        total = jnp.sum(cnt, axis=1, keepdims=True)
        return jnp.where(total >= topk, p_try, p)

    p_fin = lax.fori_loop(16, 32, bit_body, p_top)
    thr = jnp.maximum(p_fin ^ jnp.int32(INT_MIN), jnp.int32(INT_MIN + 1))

    m_ref[...] = jnp.full(m_ref.shape, NEG_BIG, F32)
    l_ref[...] = jnp.zeros(l_ref.shape, F32)
    acc_ref[...] = jnp.zeros(acc_ref.shape, F32)

    def to_mask(kt, carry):
        for c in range(n_ch):
            cols = slice(c * LANES, (c + 1) * LANES)
            mask = jnp.where(scf_ref[kt, :, cols] >= thr, 0.0, -jnp.inf)
            scf_ref[kt, :, cols] = pltpu.bitcast(mask, jnp.int32)
        return carry

    n_wide = (n_far + ATTN_TILES - 1) // ATTN_TILES
    lax.fori_loop(0, n_wide * ATTN_TILES, to_mask, 0)
    for c in range(2):
        cols = slice(c * LANES, (c + 1) * LANES)
        madd_ref[:, cols] = jnp.where(scn_ref[:, cols] >= thr, 0.0, -jnp.inf)

    GR = HEAD_GROUP * QB
    n_grp = H // HEAD_GROUP

    def rows_at(start, size):
        return pl.ds(start if isinstance(start, int) else pl.multiple_of(start, size), size)

    def group_rows(hg):
        return rows_at(hg * GR, GR)

    def softmax_group(width, near, z_ref, p_ref, al_ref, kt, hg):
        chunks = [slice(c * LANES, (c + 1) * LANES) for c in range(width // LANES)]
        for hh in range(HEAD_GROUP):
            h = hg * HEAD_GROUP + hh
            rows = slice(hh * QB, (hh + 1) * QB)
            stat = rows_at(h * QB, QB)

            def masked_logits(ch):
                if near:
                    v = z_ref[rows, ch] + madd_ref[:, ch] + toep_ref[h, :, ch]
                else:
                    tile, off = divmod(ch.start, FAR_TILE)
                    v = z_ref[rows, ch] + pltpu.bitcast(scf_ref[ATTN_TILES * kt + tile, :, off:off + LANES], F32)
                z_ref[rows, ch] = v
                return v

            mx = masked_logits(chunks[0])
            for ch in chunks[1:]:
                mx = jnp.maximum(mx, masked_logits(ch))
            mx = jnp.max(mx, axis=1, keepdims=True)
            bias = 0.0 if near else bias15_ref[h]
            m_old = m_ref[stat, :]
            m_new = jnp.maximum(m_old, mx + bias)
            shift = m_new - bias
            psum = jnp.zeros((QB, LANES), F32)
            for ch in chunks:
                p = jnp.exp2(z_ref[rows, ch] - shift)
                psum = psum + p
                p_ref[rows, ch] = p.astype(BF16)
            alpha = jnp.exp2(m_old - m_new)
            l_ref[stat, :] = alpha * l_ref[stat, :] + jnp.sum(psum, axis=1, keepdims=True)
            m_ref[stat, :] = m_new
            al_ref[rows, :] = alpha

    def pv_group(p_ref, al_ref, c_tile, width, hg):
        al = al_ref[...]
        rows = group_rows(hg)
        acc_ref[rows, :] = (acc_ref[rows, :] * jnp.concatenate([al] * (A_LATENT // LANES), axis=1)
                            + _dot(p_ref[:, :width], c_tile))

    AW = ATTN_TILES * FAR_TILE

    def wide_rows(kw):
        start = jnp.clip(Q_BLOCK + kw * AW, Q_BLOCK, ckv_ref.shape[0] - AW)
        return pl.ds(pl.multiple_of(start, LANES), AW)

    def far_logits(z_ref, kw, hg):
        z_ref[...] = _dot_nt(qa2_ref[group_rows(hg), :], ckv_ref[wide_rows(kw), :])

    def far_pv(slot, kw, hg):
        pv_group(pbuf[slot], albuf[slot], ckv_ref[wide_rows(kw), :], AW, hg)

    p1_ref[...] = jnp.zeros(p1_ref.shape, BF16)
    al1_ref[...] = jnp.ones(al1_ref.shape, F32)
    far_logits(zbuf[0], 0, 0)

    def attn_pair(j, carry):
        unit = lambda g: (g // n_grp, g % n_grp)
        g = 2 * j
        far_logits(zbuf[1], *unit(g + 1))
        softmax_group(AW, False, zbuf[0], pbuf[0], albuf[0], *unit(g))
        far_pv(1, *unit(g - 1))
        far_logits(zbuf[0], *unit(g + 2))
        softmax_group(AW, False, zbuf[1], pbuf[1], albuf[1], *unit(g + 1))
        far_pv(0, *unit(g))
        return carry

    lax.fori_loop(0, n_wide * (n_grp // 2), attn_pair, 0)
    far_pv(1, n_wide - 1, n_grp - 1)

    c_near = ckv_ref[pl.ds(near_start, 2 * LANES), :]

    for hg in range(n_grp):
        s = hg % 2
        zbuf[s][:, :2 * LANES] = _dot_nt(qa2_ref[group_rows(hg), :], c_near)
        softmax_group(2 * LANES, True, zbuf[s], pbuf[s], albuf[s], 0, hg)
        pv_group(pbuf[s], albuf[s], c_near, 2 * LANES, hg)

    for h in range(H):
        rows = slice(h * QB, (h + 1) * QB)
        inv = 1.0 / l_ref[rows, :]
        o_lat = jnp.concatenate([acc_ref[rows, c * LANES:(c + 1) * LANES] * inv
                                 for c in range(A_LATENT // LANES)], axis=1).astype(BF16)
        out_ref[:, h * A_HEAD_DIM:(h + 1) * A_HEAD_DIM] = _dot(o_lat, wuv_ref[h]).astype(out_ref.dtype)


def _dsa_far_tiles(S):
    return max(1, -(-(S - 2 * Q_BLOCK) // FAR_TILE))


def _dsa_key_rows(S):
    n_wide = -(-_dsa_far_tiles(S) // ATTN_TILES)
    return Q_BLOCK + max(S, n_wide * ATTN_TILES * FAR_TILE)


def _dsa(q_a, qidx, widx, kidx_pad, ckv_pad, toep, bias15, w_uk_t, w_uv, *, B, S):
    nqb = S // Q_BLOCK
    topk = min(TOPK_MAX, S // 4)
    n_far_max = _dsa_far_tiles(S)
    assert (n_far_max + 1) * (FAR_TILE // LANES) + 2 <= 256
    key_rows = _dsa_key_rows(S)
    AW = ATTN_TILES * FAR_TILE
    HQ = A_HEADS * Q_BLOCK
    GR = HEAD_GROUP * Q_BLOCK
    resident = dict(pipeline_mode=pl.Buffered(1))
    return pl.pallas_call(
        functools.partial(_dsa_kernel, topk=topk),
        grid=(B, nqb),
        in_specs=[
            pl.BlockSpec(memory_space=pltpu.SMEM),
            pl.BlockSpec((Q_BLOCK, A_HEADS * A_HEAD_DIM), lambda b, i: (b * nqb + i, 0)),
            pl.BlockSpec((Q_BLOCK, IDX_HEADS * IDX_DIM), lambda b, i: (b * nqb + i, 0)),
            pl.BlockSpec((Q_BLOCK, LANES), lambda b, i: (b * nqb + i, 0)),
            pl.BlockSpec((None, key_rows, IDX_DIM), lambda b, i: (b, 0, 0), **resident),
            pl.BlockSpec((None, key_rows, A_LATENT), lambda b, i: (b, 0, 0), **resident),
            pl.BlockSpec((A_HEADS, Q_BLOCK, 2 * LANES), lambda b, i: (0, 0, 0), **resident),
            pl.BlockSpec((A_HEADS, A_HEAD_DIM, A_LATENT), lambda b, i: (0, 0, 0), **resident),
            pl.BlockSpec((A_HEADS, A_LATENT, A_HEAD_DIM), lambda b, i: (0, 0, 0), **resident),
        ],
        out_specs=pl.BlockSpec((Q_BLOCK, A_HEADS * A_HEAD_DIM), lambda b, i: (b * nqb + i, 0)),
        out_shape=jax.ShapeDtypeStruct((B * S, A_HEADS * A_HEAD_DIM), BF16),
        scratch_shapes=[
            pltpu.VMEM((n_far_max + 1, Q_BLOCK, FAR_TILE), jnp.int32),
            pltpu.VMEM((Q_BLOCK, 2 * LANES), jnp.int32),
            pltpu.VMEM((n_far_max + 1, Q_BLOCK, FAR_TILE), BF16),
            pltpu.VMEM((Q_BLOCK, 2 * LANES), BF16),
            pltpu.VMEM((HQ, FAR_TILE), F32),
            pltpu.VMEM((HQ, FAR_TILE), F32),
            pltpu.VMEM((GR, AW), F32),
            pltpu.VMEM((GR, AW), F32),
            pltpu.VMEM((Q_BLOCK, 2 * LANES), F32),
            pltpu.VMEM((HQ, A_LATENT), BF16),
            pltpu.VMEM((HQ, IDX_DIM), BF16),
            pltpu.VMEM((IDX_HEADS, Q_BLOCK, LANES), F32),
            pltpu.VMEM((GR, AW), BF16),
            pltpu.VMEM((GR, AW), BF16),
            pltpu.VMEM((HQ, A_LATENT), F32),
            pltpu.VMEM((HQ, LANES), F32),
            pltpu.VMEM((HQ, LANES), F32),
            pltpu.VMEM((GR, LANES), F32),
            pltpu.VMEM((GR, LANES), F32),
        ],
        compiler_params=_cparams("parallel", "arbitrary"),
        name="dsa",
    )(bias15, q_a, qidx, widx, kidx_pad, ckv_pad, toep, w_uk_t, w_uv)


def _ret_kernel(gtot_ref, q_ref, k_ref, v_ref, g_ref, d_ref, xi_ref, zeta_ref, gnw_ref, gnb_ref,
                o_ref, state_ref):
    @pl.when(pl.program_id(2) == 0)
    def _():
        state_ref[...] = jnp.zeros(state_ref.shape, F32)

    h = pl.program_id(1)
    q, k, v = q_ref[...], k_ref[...], v_ref[...]
    s = _dot_nt(q, k) * d_ref[0]
    o = _dot(s.astype(BF16), v) + _dot(q, state_ref[...].astype(BF16)) * xi_ref[0]
    kz = (k.astype(F32) * zeta_ref[0]).astype(BF16)
    upd = lax.dot_general(kz, v, (((0,), (0,)), ((), ())), preferred_element_type=F32)
    state_ref[...] = state_ref[...] * gtot_ref[h] + upd

    mu = jnp.mean(o, axis=-1, keepdims=True)
    d = o - mu
    var = jnp.mean(d * d, axis=-1, keepdims=True)
    y = d * lax.rsqrt(var + LN_EPS) * gnw_ref[...] + gnb_ref[...]
    g = g_ref[...]
    o_ref[...] = (g * _sigmoid(g) * y).astype(o_ref.dtype)


def _retention(q_rot, k_rot, v, g_r, gn_w, gn_b, consts, *, B, S):
    d_mat, xi, zeta, gtot = consts
    ng = S // RET_GROUP
    G = RET_GROUP
    return pl.pallas_call(
        _ret_kernel,
        grid=(B, R_HEADS, ng),
        in_specs=[
            pl.BlockSpec(memory_space=pltpu.SMEM),
            pl.BlockSpec((G, R_QK_DIM), lambda b, h, g: (b * ng + g, h)),
            pl.BlockSpec((G, R_QK_DIM), lambda b, h, g: (b * ng + g, h)),
            pl.BlockSpec((G, R_V_DIM), lambda b, h, g: (b * ng + g, h)),
            pl.BlockSpec((G, R_V_DIM), lambda b, h, g: (b * ng + g, h)),
            pl.BlockSpec((1, G, G), lambda b, h, g: (h, 0, 0)),
            pl.BlockSpec((1, G, 1), lambda b, h, g: (h, 0, 0)),
            pl.BlockSpec((1, G, 1), lambda b, h, g: (h, 0, 0)),
            pl.BlockSpec((1, R_V_DIM), lambda b, h, g: (0, h)),
            pl.BlockSpec((1, R_V_DIM), lambda b, h, g: (0, h)),
        ],
        out_specs=pl.BlockSpec((G, R_V_DIM), lambda b, h, g: (b * ng + g, h)),
        out_shape=jax.ShapeDtypeStruct((B * S, R_HEADS * R_V_DIM), BF16),
        scratch_shapes=[pltpu.VMEM((R_QK_DIM, R_V_DIM), F32)],
        compiler_params=_cparams("parallel", "parallel", "arbitrary"),
        name="retention",
    )(gtot, q_rot, k_rot, v, g_r, d_mat, xi, zeta, gn_w, gn_b)


def _retention_consts():
    G = RET_GROUP
    log_g = jnp.log1p(-jnp.exp2(-5.0 - jnp.arange(R_HEADS, dtype=F32)))
    pos = jnp.arange(G, dtype=F32)
    diff = pos[:, None] - pos[None, :]
    ci = jnp.arange(G)[:, None] // CHUNK
    cj = jnp.arange(G)[None, :] // CHUNK
    same = jnp.exp(log_g[:, None, None] * jnp.abs(diff))
    earlier = jnp.exp(log_g[:, None, None] * diff)
    d_mat = jnp.where(ci == cj, same, jnp.where(cj < ci, earlier, 0.0))
    xi = jnp.exp(log_g[:, None] * (pos[None, :] + 1.0))[..., None]
    zeta = jnp.exp(log_g[:, None] * (G - 1.0 - pos[None, :]))[..., None]
    gtot = jnp.exp(log_g * G)
    return d_mat, xi, zeta, gtot


def _rank_rows(v):
    n = v.shape[0]
    ridx = lax.broadcasted_iota(jnp.int32, v.shape, 0)
    rank = jnp.zeros(v.shape, jnp.int32)
    for j in range(n):
        rj = v[j:j + 1, :]
        rank = rank + jnp.where(ridx > j, jnp.where(rj >= v, 1, 0), jnp.where(rj > v, 1, 0))
    return rank


def _router_kernel(x_ref, wr_ref, b_ref, gt_ref, rk_ref):
    st = _dot_nt(wr_ref[...], x_ref[...])
    sig = _sigmoid(st)
    biased = sig + b_ref[...]
    per = N_EXPERTS // N_GROUPS
    blocks = [biased[g * per:(g + 1) * per, :] for g in range(N_GROUPS)]
    gscore = []
    for blk in blocks:
        top2 = jnp.where(_rank_rows(blk) < 2, blk, 0.0)
        gscore.append(jnp.sum(top2, axis=0, keepdims=True))
    masked = []
    for g in range(N_GROUPS):
        grank = jnp.zeros(gscore[g].shape, jnp.int32)
        for g2 in range(N_GROUPS):
            if g2 == g:
                continue
            beats = (gscore[g2] >= gscore[g]) if g2 < g else (gscore[g2] > gscore[g])
            grank = grank + jnp.where(beats, 1, 0)
        keep = jnp.broadcast_to(grank, blocks[g].shape) < TOPK_GROUPS
        masked.append(jnp.where(keep, blocks[g], -jnp.inf))
    cand = jnp.concatenate(masked, axis=0)
    sel = _rank_rows(cand) < TOP_K
    gates = jnp.where(sel, sig, 0.0)
    denom = jnp.sum(gates, axis=0, keepdims=True)
    gt_ref[...] = gates / denom * ROUTED_SCALE
    tm = sel.shape[1]
    upper = lax.broadcasted_iota(jnp.int32, (tm, tm), 0) <= lax.broadcasted_iota(jnp.int32, (tm, tm), 1)
    prefix = _dot(jnp.where(sel, 1.0, 0.0).astype(BF16), jnp.where(upper, 1.0, 0.0).astype(BF16))
    rk_ref[...] = jnp.where(sel, prefix.astype(jnp.int32) - 1, -1)


def _router(xb, wr_t, b_col):
    N = xb.shape[0]
    tm = MOE_WINDOW
    tile = pl.BlockSpec((N_EXPERTS, tm), lambda i: (0, i))
    return pl.pallas_call(
        _router_kernel,
        grid=(N // tm,),
        in_specs=[
            pl.BlockSpec((tm, xb.shape[1]), lambda i: (i, 0)),
            pl.BlockSpec(wr_t.shape, lambda i: (0, 0)),
            pl.BlockSpec(b_col.shape, lambda i: (0, 0)),
        ],
        out_specs=[tile, tile],
        out_shape=[jax.ShapeDtypeStruct((N_EXPERTS, N), F32), jax.ShapeDtypeStruct((N_EXPERTS, N), jnp.int32)],
        compiler_params=_cparams("parallel"),
        name="router",
    )(xb, wr_t, b_col)


def _swiglu(xb, wg, wu, wd):
    hg = _dot(xb, wg)
    hidden = (hg * _sigmoid(hg) * _dot(xb, wu)).astype(BF16)
    return _dot(hidden, wd)


def _moe_kernel(nsub_ref, x_ref, rk_ref, g_ref, wg_ref, wu_ref, wd_ref, sg_ref, su_ref, sd_ref,
                lnw_ref, lnb_ref, of_ref, ob_ref, xb_ref, acc_ref, *, alpha):
    i = pl.program_id(0)
    s = pl.program_id(1)
    n_pairs = pl.num_programs(1) - MOE_SUPER
    W = MOE_WINDOW

    @pl.when(s == 0)
    def _():
        xb_ref[...] = x_ref[...].astype(BF16)
        acc_ref[...] = jnp.zeros(acc_ref.shape, F32)

    @pl.when(s < n_pairs)
    def _():
        row_id = lax.broadcasted_iota(jnp.int32, (MOE_ROWS, W), 0)
        for k in range(MOE_SUPER):
            win = slice(k * W, (k + 1) * W)

            def sub_tile(j, carry, k=k, win=win):
                picks, outs = [], []
                for q in range(2):
                    hit = (row_id + j * MOE_ROWS) == rk_ref[k, q:q + 1, :]
                    onehot = jnp.where(hit, 1.0, 0.0)
                    pick = onehot.astype(BF16)
                    xs = _dot(pick, xb_ref[win, :]).astype(BF16)
                    y = _swiglu(xs, wg_ref[q], wu_ref[q], wd_ref[q])
                    gate = jnp.sum(onehot * g_ref[k, q:q + 1, :], axis=1, keepdims=True)
                    picks.append(pick)
                    outs.append((y * gate).astype(BF16))
                acc_ref[win, :] += lax.dot_general(
                    jnp.concatenate(picks, axis=0), jnp.concatenate(outs, axis=0),
                    (((0,), (0,)), ((), ())), preferred_element_type=F32)
                return carry

            lax.fori_loop(0, nsub_ref[(i * MOE_SUPER + k) * n_pairs + s], sub_tile, 0)

    for k in range(MOE_SUPER):
        @pl.when(s == n_pairs + k)
        def _(k=k):
            win = slice(k * W, (k + 1) * W)
            shared = _swiglu(xb_ref[win, :], sg_ref[...], su_ref[...], sd_ref[...])
            y = _layer_norm_rows(alpha * x_ref[win, :] + acc_ref[win, :] + shared, lnw_ref[...], lnb_ref[...])
            of_ref[...] = y
            ob_ref[...] = y.astype(BF16)


def _moe(x1, gt, rk, p, *, alpha):
    N, D = x1.shape
    W = MOE_WINDOW
    nw, n_pairs = N // W, N_EXPERTS // 2
    rank = rk.reshape(N_EXPERTS, nw, W)
    sel = rank >= 0
    SW = MOE_SUPER
    per_pair = lambda a: jnp.transpose(a.reshape(n_pairs, 2, nw // SW, SW, W), (2, 0, 3, 1, 4))
    n_sub = (jnp.sum(sel, axis=2, dtype=jnp.int32) + MOE_ROWS - 1) // MOE_ROWS
    n_sub = jnp.max(n_sub.T.reshape(nw, n_pairs, 2), axis=2).reshape(-1)
    pair = lambda i, s, n: (jnp.minimum(s, n_pairs - 1), 0, 0)
    resident = dict(pipeline_mode=pl.Buffered(1))
    const2 = lambda i, s, n: (0, 0)
    routed = lambda i, s, n: (i, jnp.minimum(s, n_pairs - 1), 0, 0, 0)
    out_win = lambda i, s, n: (i * SW + jnp.clip(s - n_pairs, 0, SW - 1), 0)
    grid_spec = pltpu.PrefetchScalarGridSpec(
        num_scalar_prefetch=1,
        grid=(nw // SW, n_pairs + SW),
        in_specs=[
            pl.BlockSpec((SW * W, D), lambda i, s, n: (i, 0), **resident),
            pl.BlockSpec((None, None, SW, 2, W), routed),
            pl.BlockSpec((None, None, SW, 2, W), routed),
            pl.BlockSpec((2, D, EXPERT_DIM), pair),
            pl.BlockSpec((2, D, EXPERT_DIM), pair),
            pl.BlockSpec((2, EXPERT_DIM, D), pair),
            pl.BlockSpec((D, EXPERT_DIM), const2, **resident),
            pl.BlockSpec((D, EXPERT_DIM), const2, **resident),
            pl.BlockSpec((EXPERT_DIM, D), const2, **resident),
            pl.BlockSpec((1, D), const2),
            pl.BlockSpec((1, D), const2),
        ],
        out_specs=[pl.BlockSpec((W, D), out_win), pl.BlockSpec((W, D), out_win)],
        scratch_shapes=[pltpu.VMEM((SW * W, D), BF16), pltpu.VMEM((SW * W, D), F32)],
    )
    return pl.pallas_call(
        functools.partial(_moe_kernel, alpha=alpha),
        grid_spec=grid_spec,
        out_shape=[jax.ShapeDtypeStruct((N, D), F32), jax.ShapeDtypeStruct((N, D), BF16)],
        compiler_params=_cparams("parallel", "arbitrary"),
        name="moe",
    )(n_sub, x1, per_pair(rank), per_pair(gt.reshape(N_EXPERTS, nw, W)),
      p['we_gate'].astype(BF16), p['we_up'].astype(BF16), p['we_down'].astype(BF16),
      p['ws_gate'].astype(BF16), p['ws_up'].astype(BF16), p['ws_down'].astype(BF16),
      p['ln2_w'][None, :], p['ln2_b'][None, :])


def _t5_bucket(rel):
    half = REL_BUCKETS // 2
    max_exact = half // 2
    ret = jnp.where(rel > 0, half, 0)
    n = jnp.abs(rel)
    nf = jnp.maximum(n, 1).astype(F32)
    large = max_exact + (jnp.log(nf / max_exact) / math.log(REL_MAX_DIST / max_exact)
                         * (half - max_exact)).astype(jnp.int32)
    large = jnp.minimum(large, half - 1)
    return ret + jnp.where(n < max_exact, n, large)


def _bias_tables(rel_bias):
    i = jnp.arange(Q_BLOCK)[:, None]
    u = jnp.arange(2 * Q_BLOCK)[None, :]
    toep = jnp.transpose(rel_bias[_t5_bucket(u - Q_BLOCK - i)], (2, 0, 1))
    far = rel_bias[_t5_bucket(jnp.asarray(-(Q_BLOCK + 1)))]
    return toep.astype(F32) * LOG2E, far.astype(F32) * LOG2E


def _rope_tables(S):
    half = R_QK_DIM // 2
    inv = ROPE_BASE ** (-jnp.arange(half, dtype=F32) / half)
    ang = jnp.arange(S, dtype=F32)[:, None] * inv[None, :]
    return jnp.cos(ang), jnp.sin(ang)


def _layer(x, xb, tables, p, *, B, S, alpha):
    N, D = x.shape
    cos, sin, toep, bias15, ret_consts = tables
    TM = 512 if N % 512 == 0 else 256
    a_w = A_HEADS * A_HEAD_DIM
    cuts = np.cumsum([0, a_w, A_LATENT, IDX_HEADS * IDX_DIM, IDX_DIM, IDX_HEADS,
                      R_HEADS * R_QK_DIM, R_HEADS * R_QK_DIM, R_HEADS * R_V_DIM, R_HEADS * R_V_DIM, D, D])
    w_in = p['w_in']
    piece = lambda k: w_in[:, int(cuts[k]):int(cuts[k + 1])].astype(BF16)
    row_spec = lambda w: pl.BlockSpec((1, w), lambda i, j: (0, 0))

    TMB = 1024 if (N % 1024 == 0 and S % 1024 == 0) else TM
    big = dict(tm=TMB, tn=1024, epilogue=_epi_plain)
    q_a = _mm(xb, piece(0), out_dtypes=[BF16], name="proj_qa", **big)
    c_kv = _mm(xb, piece(1), tm=TMB, tn=A_LATENT, out_dtypes=[BF16], epilogue=_epi_rms,
               extras=[p['ckv_norm'][None, :]], extra_specs=[row_spec(A_LATENT)], name="proj_ckv")
    q_idx = _mm(xb, piece(2), out_dtypes=[BF16], name="proj_qidx", **big)
    k_idx = _mm(xb, piece(3), tm=TMB, tn=IDX_DIM, out_dtypes=[BF16], epilogue=_epi_rms,
                extras=[p['kidx_norm'][None, :]], extra_specs=[row_spec(IDX_DIM)], name="proj_kidx")
    w_widx = jnp.pad(piece(4), ((0, 0), (0, LANES - IDX_HEADS)))
    w_idx = _mm(xb, w_widx, tm=TMB, tn=LANES, out_dtypes=[F32],
                epilogue=functools.partial(_epi_scale, scale=IDX_HEADS ** -0.5 * IDX_DIM ** -0.5),
                name="proj_widx")
    pos_spec = pl.BlockSpec((TMB, R_QK_DIM // 2), lambda i, j: (i % (S // TMB), 0))
    q_r = _mm(xb, piece(5), tm=TMB, tn=2 * R_QK_DIM, out_dtypes=[BF16],
              epilogue=functools.partial(_epi_rope, scale=1.0),
              extras=[cos, sin], extra_specs=[pos_spec, pos_spec], name="proj_qr")
    k_r = _mm(xb, piece(6), tm=TMB, tn=2 * R_QK_DIM, out_dtypes=[BF16],
              epilogue=functools.partial(_epi_rope, scale=R_QK_DIM ** -0.5),
              extras=[cos, sin], extra_specs=[pos_spec, pos_spec], name="proj_kr")
    v_r = _mm(xb, piece(7), out_dtypes=[BF16], name="proj_vr", **big)
    g_r = _mm(xb, piece(8), out_dtypes=[F32], name="proj_gr", **big)
    g_a = _mm(xb, piece(9), out_dtypes=[F32], name="proj_ga", **big)
    g_b = _mm(xb, piece(10), out_dtypes=[F32], name="proj_gb", **big)

    w_uk_t = jnp.transpose(p['w_uk'], (0, 2, 1)).astype(BF16)
    pad = lambda a: jnp.pad(a.reshape(B, S, -1), ((0, 0), (Q_BLOCK, _dsa_key_rows(S) - Q_BLOCK - S), (0, 0)))
    y_a = _dsa(q_a, q_idx, w_idx, pad(k_idx), pad(c_kv), toep, bias15, w_uk_t, p['w_uv'].astype(BF16),
               B=B, S=S)

    y_r = _retention(q_r, k_r, v_r, g_r, p['gn_w'][None, :], p['gn_b'][None, :], ret_consts, B=B, S=S)

    tile_spec = pl.BlockSpec((TM, 1024), lambda i, j: (i, j))
    m_a = _mm(y_a, p['w_pa'].astype(BF16), tm=TM, tn=1024, out_dtypes=[F32], epilogue=_epi_gate,
              extras=[g_a], extra_specs=[tile_spec], name="proj_a")
    merged = _mm(y_r, p['w_pb'].astype(BF16), tm=TM, tn=1024, out_dtypes=[BF16], epilogue=_epi_gate_add,
                 extras=[g_b, m_a], extra_specs=[tile_spec, tile_spec], name="proj_b")
    full_row = pl.BlockSpec((TM, D), lambda i, j: (i, 0))
    x1, x1b = _mm(merged, p['w_o'].astype(BF16), tm=TM, tn=D, out_dtypes=[F32, BF16],
                  epilogue=functools.partial(_epi_ln, alpha=alpha),
                  extras=[x, p['ln1_w'][None, :], p['ln1_b'][None, :]],
                  extra_specs=[full_row, row_spec(D), row_spec(D)], name="out_ln1")

    gt, rk = _router(x1b, p['w_router'].T.astype(BF16), p['b_router'][:, None])
    return _moe(x1, gt, rk, p, alpha=alpha)


def kernel(x, rel_bias, w_in, ckv_norm, kidx_norm, w_uk, w_uv, gn_w, gn_b, w_pa, w_pb, w_o, ln1_w, ln1_b,
           w_router, b_router, we_gate, we_up, we_down, ws_gate, ws_up, ws_down, ln2_w, ln2_b):
    B, S, D = x.shape
    depth = w_in.shape[0]
    alpha = (2 * depth) ** 0.25
    cos, sin = _rope_tables(S)
    toep, bias15 = _bias_tables(rel_bias)
    tables = (cos, sin, toep, bias15, _retention_consts())
    params = dict(w_in=w_in, ckv_norm=ckv_norm, kidx_norm=kidx_norm, w_uk=w_uk, w_uv=w_uv, gn_w=gn_w,
                  gn_b=gn_b, w_pa=w_pa, w_pb=w_pb, w_o=w_o, ln1_w=ln1_w, ln1_b=ln1_b, w_router=w_router,
                  b_router=b_router, we_gate=we_gate, we_up=we_up, we_down=we_down, ws_gate=ws_gate,
                  ws_up=ws_up, ws_down=ws_down, ln2_w=ln2_w, ln2_b=ln2_b)
    xf = x.reshape(B * S, D)
    xb = xf.astype(BF16)
    for l in range(depth):
        xf, xb = _layer(xf, xb, tables, {k: v[l] for k, v in params.items()}, B=B, S=S, alpha=alpha)
    return xf.reshape(B, S, D)
```

```python
import functools
import math

import jax
import jax.numpy as jnp
import numpy as np
from jax import lax
from jax.experimental import pallas as pl
from jax.experimental.pallas import tpu as pltpu

CHUNK = 64
Q_BLOCK = 128
A_HEADS = 16
A_HEAD_DIM = 128
A_LATENT = 256
IDX_HEADS = 16
IDX_DIM = 128
TOPK_MAX = 256
REL_BUCKETS = 32
REL_MAX_DIST = 128
R_HEADS = 8
R_QK_DIM = 256
R_V_DIM = 512
ROPE_BASE = 10000.0
N_EXPERTS = 64
EXPERT_DIM = 256
TOP_K = 8
N_GROUPS = 8
TOPK_GROUPS = 4
ROUTED_SCALE = 2.5
LN_EPS = 1e-5
RMS_EPS = 1e-6

LANES = 128
FAR_TILE = 512
HEAD_GROUP = 4
ATTN_TILES = 1
RADIX_PLAIN_BITS = 20
RET_GROUP = 512
MOE_WINDOW = 512
MOE_SUPER = 2
MOE_ROWS = 128
VMEM_LIMIT = 56 * 1024 * 1024
INT_MIN = -2 ** 31
NEG_BIG = -1e30
LOG2E = 1.4426950408889634

F32 = jnp.float32
BF16 = jnp.bfloat16


def _cparams(*sem):
    return pltpu.CompilerParams(dimension_semantics=sem, vmem_limit_bytes=VMEM_LIMIT)


def _sigmoid(x):
    return 1.0 / (1.0 + jnp.exp(-x))


def _dot(a, b):
    return jnp.dot(a, b, preferred_element_type=F32)


def _dot_nt(a, b):
    return lax.dot_general(a, b, (((1,), (1,)), ((), ())), preferred_element_type=F32)


def _layer_norm_rows(v, w, b):
    mu = jnp.mean(v, axis=-1, keepdims=True)
    d = v - mu
    var = jnp.mean(d * d, axis=-1, keepdims=True)
    return d * lax.rsqrt(var + LN_EPS) * w + b


def _mm_kernel(*refs, epilogue, n_extra, n_out):
    a_ref, b_ref = refs[0], refs[1]
    extra = refs[2:2 + n_extra]
    outs = refs[2 + n_extra:2 + n_extra + n_out]
    acc = _dot(a_ref[...], b_ref[...])
    res = epilogue(acc, *extra)
    if not isinstance(res, tuple):
        res = (res,)
    for o_ref, r in zip(outs, res):
        o_ref[...] = r.astype(o_ref.dtype)


def _mm(a, b, *, tm, tn, out_dtypes, epilogue, extras=(), extra_specs=(), name):
    M = a.shape[0]
    K, n_cols = b.shape
    grid = (M // tm, n_cols // tn)
    a_spec = pl.BlockSpec((tm, K), lambda i, j: (i, 0))
    b_spec = pl.BlockSpec((K, tn), lambda i, j: (0, j))
    out_shape = [jax.ShapeDtypeStruct((M, n_cols), dt) for dt in out_dtypes]
    out_specs = [pl.BlockSpec((tm, tn), lambda i, j: (i, j)) for _ in out_dtypes]
    res = pl.pallas_call(
        functools.partial(_mm_kernel, epilogue=epilogue, n_extra=len(extras), n_out=len(out_dtypes)),
        grid=grid,
        in_specs=[a_spec, b_spec, *extra_specs],
        out_specs=out_specs,
        out_shape=out_shape,
        compiler_params=_cparams("parallel", "arbitrary"),
        name=name,
    )(a, b, *extras)
    return res[0] if len(res) == 1 else res


def _epi_plain(acc):
    return acc


def _epi_scale(acc, *, scale):
    return acc * scale


def _epi_rms(acc, w_ref):
    return acc * lax.rsqrt(jnp.mean(acc * acc, axis=-1, keepdims=True) + RMS_EPS) * w_ref[...]


def _epi_rope(acc, cos_ref, sin_ref, *, scale):
    half = R_QK_DIM // 2
    c, s = cos_ref[...] * scale, sin_ref[...] * scale
    out = []
    for h in range(acc.shape[-1] // R_QK_DIM):
        x1 = acc[:, h * R_QK_DIM:h * R_QK_DIM + half]
        x2 = acc[:, h * R_QK_DIM + half:(h + 1) * R_QK_DIM]
        out += [x1 * c - x2 * s, x1 * s + x2 * c]
    return jnp.concatenate(out, axis=-1)


def _epi_gate(acc, g_ref):
    return _sigmoid(g_ref[...]) * acc


def _epi_gate_add(acc, g_ref, add_ref):
    return _sigmoid(g_ref[...]) * acc + add_ref[...]


def _epi_ln(acc, res_ref, w_ref, b_ref, *, alpha):
    y = _layer_norm_rows(alpha * res_ref[...] + acc, w_ref[...], b_ref[...])
    return y, y


def _float_key(x):
    bits = pltpu.bitcast(x, jnp.int32)
    return bits ^ ((bits >> 31) & jnp.int32(0x7FFFFFFF))


def _dsa_kernel(bias15_ref, qa_ref, qidx_ref, widx_ref, kidx_ref, ckv_ref, toep_ref, wuk_ref, wuv_ref,
                out_ref, scf_ref, scn_ref, tmp0_ref, tmp1_ref, zb0_ref, zb1_ref, madd_ref, qa2_ref, qi2_ref,
                wb_ref, p0_ref, p1_ref, acc_ref, m_ref, l_ref, al0_ref, al1_ref, *, topk):
    i = pl.program_id(1)
    t0 = i * Q_BLOCK
    far_end = t0 - Q_BLOCK
    n_far = (jnp.maximum(i - 1, 0) * Q_BLOCK + FAR_TILE - 1) // FAR_TILE
    n_ch = FAR_TILE // LANES
    H = A_HEADS
    QB = Q_BLOCK

    for h in range(H):
        q_abs = _dot(qa_ref[:, h * A_HEAD_DIM:(h + 1) * A_HEAD_DIM], wuk_ref[h])
        qa2_ref[h * QB:(h + 1) * QB, :] = (q_abs * (A_HEAD_DIM ** -0.5 * LOG2E)).astype(BF16)
        qi2_ref[h * QB:(h + 1) * QB, :] = qidx_ref[:, h * IDX_DIM:(h + 1) * IDX_DIM]
        wb_ref[h] = jnp.broadcast_to(widx_ref[:, h:h + 1], (QB, LANES))

    lane = lax.broadcasted_iota(jnp.int32, (QB, LANES), 1)
    row = lax.broadcasted_iota(jnp.int32, (QB, LANES), 0)

    tmp = (tmp0_ref, tmp1_ref)
    zbuf = (zb0_ref, zb1_ref)
    pbuf = (p0_ref, p1_ref)
    albuf = (al0_ref, al1_ref)
    n_pairs = (n_far + 1) // 2
    last_start = kidx_ref.shape[0] - FAR_TILE

    def far_rows(kt):
        return pl.ds(pl.multiple_of(jnp.clip(Q_BLOCK + kt * FAR_TILE, Q_BLOCK, last_start), LANES), FAR_TILE)

    def head_sum(z_ref, c):
        acc = jnp.zeros((QB, LANES), F32)
        for h in range(H):
            z = z_ref[h * QB:(h + 1) * QB, c * LANES:(c + 1) * LANES]
            acc = acc + wb_ref[h] * jnp.maximum(z, 0.0)
        return acc

    def far_scores(z_ref, kt):
        for c in range(n_ch):
            key = _float_key(head_sum(z_ref, c))
            s_pos = kt * FAR_TILE + c * LANES + lane
            scf_ref[kt, :, c * LANES:(c + 1) * LANES] = jnp.where(s_pos < far_end, key, INT_MIN)

    def idx_matmul(z_ref, kt):
        z_ref[...] = _dot_nt(qi2_ref[...], kidx_ref[far_rows(kt), :])

    idx_matmul(tmp[0], 0)

    def score_pair(j, carry):
        kt = 2 * j
        idx_matmul(tmp[1], kt + 1)
        far_scores(tmp[0], kt)
        idx_matmul(tmp[0], kt + 2)
        far_scores(tmp[1], kt + 1)
        return carry

    lax.fori_loop(0, n_pairs, score_pair, 0)

    near_start = pl.multiple_of(t0, LANES)
    tmp[0][:, :2 * LANES] = _dot_nt(qi2_ref[...], kidx_ref[pl.ds(near_start, 2 * LANES), :])
    u_lo = jnp.where(i == 0, QB, 0)
    u_hi = jnp.where(row < CHUNK, QB + CHUNK, 2 * QB)
    for c in range(2):
        key = _float_key(head_sum(tmp[0], c))
        u = c * LANES + lane
        vis = jnp.logical_and(u >= u_lo, u < u_hi)
        scn_ref[:, c * LANES:(c + 1) * LANES] = jnp.where(vis, key, INT_MIN)

    def bit_body(bi, carry):
        p, kept = carry
        bit = jnp.left_shift(jnp.int32(1), 31 - bi)
        p_try = p | bit
        t_try = p_try ^ jnp.int32(INT_MIN)

        def cnt_body(j, cnt):
            for kt in (2 * j, 2 * j + 1):
                for c in range(n_ch):
                    cnt = cnt + jnp.where(scf_ref[kt, :, c * LANES:(c + 1) * LANES] >= t_try, 1, 0)
            return cnt

        cnt = lax.fori_loop(0, n_pairs, cnt_body, jnp.zeros((QB, LANES), jnp.int32))
        for c in range(2):
            cnt = cnt + jnp.where(scn_ref[:, c * LANES:(c + 1) * LANES] >= t_try, 1, 0)
        total = jnp.broadcast_to(jnp.sum(cnt, axis=1, keepdims=True), (QB, LANES))
        accept = total >= topk
        return jnp.where(accept, p_try, p), jnp.where(accept, total, kept)

    start = (jnp.zeros((QB, LANES), jnp.int32), jnp.full((QB, LANES), 2 ** 30, jnp.int32))
    carry = lax.fori_loop(0, RADIX_PLAIN_BITS, bit_body, start)

    def pending(kept):
        return jnp.max(jnp.where(kept == topk, 0, 1))

    def tail_cond(state):
        bi, more, _, _ = state
        return jnp.logical_and(bi < 32, more > 0)

    def tail_body(state):
        bi, _, p, kept = state
        p, kept = bit_body(bi + 1, bit_body(bi, (p, kept)))
        return bi + 2, pending(kept), p, kept

    _, _, p_fin, _ = lax.while_loop(tail_cond, tail_body, (jnp.int32(RADIX_PLAIN_BITS), pending(carry[1]), *carry))
    thr = jnp.maximum(p_fin ^ jnp.int32(INT_MIN), jnp.int32(INT_MIN + 1))

    m_ref[...] = jnp.full(m_ref.shape, NEG_BIG, F32)
    l_ref[...] = jnp.zeros(l_ref.shape, F32)
    acc_ref[...] = jnp.zeros(acc_ref.shape, F32)

    def to_mask(kt, carry):
        for c in range(n_ch):
            cols = slice(c * LANES, (c + 1) * LANES)
            mask = jnp.where(scf_ref[kt, :, cols] >= thr, 0.0, -jnp.inf)
            scf_ref[kt, :, cols] = pltpu.bitcast(mask, jnp.int32)
        return carry

    n_wide = (n_far + ATTN_TILES - 1) // ATTN_TILES
    lax.fori_loop(0, n_wide * ATTN_TILES, to_mask, 0)
    for c in range(2):
        cols = slice(c * LANES, (c + 1) * LANES)
        madd_ref[:, cols] = jnp.where(scn_ref[:, cols] >= thr, 0.0, -jnp.inf)

    GR = HEAD_GROUP * QB
    n_grp = H // HEAD_GROUP

    def rows_at(start, size):
        return pl.ds(start if isinstance(start, int) else pl.multiple_of(start, size), size)

    def group_rows(hg):
        return rows_at(hg * GR, GR)

    def softmax_group(width, near, z_ref, p_ref, al_ref, kt, hg):
        chunks = [slice(c * LANES, (c + 1) * LANES) for c in range(width // LANES)]
        for hh in range(HEAD_GROUP):
            h = hg * HEAD_GROUP + hh
            rows = slice(hh * QB, (hh + 1) * QB)
            stat = rows_at(h * QB, QB)

            def masked_logits(ch):
                if near:
                    v = z_ref[rows, ch] + madd_ref[:, ch] + toep_ref[h, :, ch]
                else:
                    tile, off = divmod(ch.start, FAR_TILE)
                    v = z_ref[rows, ch] + pltpu.bitcast(scf_ref[ATTN_TILES * kt + tile, :, off:off + LANES], F32)
                z_ref[rows, ch] = v
                return v

            mx = masked_logits(chunks[0])
            for ch in chunks[1:]:
                mx = jnp.maximum(mx, masked_logits(ch))
            mx = jnp.max(mx, axis=1, keepdims=True)
            bias = 0.0 if near else bias15_ref[h]
            m_old = m_ref[stat, :]
            m_new = jnp.maximum(m_old, mx + bias)
            shift = m_new - bias
            psum = jnp.zeros((QB, LANES), F32)
            for ch in chunks:
                p = jnp.exp2(z_ref[rows, ch] - shift)
                psum = psum + p
                p_ref[rows, ch] = p.astype(BF16)
            alpha = jnp.exp2(m_old - m_new)
            l_ref[stat, :] = alpha * l_ref[stat, :] + jnp.sum(psum, axis=1, keepdims=True)
            m_ref[stat, :] = m_new
            al_ref[rows, :] = alpha

    def pv_group(p_ref, al_ref, c_tile, width, hg):
        al = al_ref[...]
        rows = group_rows(hg)
        acc_ref[rows, :] = (acc_ref[rows, :] * jnp.concatenate([al] * (A_LATENT // LANES), axis=1)
                            + _dot(p_ref[:, :width], c_tile))

    AW = ATTN_TILES * FAR_TILE

    def wide_rows(kw):
        start = jnp.clip(Q_BLOCK + kw * AW, Q_BLOCK, ckv_ref.shape[0] - AW)
        return pl.ds(pl.multiple_of(start, LANES), AW)

    def far_logits(z_ref, kw, hg):
        z_ref[...] = _dot_nt(qa2_ref[group_rows(hg), :], ckv_ref[wide_rows(kw), :])

    def far_pv(slot, kw, hg):
        pv_group(pbuf[slot], albuf[slot], ckv_ref[wide_rows(kw), :], AW, hg)

    p1_ref[...] = jnp.zeros(p1_ref.shape, BF16)
    al1_ref[...] = jnp.ones(al1_ref.shape, F32)
    far_logits(zbuf[0], 0, 0)

    def attn_pair(j, carry):
        unit = lambda g: (g // n_grp, g % n_grp)
        g = 2 * j
        far_logits(zbuf[1], *unit(g + 1))
        softmax_group(AW, False, zbuf[0], pbuf[0], albuf[0], *unit(g))
        far_pv(1, *unit(g - 1))
        far_logits(zbuf[0], *unit(g + 2))
        softmax_group(AW, False, zbuf[1], pbuf[1], albuf[1], *unit(g + 1))
        far_pv(0, *unit(g))
        return carry

    lax.fori_loop(0, n_wide * (n_grp // 2), attn_pair, 0)
    far_pv(1, n_wide - 1, n_grp - 1)

    c_near = ckv_ref[pl.ds(near_start, 2 * LANES), :]

    for hg in range(n_grp):
        s = hg % 2
        zbuf[s][:, :2 * LANES] = _dot_nt(qa2_ref[group_rows(hg), :], c_near)
        softmax_group(2 * LANES, True, zbuf[s], pbuf[s], albuf[s], 0, hg)
        pv_group(pbuf[s], albuf[s], c_near, 2 * LANES, hg)

    for h in range(H):
        rows = slice(h * QB, (h + 1) * QB)
        inv = 1.0 / l_ref[rows, :]
        o_lat = jnp.concatenate([acc_ref[rows, c * LANES:(c + 1) * LANES] * inv
                                 for c in range(A_LATENT // LANES)], axis=1).astype(BF16)
        out_ref[:, h * A_HEAD_DIM:(h + 1) * A_HEAD_DIM] = _dot(o_lat, wuv_ref[h]).astype(out_ref.dtype)


def _dsa_far_tiles(S):
    return max(1, -(-(S - 2 * Q_BLOCK) // FAR_TILE))


def _dsa_key_rows(S):
    n_wide = -(-_dsa_far_tiles(S) // ATTN_TILES)
    return Q_BLOCK + max(S, n_wide * ATTN_TILES * FAR_TILE)


def _dsa(q_a, qidx, widx, kidx_pad, ckv_pad, toep, bias15, w_uk_t, w_uv, *, B, S):
    nqb = S // Q_BLOCK
    topk = min(TOPK_MAX, S // 4)
    n_far_max = _dsa_far_tiles(S)
    key_rows = _dsa_key_rows(S)
    AW = ATTN_TILES * FAR_TILE
    HQ = A_HEADS * Q_BLOCK
    GR = HEAD_GROUP * Q_BLOCK
    resident = dict(pipeline_mode=pl.Buffered(1))
    return pl.pallas_call(
        functools.partial(_dsa_kernel, topk=topk),
        grid=(B, nqb),
        in_specs=[
            pl.BlockSpec(memory_space=pltpu.SMEM),
            pl.BlockSpec((Q_BLOCK, A_HEADS * A_HEAD_DIM), lambda b, i: (b * nqb + i, 0)),
            pl.BlockSpec((Q_BLOCK, IDX_HEADS * IDX_DIM), lambda b, i: (b * nqb + i, 0)),
            pl.BlockSpec((Q_BLOCK, LANES), lambda b, i: (b * nqb + i, 0)),
            pl.BlockSpec((None, key_rows, IDX_DIM), lambda b, i: (b, 0, 0), **resident),
            pl.BlockSpec((None, key_rows, A_LATENT), lambda b, i: (b, 0, 0), **resident),
            pl.BlockSpec((A_HEADS, Q_BLOCK, 2 * LANES), lambda b, i: (0, 0, 0), **resident),
            pl.BlockSpec((A_HEADS, A_HEAD_DIM, A_LATENT), lambda b, i: (0, 0, 0), **resident),
            pl.BlockSpec((A_HEADS, A_LATENT, A_HEAD_DIM), lambda b, i: (0, 0, 0), **resident),
        ],
        out_specs=pl.BlockSpec((Q_BLOCK, A_HEADS * A_HEAD_DIM), lambda b, i: (b * nqb + i, 0)),
        out_shape=jax.ShapeDtypeStruct((B * S, A_HEADS * A_HEAD_DIM), BF16),
        scratch_shapes=[
            pltpu.VMEM((n_far_max + 1, Q_BLOCK, FAR_TILE), jnp.int32),
            pltpu.VMEM((Q_BLOCK, 2 * LANES), jnp.int32),
            pltpu.VMEM((HQ, FAR_TILE), F32),
            pltpu.VMEM((HQ, FAR_TILE), F32),
            pltpu.VMEM((GR, AW), F32),
            pltpu.VMEM((GR, AW), F32),
            pltpu.VMEM((Q_BLOCK, 2 * LANES), F32),
            pltpu.VMEM((HQ, A_LATENT), BF16),
            pltpu.VMEM((HQ, IDX_DIM), BF16),
            pltpu.VMEM((IDX_HEADS, Q_BLOCK, LANES), F32),
            pltpu.VMEM((GR, AW), BF16),
            pltpu.VMEM((GR, AW), BF16),
            pltpu.VMEM((HQ, A_LATENT), F32),
            pltpu.VMEM((HQ, LANES), F32),
            pltpu.VMEM((HQ, LANES), F32),
            pltpu.VMEM((GR, LANES), F32),
            pltpu.VMEM((GR, LANES), F32),
        ],
        compiler_params=_cparams("parallel", "arbitrary"),
        name="dsa",
    )(bias15, q_a, qidx, widx, kidx_pad, ckv_pad, toep, w_uk_t, w_uv)


def _ret_kernel(gtot_ref, q_ref, k_ref, v_ref, g_ref, d_ref, xi_ref, zeta_ref, gnw_ref, gnb_ref,
                o_ref, state_ref):
    @pl.when(pl.program_id(2) == 0)
    def _():
        state_ref[...] = jnp.zeros(state_ref.shape, F32)

    h = pl.program_id(1)
    q, k, v = q_ref[...], k_ref[...], v_ref[...]
    s = _dot_nt(q, k) * d_ref[0]
    o = _dot(s.astype(BF16), v) + _dot(q, state_ref[...].astype(BF16)) * xi_ref[0]
    kz = (k.astype(F32) * zeta_ref[0]).astype(BF16)
    upd = lax.dot_general(kz, v, (((0,), (0,)), ((), ())), preferred_element_type=F32)
    state_ref[...] = state_ref[...] * gtot_ref[h] + upd

    mu = jnp.mean(o, axis=-1, keepdims=True)
    d = o - mu
    var = jnp.mean(d * d, axis=-1, keepdims=True)
    y = d * lax.rsqrt(var + LN_EPS) * gnw_ref[...] + gnb_ref[...]
    g = g_ref[...]
    o_ref[...] = (g * _sigmoid(g) * y).astype(o_ref.dtype)


def _retention(q_rot, k_rot, v, g_r, gn_w, gn_b, consts, *, B, S):
    d_mat, xi, zeta, gtot = consts
    ng = S // RET_GROUP
    G = RET_GROUP
    return pl.pallas_call(
        _ret_kernel,
        grid=(B, R_HEADS, ng),
        in_specs=[
            pl.BlockSpec(memory_space=pltpu.SMEM),
            pl.BlockSpec((G, R_QK_DIM), lambda b, h, g: (b * ng + g, h)),
            pl.BlockSpec((G, R_QK_DIM), lambda b, h, g: (b * ng + g, h)),
            pl.BlockSpec((G, R_V_DIM), lambda b, h, g: (b * ng + g, h)),
            pl.BlockSpec((G, R_V_DIM), lambda b, h, g: (b * ng + g, h)),
            pl.BlockSpec((1, G, G), lambda b, h, g: (h, 0, 0)),
            pl.BlockSpec((1, G, 1), lambda b, h, g: (h, 0, 0)),
            pl.BlockSpec((1, G, 1), lambda b, h, g: (h, 0, 0)),
            pl.BlockSpec((1, R_V_DIM), lambda b, h, g: (0, h)),
            pl.BlockSpec((1, R_V_DIM), lambda b, h, g: (0, h)),
        ],
        out_specs=pl.BlockSpec((G, R_V_DIM), lambda b, h, g: (b * ng + g, h)),
        out_shape=jax.ShapeDtypeStruct((B * S, R_HEADS * R_V_DIM), BF16),
        scratch_shapes=[pltpu.VMEM((R_QK_DIM, R_V_DIM), F32)],
        compiler_params=_cparams("parallel", "parallel", "arbitrary"),
        name="retention",
    )(gtot, q_rot, k_rot, v, g_r, d_mat, xi, zeta, gn_w, gn_b)


def _retention_consts():
    G = RET_GROUP
    log_g = jnp.log1p(-jnp.exp2(-5.0 - jnp.arange(R_HEADS, dtype=F32)))
    pos = jnp.arange(G, dtype=F32)
    diff = pos[:, None] - pos[None, :]
    ci = jnp.arange(G)[:, None] // CHUNK
    cj = jnp.arange(G)[None, :] // CHUNK
    same = jnp.exp(log_g[:, None, None] * jnp.abs(diff))
    earlier = jnp.exp(log_g[:, None, None] * diff)
    d_mat = jnp.where(ci == cj, same, jnp.where(cj < ci, earlier, 0.0))
    xi = jnp.exp(log_g[:, None] * (pos[None, :] + 1.0))[..., None]
    zeta = jnp.exp(log_g[:, None] * (G - 1.0 - pos[None, :]))[..., None]
    gtot = jnp.exp(log_g * G)
    return d_mat, xi, zeta, gtot


def _rank_rows(v):
    n = v.shape[0]
    ridx = lax.broadcasted_iota(jnp.int32, v.shape, 0)
    rank = jnp.zeros(v.shape, jnp.int32)
    for j in range(n):
        rj = v[j:j + 1, :]
        rank = rank + jnp.where(ridx > j, jnp.where(rj >= v, 1, 0), jnp.where(rj > v, 1, 0))
    return rank


def _router_kernel(x_ref, wr_ref, b_ref, gt_ref, rk_ref):
    st = _dot_nt(wr_ref[...], x_ref[...])
    sig = _sigmoid(st)
    biased = sig + b_ref[...]
    per = N_EXPERTS // N_GROUPS
    blocks = [biased[g * per:(g + 1) * per, :] for g in range(N_GROUPS)]
    gscore = []
    for blk in blocks:
        top2 = jnp.where(_rank_rows(blk) < 2, blk, 0.0)
        gscore.append(jnp.sum(top2, axis=0, keepdims=True))
    masked = []
    for g in range(N_GROUPS):
        grank = jnp.zeros(gscore[g].shape, jnp.int32)
        for g2 in range(N_GROUPS):
            if g2 == g:
                continue
            beats = (gscore[g2] >= gscore[g]) if g2 < g else (gscore[g2] > gscore[g])
            grank = grank + jnp.where(beats, 1, 0)
        keep = jnp.broadcast_to(grank, blocks[g].shape) < TOPK_GROUPS
        masked.append(jnp.where(keep, blocks[g], -jnp.inf))
    cand = jnp.concatenate(masked, axis=0)
    sel = _rank_rows(cand) < TOP_K
    gates = jnp.where(sel, sig, 0.0)
    denom = jnp.sum(gates, axis=0, keepdims=True)
    gt_ref[...] = gates / denom * ROUTED_SCALE
    tm = sel.shape[1]
    upper = lax.broadcasted_iota(jnp.int32, (tm, tm), 0) <= lax.broadcasted_iota(jnp.int32, (tm, tm), 1)
    prefix = _dot(jnp.where(sel, 1.0, 0.0).astype(BF16), jnp.where(upper, 1.0, 0.0).astype(BF16))
    rk_ref[...] = jnp.where(sel, prefix.astype(jnp.int32) - 1, -1)


def _router(xb, wr_t, b_col):
    N = xb.shape[0]
    tm = MOE_WINDOW
    tile = pl.BlockSpec((N_EXPERTS, tm), lambda i: (0, i))
    return pl.pallas_call(
        _router_kernel,
        grid=(N // tm,),
        in_specs=[
            pl.BlockSpec((tm, xb.shape[1]), lambda i: (i, 0)),
            pl.BlockSpec(wr_t.shape, lambda i: (0, 0)),
            pl.BlockSpec(b_col.shape, lambda i: (0, 0)),
        ],
        out_specs=[tile, tile],
        out_shape=[jax.ShapeDtypeStruct((N_EXPERTS, N), F32), jax.ShapeDtypeStruct((N_EXPERTS, N), jnp.int32)],
        compiler_params=_cparams("parallel"),
        name="router",
    )(xb, wr_t, b_col)


def _swiglu(xb, wg, wu, wd):
    hg = _dot(xb, wg)
    hidden = (hg * _sigmoid(hg) * _dot(xb, wu)).astype(BF16)
    return _dot(hidden, wd)


def _moe_kernel(nsub_ref, x_ref, rk_ref, g_ref, wg_ref, wu_ref, wd_ref, sg_ref, su_ref, sd_ref,
                lnw_ref, lnb_ref, of_ref, ob_ref, xb_ref, acc_ref, *, alpha):
    i = pl.program_id(0)
    s = pl.program_id(1)
    n_pairs = pl.num_programs(1) - MOE_SUPER
    W = MOE_WINDOW

    @pl.when(s == 0)
    def _():
        xb_ref[...] = x_ref[...].astype(BF16)
        acc_ref[...] = jnp.zeros(acc_ref.shape, F32)

    @pl.when(s < n_pairs)
    def _():
        row_id = lax.broadcasted_iota(jnp.int32, (MOE_ROWS, W), 0)
        for k in range(MOE_SUPER):
            win = slice(k * W, (k + 1) * W)

            def sub_tile(j, carry, k=k, win=win):
                picks, outs = [], []
                for q in range(2):
                    hit = (row_id + j * MOE_ROWS) == rk_ref[k, q:q + 1, :]
                    onehot = jnp.where(hit, 1.0, 0.0)
                    pick = onehot.astype(BF16)
                    xs = _dot(pick, xb_ref[win, :]).astype(BF16)
                    y = _swiglu(xs, wg_ref[q], wu_ref[q], wd_ref[q])
                    gate = jnp.sum(onehot * g_ref[k, q:q + 1, :], axis=1, keepdims=True)
                    picks.append(pick)
                    outs.append((y * gate).astype(BF16))
                acc_ref[win, :] += lax.dot_general(
                    jnp.concatenate(picks, axis=0), jnp.concatenate(outs, axis=0),
                    (((0,), (0,)), ((), ())), preferred_element_type=F32)
                return carry

            lax.fori_loop(0, nsub_ref[(i * MOE_SUPER + k) * n_pairs + s], sub_tile, 0)

    for k in range(MOE_SUPER):
        @pl.when(s == n_pairs + k)
        def _(k=k):
            win = slice(k * W, (k + 1) * W)
            shared = _swiglu(xb_ref[win, :], sg_ref[...], su_ref[...], sd_ref[...])
            y = _layer_norm_rows(alpha * x_ref[win, :] + acc_ref[win, :] + shared, lnw_ref[...], lnb_ref[...])
            of_ref[...] = y
            ob_ref[...] = y.astype(BF16)


def _moe(x1, gt, rk, p, *, alpha):
    N, D = x1.shape
    W = MOE_WINDOW
    nw, n_pairs = N // W, N_EXPERTS // 2
    rank = rk.reshape(N_EXPERTS, nw, W)
    sel = rank >= 0
    SW = MOE_SUPER
    per_pair = lambda a: jnp.transpose(a.reshape(n_pairs, 2, nw // SW, SW, W), (2, 0, 3, 1, 4))
    n_sub = (jnp.sum(sel, axis=2, dtype=jnp.int32) + MOE_ROWS - 1) // MOE_ROWS
    n_sub = jnp.max(n_sub.T.reshape(nw, n_pairs, 2), axis=2).reshape(-1)
    pair = lambda i, s, n: (jnp.minimum(s, n_pairs - 1), 0, 0)
    resident = dict(pipeline_mode=pl.Buffered(1))
    const2 = lambda i, s, n: (0, 0)
    routed = lambda i, s, n: (i, jnp.minimum(s, n_pairs - 1), 0, 0, 0)
    out_win = lambda i, s, n: (i * SW + jnp.clip(s - n_pairs, 0, SW - 1), 0)
    grid_spec = pltpu.PrefetchScalarGridSpec(
        num_scalar_prefetch=1,
        grid=(nw // SW, n_pairs + SW),
        in_specs=[
            pl.BlockSpec((SW * W, D), lambda i, s, n: (i, 0), **resident),
            pl.BlockSpec((None, None, SW, 2, W), routed),
            pl.BlockSpec((None, None, SW, 2, W), routed),
            pl.BlockSpec((2, D, EXPERT_DIM), pair),
            pl.BlockSpec((2, D, EXPERT_DIM), pair),
            pl.BlockSpec((2, EXPERT_DIM, D), pair),
            pl.BlockSpec((D, EXPERT_DIM), const2, **resident),
            pl.BlockSpec((D, EXPERT_DIM), const2, **resident),
            pl.BlockSpec((EXPERT_DIM, D), const2, **resident),
            pl.BlockSpec((1, D), const2),
            pl.BlockSpec((1, D), const2),
        ],
        out_specs=[pl.BlockSpec((W, D), out_win), pl.BlockSpec((W, D), out_win)],
        scratch_shapes=[pltpu.VMEM((SW * W, D), BF16), pltpu.VMEM((SW * W, D), F32)],
    )
    return pl.pallas_call(
        functools.partial(_moe_kernel, alpha=alpha),
        grid_spec=grid_spec,
        out_shape=[jax.ShapeDtypeStruct((N, D), F32), jax.ShapeDtypeStruct((N, D), BF16)],
        compiler_params=_cparams("parallel", "arbitrary"),
        name="moe",
    )(n_sub, x1, per_pair(rank), per_pair(gt.reshape(N_EXPERTS, nw, W)),
      p['we_gate'].astype(BF16), p['we_up'].astype(BF16), p['we_down'].astype(BF16),
      p['ws_gate'].astype(BF16), p['ws_up'].astype(BF16), p['ws_down'].astype(BF16),
      p['ln2_w'][None, :], p['ln2_b'][None, :])


def _t5_bucket(rel):
    half = REL_BUCKETS // 2
    max_exact = half // 2
    ret = jnp.where(rel > 0, half, 0)
    n = jnp.abs(rel)
    nf = jnp.maximum(n, 1).astype(F32)
    large = max_exact + (jnp.log(nf / max_exact) / math.log(REL_MAX_DIST / max_exact)
                         * (half - max_exact)).astype(jnp.int32)
    large = jnp.minimum(large, half - 1)
    return ret + jnp.where(n < max_exact, n, large)


def _bias_tables(rel_bias):
    i = jnp.arange(Q_BLOCK)[:, None]
    u = jnp.arange(2 * Q_BLOCK)[None, :]
    toep = jnp.transpose(rel_bias[_t5_bucket(u - Q_BLOCK - i)], (2, 0, 1))
    far = rel_bias[_t5_bucket(jnp.asarray(-(Q_BLOCK + 1)))]
    return toep.astype(F32) * LOG2E, far.astype(F32) * LOG2E


def _rope_tables(S):
    half = R_QK_DIM // 2
    inv = ROPE_BASE ** (-jnp.arange(half, dtype=F32) / half)
    ang = jnp.arange(S, dtype=F32)[:, None] * inv[None, :]
    return jnp.cos(ang), jnp.sin(ang)


def _layer(x, xb, tables, p, *, B, S, alpha):
    N, D = x.shape
    cos, sin, toep, bias15, ret_consts = tables
    TM = 512 if N % 512 == 0 else 256
    a_w = A_HEADS * A_HEAD_DIM
    cuts = np.cumsum([0, a_w, A_LATENT, IDX_HEADS * IDX_DIM, IDX_DIM, IDX_HEADS,
                      R_HEADS * R_QK_DIM, R_HEADS * R_QK_DIM, R_HEADS * R_V_DIM, R_HEADS * R_V_DIM, D, D])
    w_in = p['w_in']
    piece = lambda k: w_in[:, int(cuts[k]):int(cuts[k + 1])].astype(BF16)
    row_spec = lambda w: pl.BlockSpec((1, w), lambda i, j: (0, 0))

    TMB = 1024 if (N % 1024 == 0 and S % 1024 == 0) else TM
    big = dict(tm=TMB, tn=1024, epilogue=_epi_plain)
    q_a = _mm(xb, piece(0), out_dtypes=[BF16], name="proj_qa", **big)
    c_kv = _mm(xb, piece(1), tm=TMB, tn=A_LATENT, out_dtypes=[BF16], epilogue=_epi_rms,
               extras=[p['ckv_norm'][None, :]], extra_specs=[row_spec(A_LATENT)], name="proj_ckv")
    q_idx = _mm(xb, piece(2), out_dtypes=[BF16], name="proj_qidx", **big)
    k_idx = _mm(xb, piece(3), tm=TMB, tn=IDX_DIM, out_dtypes=[BF16], epilogue=_epi_rms,
                extras=[p['kidx_norm'][None, :]], extra_specs=[row_spec(IDX_DIM)], name="proj_kidx")
    w_widx = jnp.pad(piece(4), ((0, 0), (0, LANES - IDX_HEADS)))
    w_idx = _mm(xb, w_widx, tm=TMB, tn=LANES, out_dtypes=[F32],
                epilogue=functools.partial(_epi_scale, scale=IDX_HEADS ** -0.5 * IDX_DIM ** -0.5),
                name="proj_widx")
    pos_spec = pl.BlockSpec((TMB, R_QK_DIM // 2), lambda i, j: (i % (S // TMB), 0))
    q_r = _mm(xb, piece(5), tm=TMB, tn=2 * R_QK_DIM, out_dtypes=[BF16],
              epilogue=functools.partial(_epi_rope, scale=1.0),
              extras=[cos, sin], extra_specs=[pos_spec, pos_spec], name="proj_qr")
    k_r = _mm(xb, piece(6), tm=TMB, tn=2 * R_QK_DIM, out_dtypes=[BF16],
              epilogue=functools.partial(_epi_rope, scale=R_QK_DIM ** -0.5),
              extras=[cos, sin], extra_specs=[pos_spec, pos_spec], name="proj_kr")
    v_r = _mm(xb, piece(7), out_dtypes=[BF16], name="proj_vr", **big)
    g_r = _mm(xb, piece(8), out_dtypes=[F32], name="proj_gr", **big)
    g_a = _mm(xb, piece(9), out_dtypes=[F32], name="proj_ga", **big)
    g_b = _mm(xb, piece(10), out_dtypes=[F32], name="proj_gb", **big)

    w_uk_t = jnp.transpose(p['w_uk'], (0, 2, 1)).astype(BF16)
    pad = lambda a: jnp.pad(a.reshape(B, S, -1), ((0, 0), (Q_BLOCK, _dsa_key_rows(S) - Q_BLOCK - S), (0, 0)))
    y_a = _dsa(q_a, q_idx, w_idx, pad(k_idx), pad(c_kv), toep, bias15, w_uk_t, p['w_uv'].astype(BF16),
               B=B, S=S)

    y_r = _retention(q_r, k_r, v_r, g_r, p['gn_w'][None, :], p['gn_b'][None, :], ret_consts, B=B, S=S)

    tile_spec = pl.BlockSpec((TM, 1024), lambda i, j: (i, j))
    m_a = _mm(y_a, p['w_pa'].astype(BF16), tm=TM, tn=1024, out_dtypes=[F32], epilogue=_epi_gate,
              extras=[g_a], extra_specs=[tile_spec], name="proj_a")
    merged = _mm(y_r, p['w_pb'].astype(BF16), tm=TM, tn=1024, out_dtypes=[BF16], epilogue=_epi_gate_add,
                 extras=[g_b, m_a], extra_specs=[tile_spec, tile_spec], name="proj_b")
    full_row = pl.BlockSpec((TM, D), lambda i, j: (i, 0))
    x1, x1b = _mm(merged, p['w_o'].astype(BF16), tm=TM, tn=D, out_dtypes=[F32, BF16],
                  epilogue=functools.partial(_epi_ln, alpha=alpha),
                  extras=[x, p['ln1_w'][None, :], p['ln1_b'][None, :]],
                  extra_specs=[full_row, row_spec(D), row_spec(D)], name="out_ln1")

    gt, rk = _router(x1b, p['w_router'].T.astype(BF16), p['b_router'][:, None])
    return _moe(x1, gt, rk, p, alpha=alpha)


def kernel(x, rel_bias, w_in, ckv_norm, kidx_norm, w_uk, w_uv, gn_w, gn_b, w_pa, w_pb, w_o, ln1_w, ln1_b,
           w_router, b_router, we_gate, we_up, we_down, ws_gate, ws_up, ws_down, ln2_w, ln2_b):
    B, S, D = x.shape
    depth = w_in.shape[0]
    alpha = (2 * depth) ** 0.25
    cos, sin = _rope_tables(S)
    toep, bias15 = _bias_tables(rel_bias)
    tables = (cos, sin, toep, bias15, _retention_consts())
    params = dict(w_in=w_in, ckv_norm=ckv_norm, kidx_norm=kidx_norm, w_uk=w_uk, w_uv=w_uv, gn_w=gn_w,
                  gn_b=gn_b, w_pa=w_pa, w_pb=w_pb, w_o=w_o, ln1_w=ln1_w, ln1_b=ln1_b, w_router=w_router,
                  b_router=b_router, we_gate=we_gate, we_up=we_up, we_down=we_down, ws_gate=ws_gate,
                  ws_up=ws_up, ws_down=ws_down, ln2_w=ln2_w, ln2_b=ln2_b)
    xf = x.reshape(B * S, D)
    xb = xf.astype(BF16)
    for l in range(depth):
        xf, xb = _layer(xf, xb, tables, {k: v[l] for k, v in params.items()}, B=B, S=S, alpha=alpha)
    return xf.reshape(B, S, D)
```

```python
import functools
import math

import jax
import jax.numpy as jnp
import numpy as np
from jax import lax
from jax.experimental import pallas as pl
from jax.experimental.pallas import tpu as pltpu

CHUNK = 64
Q_BLOCK = 128
A_HEADS = 16
A_HEAD_DIM = 128
A_LATENT = 256
IDX_HEADS = 16
IDX_DIM = 128
TOPK_MAX = 256
REL_BUCKETS = 32
REL_MAX_DIST = 128
R_HEADS = 8
R_QK_DIM = 256
R_V_DIM = 512
ROPE_BASE = 10000.0
N_EXPERTS = 64
EXPERT_DIM = 256
TOP_K = 8
N_GROUPS = 8
TOPK_GROUPS = 4
ROUTED_SCALE = 2.5
LN_EPS = 1e-5
RMS_EPS = 1e-6

LANES = 128
FAR_TILE = 512
HEAD_GROUP = 4
ATTN_TILES = 1
RADIX_PLAIN_BITS = 20
RET_GROUP = 512
MOE_WINDOW = 512
MOE_SUPER = 2
MOE_ROWS = 128
VMEM_LIMIT = 56 * 1024 * 1024
INT_MIN = -2 ** 31
NEG_BIG = -1e30
LOG2E = 1.4426950408889634

F32 = jnp.float32
BF16 = jnp.bfloat16


def _cparams(*sem):
    return pltpu.CompilerParams(dimension_semantics=sem, vmem_limit_bytes=VMEM_LIMIT)


def _sigmoid(x):
    return 1.0 / (1.0 + jnp.exp(-x))


def _dot(a, b):
    return jnp.dot(a, b, preferred_element_type=F32)


def _dot_nt(a, b):
    return lax.dot_general(a, b, (((1,), (1,)), ((), ())), preferred_element_type=F32)


def _layer_norm_rows(v, w, b):
    mu = jnp.mean(v, axis=-1, keepdims=True)
    d = v - mu
    var = jnp.mean(d * d, axis=-1, keepdims=True)
    return d * lax.rsqrt(var + LN_EPS) * w + b


def _mm_kernel(*refs, epilogue, n_extra, n_out):
    a_ref, b_ref = refs[0], refs[1]
    extra = refs[2:2 + n_extra]
    outs = refs[2 + n_extra:2 + n_extra + n_out]
    acc = _dot(a_ref[...], b_ref[...])
    res = epilogue(acc, *extra)
    if not isinstance(res, tuple):
        res = (res,)
    for o_ref, r in zip(outs, res):
        o_ref[...] = r.astype(o_ref.dtype)


def _mm(a, b, *, tm, tn, out_dtypes, epilogue, extras=(), extra_specs=(), name):
    M = a.shape[0]
    K, n_cols = b.shape
    grid = (M // tm, n_cols // tn)
    a_spec = pl.BlockSpec((tm, K), lambda i, j: (i, 0))
    b_spec = pl.BlockSpec((K, tn), lambda i, j: (0, j))
    out_shape = [jax.ShapeDtypeStruct((M, n_cols), dt) for dt in out_dtypes]
    out_specs = [pl.BlockSpec((tm, tn), lambda i, j: (i, j)) for _ in out_dtypes]
    res = pl.pallas_call(
        functools.partial(_mm_kernel, epilogue=epilogue, n_extra=len(extras), n_out=len(out_dtypes)),
        grid=grid,
        in_specs=[a_spec, b_spec, *extra_specs],
        out_specs=out_specs,
        out_shape=out_shape,
        compiler_params=_cparams("parallel", "arbitrary"),
        name=name,
    )(a, b, *extras)
    return res[0] if len(res) == 1 else res


def _epi_plain(acc):
    return acc


def _epi_scale(acc, *, scale):
    return acc * scale


def _epi_rms(acc, w_ref):
    return acc * lax.rsqrt(jnp.mean(acc * acc, axis=-1, keepdims=True) + RMS_EPS) * w_ref[...]


def _epi_rope(acc, cos_ref, sin_ref, *, scale):
    half = R_QK_DIM // 2
    c, s = cos_ref[...] * scale, sin_ref[...] * scale
    out = []
    for h in range(acc.shape[-1] // R_QK_DIM):
        x1 = acc[:, h * R_QK_DIM:h * R_QK_DIM + half]
        x2 = acc[:, h * R_QK_DIM + half:(h + 1) * R_QK_DIM]
        out += [x1 * c - x2 * s, x1 * s + x2 * c]
    return jnp.concatenate(out, axis=-1)


def _epi_merge(acc_b, gb_ref, ya_ref, wpa_ref, ga_ref):
    return _sigmoid(gb_ref[...]) * acc_b + _sigmoid(ga_ref[...]) * _dot(ya_ref[...], wpa_ref[...])


def _epi_ln(acc, res_ref, w_ref, b_ref, *, alpha):
    y = _layer_norm_rows(alpha * res_ref[...] + acc, w_ref[...], b_ref[...])
    return y, y


def _float_key(x):
    bits = pltpu.bitcast(x, jnp.int32)
    return bits ^ ((bits >> 31) & jnp.int32(0x7FFFFFFF))


def _dsa_kernel(bias15_ref, qa_ref, qidx_ref, widx_ref, kidx_ref, ckv_ref, toep_ref, wuk_ref, wuv_ref,
                out_ref, scf_ref, scn_ref, tmp0_ref, tmp1_ref, zb0_ref, zb1_ref, madd_ref, qa2_ref, qi2_ref,
                wb_ref, p0_ref, p1_ref, acc_ref, m_ref, l_ref, al0_ref, al1_ref, *, topk):
    i = pl.program_id(1)
    t0 = i * Q_BLOCK
    far_end = t0 - Q_BLOCK
    n_far = (jnp.maximum(i - 1, 0) * Q_BLOCK + FAR_TILE - 1) // FAR_TILE
    n_ch = FAR_TILE // LANES
    H = A_HEADS
    QB = Q_BLOCK

    for h in range(H):
        q_abs = _dot(qa_ref[:, h * A_HEAD_DIM:(h + 1) * A_HEAD_DIM], wuk_ref[h])
        qa2_ref[h * QB:(h + 1) * QB, :] = (q_abs * (A_HEAD_DIM ** -0.5 * LOG2E)).astype(BF16)
        qi2_ref[h * QB:(h + 1) * QB, :] = qidx_ref[:, h * IDX_DIM:(h + 1) * IDX_DIM]
        wb_ref[h] = jnp.broadcast_to(widx_ref[:, h:h + 1], (QB, LANES))

    lane = lax.broadcasted_iota(jnp.int32, (QB, LANES), 1)
    row = lax.broadcasted_iota(jnp.int32, (QB, LANES), 0)

    tmp = (tmp0_ref, tmp1_ref)
    zbuf = (zb0_ref, zb1_ref)
    pbuf = (p0_ref, p1_ref)
    albuf = (al0_ref, al1_ref)
    n_pairs = (n_far + 1) // 2
    last_start = kidx_ref.shape[0] - FAR_TILE

    def far_rows(kt):
        return pl.ds(pl.multiple_of(jnp.clip(Q_BLOCK + kt * FAR_TILE, Q_BLOCK, last_start), LANES), FAR_TILE)

    def head_sum(z_ref, c):
        acc = jnp.zeros((QB, LANES), F32)
        for h in range(H):
            z = z_ref[h * QB:(h + 1) * QB, c * LANES:(c + 1) * LANES]
            acc = acc + wb_ref[h] * jnp.maximum(z, 0.0)
        return acc

    def far_scores(z_ref, kt):
        for c in range(n_ch):
            key = _float_key(head_sum(z_ref, c))
            s_pos = kt * FAR_TILE + c * LANES + lane
            scf_ref[kt, :, c * LANES:(c + 1) * LANES] = jnp.where(s_pos < far_end, key, INT_MIN)

    def idx_matmul(z_ref, kt):
        z_ref[...] = _dot_nt(qi2_ref[...], kidx_ref[far_rows(kt), :])

    idx_matmul(tmp[0], 0)

    def score_pair(j, carry):
        kt = 2 * j
        idx_matmul(tmp[1], kt + 1)
        far_scores(tmp[0], kt)
        idx_matmul(tmp[0], kt + 2)
        far_scores(tmp[1], kt + 1)
        return carry

    lax.fori_loop(0, n_pairs, score_pair, 0)

    near_start = pl.multiple_of(t0, LANES)
    tmp[0][:, :2 * LANES] = _dot_nt(qi2_ref[...], kidx_ref[pl.ds(near_start, 2 * LANES), :])
    u_lo = jnp.where(i == 0, QB, 0)
    u_hi = jnp.where(row < CHUNK, QB + CHUNK, 2 * QB)
    for c in range(2):
        key = _float_key(head_sum(tmp[0], c))
        u = c * LANES + lane
        vis = jnp.logical_and(u >= u_lo, u < u_hi)
        scn_ref[:, c * LANES:(c + 1) * LANES] = jnp.where(vis, key, INT_MIN)

    def bit_body(bi, carry):
        p, kept = carry
        bit = jnp.left_shift(jnp.int32(1), 31 - bi)
        p_try = p | bit
        t_try = p_try ^ jnp.int32(INT_MIN)

        def cnt_body(j, cnt):
            for kt in (2 * j, 2 * j + 1):
                for c in range(n_ch):
                    cnt = cnt + jnp.where(scf_ref[kt, :, c * LANES:(c + 1) * LANES] >= t_try, 1, 0)
            return cnt

        cnt = lax.fori_loop(0, n_pairs, cnt_body, jnp.zeros((QB, LANES), jnp.int32))
        for c in range(2):
            cnt = cnt + jnp.where(scn_ref[:, c * LANES:(c + 1) * LANES] >= t_try, 1, 0)
        total = jnp.broadcast_to(jnp.sum(cnt, axis=1, keepdims=True), (QB, LANES))
        accept = total >= topk
        return jnp.where(accept, p_try, p), jnp.where(accept, total, kept)

    start = (jnp.zeros((QB, LANES), jnp.int32), jnp.full((QB, LANES), 2 ** 30, jnp.int32))
    carry = lax.fori_loop(0, RADIX_PLAIN_BITS, bit_body, start)

    def pending(kept):
        return jnp.max(jnp.where(kept == topk, 0, 1))

    def tail_cond(state):
        bi, more, _, _ = state
        return jnp.logical_and(bi < 32, more > 0)

    def tail_body(state):
        bi, _, p, kept = state
        p, kept = bit_body(bi + 1, bit_body(bi, (p, kept)))
        return bi + 2, pending(kept), p, kept

    _, _, p_fin, _ = lax.while_loop(tail_cond, tail_body, (jnp.int32(RADIX_PLAIN_BITS), pending(carry[1]), *carry))
    thr = jnp.maximum(p_fin ^ jnp.int32(INT_MIN), jnp.int32(INT_MIN + 1))

    m_ref[...] = jnp.full(m_ref.shape, NEG_BIG, F32)
    l_ref[...] = jnp.zeros(l_ref.shape, F32)
    acc_ref[...] = jnp.zeros(acc_ref.shape, F32)

    def to_mask(kt, carry):
        for c in range(n_ch):
            cols = slice(c * LANES, (c + 1) * LANES)
            mask = jnp.where(scf_ref[kt, :, cols] >= thr, 0.0, -jnp.inf)
            scf_ref[kt, :, cols] = pltpu.bitcast(mask, jnp.int32)
        return carry

    n_wide = (n_far + ATTN_TILES - 1) // ATTN_TILES
    lax.fori_loop(0, n_wide * ATTN_TILES, to_mask, 0)
    for c in range(2):
        cols = slice(c * LANES, (c + 1) * LANES)
        madd_ref[:, cols] = jnp.where(scn_ref[:, cols] >= thr, 0.0, -jnp.inf)

    GR = HEAD_GROUP * QB
    n_grp = H // HEAD_GROUP

    def rows_at(start, size):
        return pl.ds(start if isinstance(start, int) else pl.multiple_of(start, size), size)

    def group_rows(hg):
        return rows_at(hg * GR, GR)

    def softmax_group(width, near, z_ref, p_ref, al_ref, kt, hg):
        chunks = [slice(c * LANES, (c + 1) * LANES) for c in range(width // LANES)]
        for hh in range(HEAD_GROUP):
            h = hg * HEAD_GROUP + hh
            rows = slice(hh * QB, (hh + 1) * QB)
            stat = rows_at(h * QB, QB)

            def masked_logits(ch):
                if near:
                    v = z_ref[rows, ch] + madd_ref[:, ch] + toep_ref[h, :, ch]
                else:
                    tile, off = divmod(ch.start, FAR_TILE)
                    v = z_ref[rows, ch] + pltpu.bitcast(scf_ref[ATTN_TILES * kt + tile, :, off:off + LANES], F32)
                z_ref[rows, ch] = v
                return v

            mx = masked_logits(chunks[0])
            for ch in chunks[1:]:
                mx = jnp.maximum(mx, masked_logits(ch))
            mx = jnp.max(mx, axis=1, keepdims=True)
            bias = 0.0 if near else bias15_ref[h]
            m_old = m_ref[stat, :]
            m_new = jnp.maximum(m_old, mx + bias)
            shift = m_new - bias
            psum = jnp.zeros((QB, LANES), F32)
            for ch in chunks:
                p = jnp.exp2(z_ref[rows, ch] - shift)
                psum = psum + p
                p_ref[rows, ch] = p.astype(BF16)
            alpha = jnp.exp2(m_old - m_new)
            l_ref[stat, :] = alpha * l_ref[stat, :] + jnp.sum(psum, axis=1, keepdims=True)
            m_ref[stat, :] = m_new
            al_ref[rows, :] = alpha

    def pv_group(p_ref, al_ref, c_tile, width, hg):
        al = al_ref[...]
        rows = group_rows(hg)
        acc_ref[rows, :] = (acc_ref[rows, :] * jnp.concatenate([al] * (A_LATENT // LANES), axis=1)
                            + _dot(p_ref[:, :width], c_tile))

    AW = ATTN_TILES * FAR_TILE

    def wide_rows(kw):
        start = jnp.clip(Q_BLOCK + kw * AW, Q_BLOCK, ckv_ref.shape[0] - AW)
        return pl.ds(pl.multiple_of(start, LANES), AW)

    def far_logits(z_ref, kw, hg):
        z_ref[...] = _dot_nt(qa2_ref[group_rows(hg), :], ckv_ref[wide_rows(kw), :])

    def far_pv(slot, kw, hg):
        pv_group(pbuf[slot], albuf[slot], ckv_ref[wide_rows(kw), :], AW, hg)

    p1_ref[...] = jnp.zeros(p1_ref.shape, BF16)
    al1_ref[...] = jnp.ones(al1_ref.shape, F32)
    far_logits(zbuf[0], 0, 0)

    def attn_pair(j, carry):
        unit = lambda g: (g // n_grp, g % n_grp)
        g = 2 * j
        far_logits(zbuf[1], *unit(g + 1))
        softmax_group(AW, False, zbuf[0], pbuf[0], albuf[0], *unit(g))
        far_pv(1, *unit(g - 1))
        far_logits(zbuf[0], *unit(g + 2))
        softmax_group(AW, False, zbuf[1], pbuf[1], albuf[1], *unit(g + 1))
        far_pv(0, *unit(g))
        return carry

    lax.fori_loop(0, n_wide * (n_grp // 2), attn_pair, 0)
    far_pv(1, n_wide - 1, n_grp - 1)

    c_near = ckv_ref[pl.ds(near_start, 2 * LANES), :]

    for hg in range(n_grp):
        s = hg % 2
        zbuf[s][:, :2 * LANES] = _dot_nt(qa2_ref[group_rows(hg), :], c_near)
        softmax_group(2 * LANES, True, zbuf[s], pbuf[s], albuf[s], 0, hg)
        pv_group(pbuf[s], albuf[s], c_near, 2 * LANES, hg)

    for h in range(H):
        rows = slice(h * QB, (h + 1) * QB)
        inv = 1.0 / l_ref[rows, :]
        o_lat = jnp.concatenate([acc_ref[rows, c * LANES:(c + 1) * LANES] * inv
                                 for c in range(A_LATENT // LANES)], axis=1).astype(BF16)
        out_ref[:, h * A_HEAD_DIM:(h + 1) * A_HEAD_DIM] = _dot(o_lat, wuv_ref[h]).astype(out_ref.dtype)


def _dsa_far_tiles(S):
    return max(1, -(-(S - 2 * Q_BLOCK) // FAR_TILE))


def _dsa_key_rows(S):
    n_wide = -(-_dsa_far_tiles(S) // ATTN_TILES)
    return Q_BLOCK + max(S, n_wide * ATTN_TILES * FAR_TILE)


def _dsa(q_a, qidx, widx, kidx_pad, ckv_pad, toep, bias15, w_uk_t, w_uv, *, B, S):
    nqb = S // Q_BLOCK
    topk = min(TOPK_MAX, S // 4)
    n_far_max = _dsa_far_tiles(S)
    key_rows = _dsa_key_rows(S)
    AW = ATTN_TILES * FAR_TILE
    HQ = A_HEADS * Q_BLOCK
    GR = HEAD_GROUP * Q_BLOCK
    resident = dict(pipeline_mode=pl.Buffered(1))
    return pl.pallas_call(
        functools.partial(_dsa_kernel, topk=topk),
        grid=(B, nqb),
        in_specs=[
            pl.BlockSpec(memory_space=pltpu.SMEM),
            pl.BlockSpec((Q_BLOCK, A_HEADS * A_HEAD_DIM), lambda b, i: (b * nqb + i, 0)),
            pl.BlockSpec((Q_BLOCK, IDX_HEADS * IDX_DIM), lambda b, i: (b * nqb + i, 0)),
            pl.BlockSpec((Q_BLOCK, LANES), lambda b, i: (b * nqb + i, 0)),
            pl.BlockSpec((None, key_rows, IDX_DIM), lambda b, i: (b, 0, 0), **resident),
            pl.BlockSpec((None, key_rows, A_LATENT), lambda b, i: (b, 0, 0), **resident),
            pl.BlockSpec((A_HEADS, Q_BLOCK, 2 * LANES), lambda b, i: (0, 0, 0), **resident),
            pl.BlockSpec((A_HEADS, A_HEAD_DIM, A_LATENT), lambda b, i: (0, 0, 0), **resident),
            pl.BlockSpec((A_HEADS, A_LATENT, A_HEAD_DIM), lambda b, i: (0, 0, 0), **resident),
        ],
        out_specs=pl.BlockSpec((Q_BLOCK, A_HEADS * A_HEAD_DIM), lambda b, i: (b * nqb + i, 0)),
        out_shape=jax.ShapeDtypeStruct((B * S, A_HEADS * A_HEAD_DIM), BF16),
        scratch_shapes=[
            pltpu.VMEM((n_far_max + 1, Q_BLOCK, FAR_TILE), jnp.int32),
            pltpu.VMEM((Q_BLOCK, 2 * LANES), jnp.int32),
            pltpu.VMEM((HQ, FAR_TILE), F32),
            pltpu.VMEM((HQ, FAR_TILE), F32),
            pltpu.VMEM((GR, AW), F32),
            pltpu.VMEM((GR, AW), F32),
            pltpu.VMEM((Q_BLOCK, 2 * LANES), F32),
            pltpu.VMEM((HQ, A_LATENT), BF16),
            pltpu.VMEM((HQ, IDX_DIM), BF16),
            pltpu.VMEM((IDX_HEADS, Q_BLOCK, LANES), F32),
            pltpu.VMEM((GR, AW), BF16),
            pltpu.VMEM((GR, AW), BF16),
            pltpu.VMEM((HQ, A_LATENT), F32),
            pltpu.VMEM((HQ, LANES), F32),
            pltpu.VMEM((HQ, LANES), F32),
            pltpu.VMEM((GR, LANES), F32),
            pltpu.VMEM((GR, LANES), F32),
        ],
        compiler_params=_cparams("parallel", "arbitrary"),
        name="dsa",
    )(bias15, q_a, qidx, widx, kidx_pad, ckv_pad, toep, w_uk_t, w_uv)


def _ret_kernel(gtot_ref, q_ref, k_ref, v_ref, g_ref, d_ref, xi_ref, zeta_ref, gnw_ref, gnb_ref,
                o_ref, state_ref):
    @pl.when(pl.program_id(2) == 0)
    def _():
        state_ref[...] = jnp.zeros(state_ref.shape, F32)

    h = pl.program_id(1)
    q, k, v = q_ref[...], k_ref[...], v_ref[...]
    s = _dot_nt(q, k) * d_ref[0]
    o = _dot(s.astype(BF16), v) + _dot(q, state_ref[...].astype(BF16)) * xi_ref[0]
    kz = (k.astype(F32) * zeta_ref[0]).astype(BF16)
    upd = lax.dot_general(kz, v, (((0,), (0,)), ((), ())), preferred_element_type=F32)
    state_ref[...] = state_ref[...] * gtot_ref[h] + upd

    mu = jnp.mean(o, axis=-1, keepdims=True)
    d = o - mu
    var = jnp.mean(d * d, axis=-1, keepdims=True)
    y = d * lax.rsqrt(var + LN_EPS) * gnw_ref[...] + gnb_ref[...]
    g = g_ref[...]
    o_ref[...] = (g * _sigmoid(g) * y).astype(o_ref.dtype)


def _retention(q_rot, k_rot, v, g_r, gn_w, gn_b, consts, *, B, S):
    d_mat, xi, zeta, gtot = consts
    ng = S // RET_GROUP
    G = RET_GROUP
    return pl.pallas_call(
        _ret_kernel,
        grid=(B, R_HEADS, ng),
        in_specs=[
            pl.BlockSpec(memory_space=pltpu.SMEM),
            pl.BlockSpec((G, R_QK_DIM), lambda b, h, g: (b * ng + g, h)),
            pl.BlockSpec((G, R_QK_DIM), lambda b, h, g: (b * ng + g, h)),
            pl.BlockSpec((G, R_V_DIM), lambda b, h, g: (b * ng + g, h)),
            pl.BlockSpec((G, R_V_DIM), lambda b, h, g: (b * ng + g, h)),
            pl.BlockSpec((1, G, G), lambda b, h, g: (h, 0, 0)),
            pl.BlockSpec((1, G, 1), lambda b, h, g: (h, 0, 0)),
            pl.BlockSpec((1, G, 1), lambda b, h, g: (h, 0, 0)),
            pl.BlockSpec((1, R_V_DIM), lambda b, h, g: (0, h)),
            pl.BlockSpec((1, R_V_DIM), lambda b, h, g: (0, h)),
        ],
        out_specs=pl.BlockSpec((G, R_V_DIM), lambda b, h, g: (b * ng + g, h)),
        out_shape=jax.ShapeDtypeStruct((B * S, R_HEADS * R_V_DIM), BF16),
        scratch_shapes=[pltpu.VMEM((R_QK_DIM, R_V_DIM), F32)],
        compiler_params=_cparams("parallel", "parallel", "arbitrary"),
        name="retention",
    )(gtot, q_rot, k_rot, v, g_r, d_mat, xi, zeta, gn_w, gn_b)


def _retention_consts():
    G = RET_GROUP
    log_g = jnp.log1p(-jnp.exp2(-5.0 - jnp.arange(R_HEADS, dtype=F32)))
    pos = jnp.arange(G, dtype=F32)
    diff = pos[:, None] - pos[None, :]
    ci = jnp.arange(G)[:, None] // CHUNK
    cj = jnp.arange(G)[None, :] // CHUNK
    same = jnp.exp(log_g[:, None, None] * jnp.abs(diff))
    earlier = jnp.exp(log_g[:, None, None] * diff)
    d_mat = jnp.where(ci == cj, same, jnp.where(cj < ci, earlier, 0.0))
    xi = jnp.exp(log_g[:, None] * (pos[None, :] + 1.0))[..., None]
    zeta = jnp.exp(log_g[:, None] * (G - 1.0 - pos[None, :]))[..., None]
    gtot = jnp.exp(log_g * G)
    return d_mat, xi, zeta, gtot


def _rank_rows(v):
    n = v.shape[0]
    ridx = lax.broadcasted_iota(jnp.int32, v.shape, 0)
    rank = jnp.zeros(v.shape, jnp.int32)
    for j in range(n):
        rj = v[j:j + 1, :]
        rank = rank + jnp.where(ridx > j, jnp.where(rj >= v, 1, 0), jnp.where(rj > v, 1, 0))
    return rank


def _router_kernel(x_ref, wr_ref, b_ref, gt_ref, rk_ref):
    st = _dot_nt(wr_ref[...], x_ref[...])
    sig = _sigmoid(st)
    biased = sig + b_ref[...]
    per = N_EXPERTS // N_GROUPS
    blocks = [biased[g * per:(g + 1) * per, :] for g in range(N_GROUPS)]
    gscore = []
    for blk in blocks:
        top2 = jnp.where(_rank_rows(blk) < 2, blk, 0.0)
        gscore.append(jnp.sum(top2, axis=0, keepdims=True))
    masked = []
    for g in range(N_GROUPS):
        grank = jnp.zeros(gscore[g].shape, jnp.int32)
        for g2 in range(N_GROUPS):
            if g2 == g:
                continue
            beats = (gscore[g2] >= gscore[g]) if g2 < g else (gscore[g2] > gscore[g])
            grank = grank + jnp.where(beats, 1, 0)
        keep = jnp.broadcast_to(grank, blocks[g].shape) < TOPK_GROUPS
        masked.append(jnp.where(keep, blocks[g], -jnp.inf))
    cand = jnp.concatenate(masked, axis=0)
    sel = _rank_rows(cand) < TOP_K
    gates = jnp.where(sel, sig, 0.0)
    denom = jnp.sum(gates, axis=0, keepdims=True)
    gt_ref[...] = gates / denom * ROUTED_SCALE
    tm = sel.shape[1]
    upper = lax.broadcasted_iota(jnp.int32, (tm, tm), 0) <= lax.broadcasted_iota(jnp.int32, (tm, tm), 1)
    prefix = _dot(jnp.where(sel, 1.0, 0.0).astype(BF16), jnp.where(upper, 1.0, 0.0).astype(BF16))
    rk_ref[...] = jnp.where(sel, prefix.astype(jnp.int32) - 1, -1)


def _router(xb, wr_t, b_col):
    N = xb.shape[0]
    tm = MOE_WINDOW
    tile = pl.BlockSpec((N_EXPERTS, tm), lambda i: (0, i))
    return pl.pallas_call(
        _router_kernel,
        grid=(N // tm,),
        in_specs=[
            pl.BlockSpec((tm, xb.shape[1]), lambda i: (i, 0)),
            pl.BlockSpec(wr_t.shape, lambda i: (0, 0)),
            pl.BlockSpec(b_col.shape, lambda i: (0, 0)),
        ],
        out_specs=[tile, tile],
        out_shape=[jax.ShapeDtypeStruct((N_EXPERTS, N), F32), jax.ShapeDtypeStruct((N_EXPERTS, N), jnp.int32)],
        compiler_params=_cparams("parallel"),
        name="router",
    )(xb, wr_t, b_col)


def _swiglu(xb, wg, wu, wd):
    hg = _dot(xb, wg)
    hidden = (hg * _sigmoid(hg) * _dot(xb, wu)).astype(BF16)
    return _dot(hidden, wd)


def _moe_kernel(nsub_ref, x_ref, rk_ref, g_ref, wg_ref, wu_ref, wd_ref, sg_ref, su_ref, sd_ref,
                lnw_ref, lnb_ref, of_ref, ob_ref, xb_ref, acc_ref, *, alpha):
    i = pl.program_id(0)
    s = pl.program_id(1)
    n_pairs = pl.num_programs(1) - MOE_SUPER
    W = MOE_WINDOW

    @pl.when(s == 0)
    def _():
        xb_ref[...] = x_ref[...].astype(BF16)
        acc_ref[...] = jnp.zeros(acc_ref.shape, F32)

    @pl.when(s < n_pairs)
    def _():
        row_id = lax.broadcasted_iota(jnp.int32, (MOE_ROWS, W), 0)

        def sub_tile(k, j):
            win = slice(k * W, (k + 1) * W)
            picks, outs = [], []
            for q in range(2):
                hit = (row_id + j * MOE_ROWS) == rk_ref[k, q:q + 1, :]
                onehot = jnp.where(hit, 1.0, 0.0)
                pick = onehot.astype(BF16)
                xs = _dot(pick, xb_ref[win, :]).astype(BF16)
                y = _swiglu(xs, wg_ref[q], wu_ref[q], wd_ref[q])
                gate = jnp.sum(onehot * g_ref[k, q:q + 1, :], axis=1, keepdims=True)
                picks.append(pick)
                outs.append((y * gate).astype(BF16))
            acc_ref[win, :] += lax.dot_general(
                jnp.concatenate(picks, axis=0), jnp.concatenate(outs, axis=0),
                (((0,), (0,)), ((), ())), preferred_element_type=F32)

        for k in range(MOE_SUPER):
            sub_tile(k, 0)
        for k in range(MOE_SUPER):
            lax.fori_loop(1, nsub_ref[(i * MOE_SUPER + k) * n_pairs + s],
                          lambda j, c, k=k: (sub_tile(k, j), c)[1], 0)

    for k in range(MOE_SUPER):
        @pl.when(s == n_pairs + k)
        def _(k=k):
            win = slice(k * W, (k + 1) * W)
            shared = _swiglu(xb_ref[win, :], sg_ref[...], su_ref[...], sd_ref[...])
            y = _layer_norm_rows(alpha * x_ref[win, :] + acc_ref[win, :] + shared, lnw_ref[...], lnb_ref[...])
            of_ref[...] = y
            ob_ref[...] = y.astype(BF16)


def _moe(x1, gt, rk, p, *, alpha):
    N, D = x1.shape
    W = MOE_WINDOW
    nw, n_pairs = N // W, N_EXPERTS // 2
    rank = rk.reshape(N_EXPERTS, nw, W)
    sel = rank >= 0
    SW = MOE_SUPER
    per_pair = lambda a: jnp.transpose(a.reshape(n_pairs, 2, nw // SW, SW, W), (2, 0, 3, 1, 4))
    n_sub = (jnp.sum(sel, axis=2, dtype=jnp.int32) + MOE_ROWS - 1) // MOE_ROWS
    n_sub = jnp.max(n_sub.T.reshape(nw, n_pairs, 2), axis=2).reshape(-1)
    pair = lambda i, s, n: (jnp.minimum(s, n_pairs - 1), 0, 0)
    resident = dict(pipeline_mode=pl.Buffered(1))
    const2 = lambda i, s, n: (0, 0)
    routed = lambda i, s, n: (i, jnp.minimum(s, n_pairs - 1), 0, 0, 0)
    out_win = lambda i, s, n: (i * SW + jnp.clip(s - n_pairs, 0, SW - 1), 0)
    grid_spec = pltpu.PrefetchScalarGridSpec(
        num_scalar_prefetch=1,
        grid=(nw // SW, n_pairs + SW),
        in_specs=[
            pl.BlockSpec((SW * W, D), lambda i, s, n: (i, 0), **resident),
            pl.BlockSpec((None, None, SW, 2, W), routed),
            pl.BlockSpec((None, None, SW, 2, W), routed),
            pl.BlockSpec((2, D, EXPERT_DIM), pair),
            pl.BlockSpec((2, D, EXPERT_DIM), pair),
            pl.BlockSpec((2, EXPERT_DIM, D), pair),
            pl.BlockSpec((D, EXPERT_DIM), const2, **resident),
            pl.BlockSpec((D, EXPERT_DIM), const2, **resident),
            pl.BlockSpec((EXPERT_DIM, D), const2, **resident),
            pl.BlockSpec((1, D), const2),
            pl.BlockSpec((1, D), const2),
        ],
        out_specs=[pl.BlockSpec((W, D), out_win), pl.BlockSpec((W, D), out_win)],
        scratch_shapes=[pltpu.VMEM((SW * W, D), BF16), pltpu.VMEM((SW * W, D), F32)],
    )
    return pl.pallas_call(
        functools.partial(_moe_kernel, alpha=alpha),
        grid_spec=grid_spec,
        out_shape=[jax.ShapeDtypeStruct((N, D), F32), jax.ShapeDtypeStruct((N, D), BF16)],
        compiler_params=_cparams("parallel", "arbitrary"),
        name="moe",
    )(n_sub, x1, per_pair(rank), per_pair(gt.reshape(N_EXPERTS, nw, W)),
      p['we_gate'].astype(BF16), p['we_up'].astype(BF16), p['we_down'].astype(BF16),
      p['ws_gate'].astype(BF16), p['ws_up'].astype(BF16), p['ws_down'].astype(BF16),
      p['ln2_w'][None, :], p['ln2_b'][None, :])


def _t5_bucket(rel):
    half = REL_BUCKETS // 2
    max_exact = half // 2
    ret = jnp.where(rel > 0, half, 0)
    n = jnp.abs(rel)
    nf = jnp.maximum(n, 1).astype(F32)
    large = max_exact + (jnp.log(nf / max_exact) / math.log(REL_MAX_DIST / max_exact)
                         * (half - max_exact)).astype(jnp.int32)
    large = jnp.minimum(large, half - 1)
    return ret + jnp.where(n < max_exact, n, large)


def _bias_tables(rel_bias):
    i = jnp.arange(Q_BLOCK)[:, None]
    u = jnp.arange(2 * Q_BLOCK)[None, :]
    toep = jnp.transpose(rel_bias[_t5_bucket(u - Q_BLOCK - i)], (2, 0, 1))
    far = rel_bias[_t5_bucket(jnp.asarray(-(Q_BLOCK + 1)))]
    return toep.astype(F32) * LOG2E, far.astype(F32) * LOG2E


def _rope_tables(S):
    half = R_QK_DIM // 2
    inv = ROPE_BASE ** (-jnp.arange(half, dtype=F32) / half)
    ang = jnp.arange(S, dtype=F32)[:, None] * inv[None, :]
    return jnp.cos(ang), jnp.sin(ang)


def _layer(x, xb, tables, p, *, B, S, alpha):
    N, D = x.shape
    cos, sin, toep, bias15, ret_consts = tables
    TM = 512 if N % 512 == 0 else 256
    a_w = A_HEADS * A_HEAD_DIM
    cuts = np.cumsum([0, a_w, A_LATENT, IDX_HEADS * IDX_DIM, IDX_DIM, IDX_HEADS,
                      R_HEADS * R_QK_DIM, R_HEADS * R_QK_DIM, R_HEADS * R_V_DIM, R_HEADS * R_V_DIM, D, D])
    w_in = p['w_in']
    piece = lambda k: w_in[:, int(cuts[k]):int(cuts[k + 1])].astype(BF16)
    row_spec = lambda w: pl.BlockSpec((1, w), lambda i, j: (0, 0))

    TMB = 1024 if (N % 1024 == 0 and S % 1024 == 0) else TM
    big = dict(tm=TMB, tn=1024, epilogue=_epi_plain)
    q_a = _mm(xb, piece(0), out_dtypes=[BF16], name="proj_qa", **big)
    c_kv = _mm(xb, piece(1), tm=TMB, tn=A_LATENT, out_dtypes=[BF16], epilogue=_epi_rms,
               extras=[p['ckv_norm'][None, :]], extra_specs=[row_spec(A_LATENT)], name="proj_ckv")
    q_idx = _mm(xb, piece(2), out_dtypes=[BF16], name="proj_qidx", **big)
    k_idx = _mm(xb, piece(3), tm=TMB, tn=IDX_DIM, out_dtypes=[BF16], epilogue=_epi_rms,
                extras=[p['kidx_norm'][None, :]], extra_specs=[row_spec(IDX_DIM)], name="proj_kidx")
    w_widx = jnp.pad(piece(4), ((0, 0), (0, LANES - IDX_HEADS)))
    w_idx = _mm(xb, w_widx, tm=TMB, tn=LANES, out_dtypes=[F32],
                epilogue=functools.partial(_epi_scale, scale=IDX_HEADS ** -0.5 * IDX_DIM ** -0.5),
                name="proj_widx")
    pos_spec = pl.BlockSpec((TMB, R_QK_DIM // 2), lambda i, j: (i % (S // TMB), 0))
    q_r = _mm(xb, piece(5), tm=TMB, tn=2 * R_QK_DIM, out_dtypes=[BF16],
              epilogue=functools.partial(_epi_rope, scale=1.0),
              extras=[cos, sin], extra_specs=[pos_spec, pos_spec], name="proj_qr")
    k_r = _mm(xb, piece(6), tm=TMB, tn=2 * R_QK_DIM, out_dtypes=[BF16],
              epilogue=functools.partial(_epi_rope, scale=R_QK_DIM ** -0.5),
              extras=[cos, sin], extra_specs=[pos_spec, pos_spec], name="proj_kr")
    v_r = _mm(xb, piece(7), out_dtypes=[BF16], name="proj_vr", **big)
    g_r = _mm(xb, piece(8), out_dtypes=[F32], name="proj_gr", **big)
    g_a = _mm(xb, piece(9), out_dtypes=[F32], name="proj_ga", **big)
    g_b = _mm(xb, piece(10), out_dtypes=[F32], name="proj_gb", **big)

    w_uk_t = jnp.transpose(p['w_uk'], (0, 2, 1)).astype(BF16)
    pad = lambda a: jnp.pad(a.reshape(B, S, -1), ((0, 0), (Q_BLOCK, _dsa_key_rows(S) - Q_BLOCK - S), (0, 0)))
    y_a = _dsa(q_a, q_idx, w_idx, pad(k_idx), pad(c_kv), toep, bias15, w_uk_t, p['w_uv'].astype(BF16),
               B=B, S=S)

    y_r = _retention(q_r, k_r, v_r, g_r, p['gn_w'][None, :], p['gn_b'][None, :], ret_consts, B=B, S=S)

    tn_m = 512
    tile_spec = pl.BlockSpec((TM, tn_m), lambda i, j: (i, j))
    merged = _mm(y_r, p['w_pb'].astype(BF16), tm=TM, tn=tn_m, out_dtypes=[BF16], epilogue=_epi_merge,
                 extras=[g_b, y_a, p['w_pa'].astype(BF16), g_a],
                 extra_specs=[tile_spec, pl.BlockSpec((TM, a_w), lambda i, j: (i, 0)),
                              pl.BlockSpec((a_w, tn_m), lambda i, j: (0, j)), tile_spec],
                 name="proj_merge")
    full_row = pl.BlockSpec((TM, D), lambda i, j: (i, 0))
    x1, x1b = _mm(merged, p['w_o'].astype(BF16), tm=TM, tn=D, out_dtypes=[F32, BF16],
                  epilogue=functools.partial(_epi_ln, alpha=alpha),
                  extras=[x, p['ln1_w'][None, :], p['ln1_b'][None, :]],
                  extra_specs=[full_row, row_spec(D), row_spec(D)], name="out_ln1")

    gt, rk = _router(x1b, p['w_router'].T.astype(BF16), p['b_router'][:, None])
    return _moe(x1, gt, rk, p, alpha=alpha)


def kernel(x, rel_bias, w_in, ckv_norm, kidx_norm, w_uk, w_uv, gn_w, gn_b, w_pa, w_pb, w_o, ln1_w, ln1_b,
           w_router, b_router, we_gate, we_up, we_down, ws_gate, ws_up, ws_down, ln2_w, ln2_b):
    B, S, D = x.shape
    depth = w_in.shape[0]
    alpha = (2 * depth) ** 0.25
    cos, sin = _rope_tables(S)
    toep, bias15 = _bias_tables(rel_bias)
    tables = (cos, sin, toep, bias15, _retention_consts())
    params = dict(w_in=w_in, ckv_norm=ckv_norm, kidx_norm=kidx_norm, w_uk=w_uk, w_uv=w_uv, gn_w=gn_w,
                  gn_b=gn_b, w_pa=w_pa, w_pb=w_pb, w_o=w_o, ln1_w=ln1_w, ln1_b=ln1_b, w_router=w_router,
                  b_router=b_router, we_gate=we_gate, we_up=we_up, we_down=we_down, ws_gate=ws_gate,
                  ws_up=ws_up, ws_down=ws_down, ln2_w=ln2_w, ln2_b=ln2_b)
    xf = x.reshape(B * S, D)
    xb = xf.astype(BF16)
    for l in range(depth):
        xf, xb = _layer(xf, xb, tables, {k: v[l] for k, v in params.items()}, B=B, S=S, alpha=alpha)
    return xf.reshape(B, S, D)
```

```python
import functools
import math

import jax
import jax.numpy as jnp
import numpy as np
from jax import lax
from jax.experimental import pallas as pl
from jax.experimental.pallas import tpu as pltpu

CHUNK = 64
Q_BLOCK = 128
A_HEADS = 16
A_HEAD_DIM = 128
A_LATENT = 256
IDX_HEADS = 16
IDX_DIM = 128
TOPK_MAX = 256
REL_BUCKETS = 32
REL_MAX_DIST = 128
R_HEADS = 8
R_QK_DIM = 256
R_V_DIM = 512
ROPE_BASE = 10000.0
N_EXPERTS = 64
EXPERT_DIM = 256
TOP_K = 8
N_GROUPS = 8
TOPK_GROUPS = 4
ROUTED_SCALE = 2.5
LN_EPS = 1e-5
RMS_EPS = 1e-6

LANES = 128
FAR_TILE = 512
HEAD_GROUP = 4
ATTN_TILES = 1
RADIX_PLAIN_BITS = 20
RET_GROUP = 512
MOE_WINDOW = 512
MOE_SUPER = 2
MOE_ROWS = 128
VMEM_LIMIT = 56 * 1024 * 1024
INT_MIN = -2 ** 31
NEG_BIG = -1e30
LOG2E = 1.4426950408889634

F32 = jnp.float32
BF16 = jnp.bfloat16


def _cparams(*sem):
    return pltpu.CompilerParams(dimension_semantics=sem, vmem_limit_bytes=VMEM_LIMIT)


def _sigmoid(x):
    return 1.0 / (1.0 + jnp.exp(-x))


def _dot(a, b):
    return jnp.dot(a, b, preferred_element_type=F32)


def _dot_nt(a, b):
    return lax.dot_general(a, b, (((1,), (1,)), ((), ())), preferred_element_type=F32)


def _layer_norm_rows(v, w, b):
    mu = jnp.mean(v, axis=-1, keepdims=True)
    d = v - mu
    var = jnp.mean(d * d, axis=-1, keepdims=True)
    return d * lax.rsqrt(var + LN_EPS) * w + b


def _mm_kernel(*refs, epilogue, n_extra, n_out):
    a_ref, b_ref = refs[0], refs[1]
    extra = refs[2:2 + n_extra]
    outs = refs[2 + n_extra:2 + n_extra + n_out]
    acc = _dot(a_ref[...], b_ref[...])
    res = epilogue(acc, *extra)
    if not isinstance(res, tuple):
        res = (res,)
    for o_ref, r in zip(outs, res):
        o_ref[...] = r.astype(o_ref.dtype)


def _mm(a, b, *, tm, tn, out_dtypes, epilogue, extras=(), extra_specs=(), name):
    M = a.shape[0]
    K, n_cols = b.shape
    grid = (M // tm, n_cols // tn)
    a_spec = pl.BlockSpec((tm, K), lambda i, j: (i, 0))
    b_spec = pl.BlockSpec((K, tn), lambda i, j: (0, j))
    out_shape = [jax.ShapeDtypeStruct((M, n_cols), dt) for dt in out_dtypes]
    out_specs = [pl.BlockSpec((tm, tn), lambda i, j: (i, j)) for _ in out_dtypes]
    res = pl.pallas_call(
        functools.partial(_mm_kernel, epilogue=epilogue, n_extra=len(extras), n_out=len(out_dtypes)),
        grid=grid,
        in_specs=[a_spec, b_spec, *extra_specs],
        out_specs=out_specs,
        out_shape=out_shape,
        compiler_params=_cparams("parallel", "arbitrary"),
        name=name,
    )(a, b, *extras)
    return res[0] if len(res) == 1 else res


def _epi_plain(acc):
    return acc


def _epi_scale(acc, *, scale):
    return acc * scale


def _epi_rms(acc, w_ref):
    return acc * lax.rsqrt(jnp.mean(acc * acc, axis=-1, keepdims=True) + RMS_EPS) * w_ref[...]


def _epi_rope(acc, cos_ref, sin_ref, *, scale):
    half = R_QK_DIM // 2
    c, s = cos_ref[...] * scale, sin_ref[...] * scale
    out = []
    for h in range(acc.shape[-1] // R_QK_DIM):
        x1 = acc[:, h * R_QK_DIM:h * R_QK_DIM + half]
        x2 = acc[:, h * R_QK_DIM + half:(h + 1) * R_QK_DIM]
        out += [x1 * c - x2 * s, x1 * s + x2 * c]
    return jnp.concatenate(out, axis=-1)


def _epi_merge(acc_b, gb_ref, ya_ref, wpa_ref, ga_ref):
    return _sigmoid(gb_ref[...]) * acc_b + _sigmoid(ga_ref[...]) * _dot(ya_ref[...], wpa_ref[...])


def _epi_ln(acc, res_ref, w_ref, b_ref, *, alpha):
    y = _layer_norm_rows(alpha * res_ref[...] + acc, w_ref[...], b_ref[...])
    return y, y


def _float_key(x):
    bits = pltpu.bitcast(x, jnp.int32)
    return bits ^ ((bits >> 31) & jnp.int32(0x7FFFFFFF))


def _dsa_kernel(bias15_ref, qa_ref, qidx_ref, widx_ref, kidx_ref, ckv_ref, toep_ref, wuk_ref, wuv_ref,
                out_ref, scf_ref, scn_ref, tmp0_ref, tmp1_ref, zb0_ref, zb1_ref, madd_ref, qa2_ref, qi2_ref,
                wb_ref, p0_ref, p1_ref, acc_ref, m_ref, l_ref, al0_ref, al1_ref, *, topk, pos_bits):
    i = pl.program_id(1)
    t0 = i * Q_BLOCK
    far_end = t0 - Q_BLOCK
    n_far = (jnp.maximum(i - 1, 0) * Q_BLOCK + FAR_TILE - 1) // FAR_TILE
    n_ch = FAR_TILE // LANES
    H = A_HEADS
    QB = Q_BLOCK

    for h in range(H):
        q_abs = _dot(qa_ref[:, h * A_HEAD_DIM:(h + 1) * A_HEAD_DIM], wuk_ref[h])
        qa2_ref[h * QB:(h + 1) * QB, :] = (q_abs * (A_HEAD_DIM ** -0.5 * LOG2E)).astype(BF16)
        qi2_ref[h * QB:(h + 1) * QB, :] = qidx_ref[:, h * IDX_DIM:(h + 1) * IDX_DIM]
        wb_ref[h] = jnp.broadcast_to(widx_ref[:, h:h + 1], (QB, LANES))

    lane = lax.broadcasted_iota(jnp.int32, (QB, LANES), 1)
    row = lax.broadcasted_iota(jnp.int32, (QB, LANES), 0)

    tmp = (tmp0_ref, tmp1_ref)
    zbuf = (zb0_ref, zb1_ref)
    pbuf = (p0_ref, p1_ref)
    albuf = (al0_ref, al1_ref)
    n_pairs = (n_far + 1) // 2
    last_start = kidx_ref.shape[0] - FAR_TILE

    def far_rows(kt):
        return pl.ds(pl.multiple_of(jnp.clip(Q_BLOCK + kt * FAR_TILE, Q_BLOCK, last_start), LANES), FAR_TILE)

    def head_sum(z_ref, c):
        acc = jnp.zeros((QB, LANES), F32)
        for h in range(H):
            z = z_ref[h * QB:(h + 1) * QB, c * LANES:(c + 1) * LANES]
            acc = acc + wb_ref[h] * jnp.maximum(z, 0.0)
        return acc

    def far_scores(z_ref, kt):
        for c in range(n_ch):
            key = _float_key(head_sum(z_ref, c))
            s_pos = kt * FAR_TILE + c * LANES + lane
            scf_ref[kt, :, c * LANES:(c + 1) * LANES] = jnp.where(s_pos < far_end, key, INT_MIN)

    def idx_matmul(z_ref, kt):
        z_ref[...] = _dot_nt(qi2_ref[...], kidx_ref[far_rows(kt), :])

    idx_matmul(tmp[0], 0)

    def score_pair(j, carry):
        kt = 2 * j
        idx_matmul(tmp[1], kt + 1)
        far_scores(tmp[0], kt)
        idx_matmul(tmp[0], kt + 2)
        far_scores(tmp[1], kt + 1)
        return carry

    lax.fori_loop(0, n_pairs, score_pair, 0)

    near_start = pl.multiple_of(t0, LANES)
    tmp[0][:, :2 * LANES] = _dot_nt(qi2_ref[...], kidx_ref[pl.ds(near_start, 2 * LANES), :])
    u_lo = jnp.where(i == 0, QB, 0)
    u_hi = jnp.where(row < CHUNK, QB + CHUNK, 2 * QB)
    for c in range(2):
        key = _float_key(head_sum(tmp[0], c))
        u = c * LANES + lane
        vis = jnp.logical_and(u >= u_lo, u < u_hi)
        scn_ref[:, c * LANES:(c + 1) * LANES] = jnp.where(vis, key, INT_MIN)

    def bit_body(bi, carry):
        p, kept = carry
        bit = jnp.left_shift(jnp.int32(1), 31 - bi)
        p_try = p | bit
        t_try = p_try ^ jnp.int32(INT_MIN)

        def cnt_body(j, cnt):
            for kt in (2 * j, 2 * j + 1):
                for c in range(n_ch):
                    cnt = cnt + jnp.where(scf_ref[kt, :, c * LANES:(c + 1) * LANES] >= t_try, 1, 0)
            return cnt

        cnt = lax.fori_loop(0, n_pairs, cnt_body, jnp.zeros((QB, LANES), jnp.int32))
        for c in range(2):
            cnt = cnt + jnp.where(scn_ref[:, c * LANES:(c + 1) * LANES] >= t_try, 1, 0)
        total = jnp.broadcast_to(jnp.sum(cnt, axis=1, keepdims=True), (QB, LANES))
        accept = total >= topk
        return jnp.where(accept, p_try, p), jnp.where(accept, total, kept)

    start = (jnp.zeros((QB, LANES), jnp.int32), jnp.full((QB, LANES), 2 ** 30, jnp.int32))
    carry = lax.fori_loop(0, RADIX_PLAIN_BITS, bit_body, start)

    def pending(kept):
        return jnp.max(jnp.where(kept == topk, 0, 1))

    def tail_cond(state):
        bi, more, _, _ = state
        return jnp.logical_and(bi < 32, more > 0)

    def tail_body(state):
        bi, _, p, kept = state
        p, kept = bit_body(bi + 1, bit_body(bi, (p, kept)))
        return bi + 2, pending(kept), p, kept

    _, _, p_fin, kept_fin = lax.while_loop(
        tail_cond, tail_body, (jnp.int32(RADIX_PLAIN_BITS), pending(carry[1]), *carry))
    t_raw = p_fin ^ jnp.int32(INT_MIN)
    thr = jnp.maximum(t_raw, jnp.int32(INT_MIN + 1))

    m_ref[...] = jnp.full(m_ref.shape, NEG_BIG, F32)
    l_ref[...] = jnp.zeros(l_ref.shape, F32)
    acc_ref[...] = jnp.zeros(acc_ref.shape, F32)

    n_wide = (n_far + ATTN_TILES - 1) // ATTN_TILES
    n_mask = n_wide * ATTN_TILES

    def write_masks(keep):
        def far_tile(kt, carry):
            for c in range(n_ch):
                cols = slice(c * LANES, (c + 1) * LANES)
                pos = kt * FAR_TILE + c * LANES + lane
                mask = jnp.where(keep(scf_ref[kt, :, cols], pos), 0.0, -jnp.inf)
                scf_ref[kt, :, cols] = pltpu.bitcast(mask, jnp.int32)
            return carry

        lax.fori_loop(0, n_mask, far_tile, 0)
        for c in range(2):
            cols = slice(c * LANES, (c + 1) * LANES)
            pos = far_end + c * LANES + lane
            madd_ref[:, cols] = jnp.where(keep(scn_ref[:, cols], pos), 0.0, -jnp.inf)

    tied = jnp.where(kept_fin > topk, jnp.where(kept_fin < 2 ** 30, 1, 0), 0)
    any_tie = jnp.max(tied)

    @pl.when(any_tie == 0)
    def _():
        write_masks(lambda key, pos: key >= thr)

    @pl.when(any_tie > 0)
    def _():
        def count(pred):
            def far_tile(kt, cnt):
                for c in range(n_ch):
                    pos = kt * FAR_TILE + c * LANES + lane
                    cnt = cnt + pred(scf_ref[kt, :, c * LANES:(c + 1) * LANES], pos)
                return cnt

            cnt = lax.fori_loop(0, n_mask, far_tile, jnp.zeros((QB, LANES), jnp.int32))
            for c in range(2):
                cnt = cnt + pred(scn_ref[:, c * LANES:(c + 1) * LANES], far_end + c * LANES + lane)
            return jnp.broadcast_to(jnp.sum(cnt, axis=1, keepdims=True), (QB, LANES))

        need = topk - count(lambda key, pos: jnp.where(key > t_raw, 1, 0))

        def pos_bit(bi, low):
            bit = jnp.left_shift(jnp.int32(1), pos_bits - 1 - bi)
            probe = low + bit - 1
            below = count(lambda key, pos: jnp.where(key == t_raw, jnp.where(pos <= probe, 1, 0), 0))
            return jnp.where(below >= need, low, low + bit)

        cut = lax.fori_loop(0, pos_bits, pos_bit, jnp.zeros((QB, LANES), jnp.int32))
        cut = jnp.where(p_fin == 0, INT_MIN, cut)
        write_masks(lambda key, pos: jnp.where(key == t_raw, jnp.where(pos <= cut, 1, 0),
                                               jnp.where(key > t_raw, 1, 0)) > 0)

    GR = HEAD_GROUP * QB
    n_grp = H // HEAD_GROUP

    def rows_at(start, size):
        return pl.ds(start if isinstance(start, int) else pl.multiple_of(start, size), size)

    def group_rows(hg):
        return rows_at(hg * GR, GR)

    def softmax_group(width, near, z_ref, p_ref, al_ref, kt, hg):
        chunks = [slice(c * LANES, (c + 1) * LANES) for c in range(width // LANES)]
        for hh in range(HEAD_GROUP):
            h = hg * HEAD_GROUP + hh
            rows = slice(hh * QB, (hh + 1) * QB)
            stat = rows_at(h * QB, QB)

            def masked_logits(ch):
                if near:
                    v = z_ref[rows, ch] + madd_ref[:, ch] + toep_ref[h, :, ch]
                else:
                    tile, off = divmod(ch.start, FAR_TILE)
                    v = z_ref[rows, ch] + pltpu.bitcast(scf_ref[ATTN_TILES * kt + tile, :, off:off + LANES], F32)
                z_ref[rows, ch] = v
                return v

            mx = masked_logits(chunks[0])
            for ch in chunks[1:]:
                mx = jnp.maximum(mx, masked_logits(ch))
            mx = jnp.max(mx, axis=1, keepdims=True)
            bias = 0.0 if near else bias15_ref[h]
            m_old = m_ref[stat, :]
            m_new = jnp.maximum(m_old, mx + bias)
            shift = m_new - bias
            psum = jnp.zeros((QB, LANES), F32)
            for ch in chunks:
                p = jnp.exp2(z_ref[rows, ch] - shift)
                psum = psum + p
                p_ref[rows, ch] = p.astype(BF16)
            alpha = jnp.exp2(m_old - m_new)
            l_ref[stat, :] = alpha * l_ref[stat, :] + jnp.sum(psum, axis=1, keepdims=True)
            m_ref[stat, :] = m_new
            al_ref[rows, :] = alpha

    def pv_group(p_ref, al_ref, c_tile, width, hg):
        al = al_ref[...]
        rows = group_rows(hg)
        acc_ref[rows, :] = (acc_ref[rows, :] * jnp.concatenate([al] * (A_LATENT // LANES), axis=1)
                            + _dot(p_ref[:, :width], c_tile))

    AW = ATTN_TILES * FAR_TILE

    def wide_rows(kw):
        start = jnp.clip(Q_BLOCK + kw * AW, Q_BLOCK, ckv_ref.shape[0] - AW)
        return pl.ds(pl.multiple_of(start, LANES), AW)

    def far_logits(z_ref, kw, hg):
        z_ref[...] = _dot_nt(qa2_ref[group_rows(hg), :], ckv_ref[wide_rows(kw), :])

    def far_pv(slot, kw, hg):
        pv_group(pbuf[slot], albuf[slot], ckv_ref[wide_rows(kw), :], AW, hg)

    p1_ref[...] = jnp.zeros(p1_ref.shape, BF16)
    al1_ref[...] = jnp.ones(al1_ref.shape, F32)
    far_logits(zbuf[0], 0, 0)

    def attn_pair(j, carry):
        unit = lambda g: (g // n_grp, g % n_grp)
        g = 2 * j
        far_logits(zbuf[1], *unit(g + 1))
        softmax_group(AW, False, zbuf[0], pbuf[0], albuf[0], *unit(g))
        far_pv(1, *unit(g - 1))
        far_logits(zbuf[0], *unit(g + 2))
        softmax_group(AW, False, zbuf[1], pbuf[1], albuf[1], *unit(g + 1))
        far_pv(0, *unit(g))
        return carry

    lax.fori_loop(0, n_wide * (n_grp // 2), attn_pair, 0)
    far_pv(1, n_wide - 1, n_grp - 1)

    c_near = ckv_ref[pl.ds(near_start, 2 * LANES), :]

    for hg in range(n_grp):
        s = hg % 2
        zbuf[s][:, :2 * LANES] = _dot_nt(qa2_ref[group_rows(hg), :], c_near)
        softmax_group(2 * LANES, True, zbuf[s], pbuf[s], albuf[s], 0, hg)
        pv_group(pbuf[s], albuf[s], c_near, 2 * LANES, hg)

    for h in range(H):
        rows = slice(h * QB, (h + 1) * QB)
        inv = 1.0 / l_ref[rows, :]
        o_lat = jnp.concatenate([acc_ref[rows, c * LANES:(c + 1) * LANES] * inv
                                 for c in range(A_LATENT // LANES)], axis=1).astype(BF16)
        out_ref[:, h * A_HEAD_DIM:(h + 1) * A_HEAD_DIM] = _dot(o_lat, wuv_ref[h]).astype(out_ref.dtype)


def _dsa_far_tiles(S):
    return max(1, -(-(S - 2 * Q_BLOCK) // FAR_TILE))


def _dsa_key_rows(S):
    n_wide = -(-_dsa_far_tiles(S) // ATTN_TILES)
    return Q_BLOCK + max(S, n_wide * ATTN_TILES * FAR_TILE)


def _dsa(q_a, qidx, widx, kidx_pad, ckv_pad, toep, bias15, w_uk_t, w_uv, *, B, S):
    nqb = S // Q_BLOCK
    topk = min(TOPK_MAX, S // 4)
    n_far_max = _dsa_far_tiles(S)
    key_rows = _dsa_key_rows(S)
    AW = ATTN_TILES * FAR_TILE
    HQ = A_HEADS * Q_BLOCK
    GR = HEAD_GROUP * Q_BLOCK
    resident = dict(pipeline_mode=pl.Buffered(1))
    return pl.pallas_call(
        functools.partial(_dsa_kernel, topk=topk, pos_bits=(S - 1).bit_length()),
        grid=(B, nqb),
        in_specs=[
            pl.BlockSpec(memory_space=pltpu.SMEM),
            pl.BlockSpec((Q_BLOCK, A_HEADS * A_HEAD_DIM), lambda b, i: (b * nqb + i, 0)),
            pl.BlockSpec((Q_BLOCK, IDX_HEADS * IDX_DIM), lambda b, i: (b * nqb + i, 0)),
            pl.BlockSpec((Q_BLOCK, LANES), lambda b, i: (b * nqb + i, 0)),
            pl.BlockSpec((None, key_rows, IDX_DIM), lambda b, i: (b, 0, 0), **resident),
            pl.BlockSpec((None, key_rows, A_LATENT), lambda b, i: (b, 0, 0), **resident),
            pl.BlockSpec((A_HEADS, Q_BLOCK, 2 * LANES), lambda b, i: (0, 0, 0), **resident),
            pl.BlockSpec((A_HEADS, A_HEAD_DIM, A_LATENT), lambda b, i: (0, 0, 0), **resident),
            pl.BlockSpec((A_HEADS, A_LATENT, A_HEAD_DIM), lambda b, i: (0, 0, 0), **resident),
        ],
        out_specs=pl.BlockSpec((Q_BLOCK, A_HEADS * A_HEAD_DIM), lambda b, i: (b * nqb + i, 0)),
        out_shape=jax.ShapeDtypeStruct((B * S, A_HEADS * A_HEAD_DIM), BF16),
        scratch_shapes=[
            pltpu.VMEM((n_far_max + 1, Q_BLOCK, FAR_TILE), jnp.int32),
            pltpu.VMEM((Q_BLOCK, 2 * LANES), jnp.int32),
            pltpu.VMEM((HQ, FAR_TILE), F32),
            pltpu.VMEM((HQ, FAR_TILE), F32),
            pltpu.VMEM((GR, AW), F32),
            pltpu.VMEM((GR, AW), F32),
            pltpu.VMEM((Q_BLOCK, 2 * LANES), F32),
            pltpu.VMEM((HQ, A_LATENT), BF16),
            pltpu.VMEM((HQ, IDX_DIM), BF16),
            pltpu.VMEM((IDX_HEADS, Q_BLOCK, LANES), F32),
            pltpu.VMEM((GR, AW), BF16),
            pltpu.VMEM((GR, AW), BF16),
            pltpu.VMEM((HQ, A_LATENT), F32),
            pltpu.VMEM((HQ, LANES), F32),
            pltpu.VMEM((HQ, LANES), F32),
            pltpu.VMEM((GR, LANES), F32),
            pltpu.VMEM((GR, LANES), F32),
        ],
        compiler_params=_cparams("parallel", "arbitrary"),
        name="dsa",
    )(bias15, q_a, qidx, widx, kidx_pad, ckv_pad, toep, w_uk_t, w_uv)


def _ret_kernel(gtot_ref, q_ref, k_ref, v_ref, g_ref, d_ref, xi_ref, zeta_ref, gnw_ref, gnb_ref,
                o_ref, state_ref):
    @pl.when(pl.program_id(2) == 0)
    def _():
        state_ref[...] = jnp.zeros(state_ref.shape, F32)

    h = pl.program_id(1)
    q, k, v = q_ref[...], k_ref[...], v_ref[...]
    s = _dot_nt(q, k) * d_ref[0]
    o = _dot(s.astype(BF16), v) + _dot(q, state_ref[...].astype(BF16)) * xi_ref[0]
    kz = (k.astype(F32) * zeta_ref[0]).astype(BF16)
    upd = lax.dot_general(kz, v, (((0,), (0,)), ((), ())), preferred_element_type=F32)
    state_ref[...] = state_ref[...] * gtot_ref[h] + upd

    mu = jnp.mean(o, axis=-1, keepdims=True)
    d = o - mu
    var = jnp.mean(d * d, axis=-1, keepdims=True)
    y = d * lax.rsqrt(var + LN_EPS) * gnw_ref[...] + gnb_ref[...]
    g = g_ref[...]
    o_ref[...] = (g * _sigmoid(g) * y).astype(o_ref.dtype)


def _retention(q_rot, k_rot, v, g_r, gn_w, gn_b, consts, *, B, S):
    d_mat, xi, zeta, gtot = consts
    ng = S // RET_GROUP
    G = RET_GROUP
    return pl.pallas_call(
        _ret_kernel,
        grid=(B, R_HEADS, ng),
        in_specs=[
            pl.BlockSpec(memory_space=pltpu.SMEM),
            pl.BlockSpec((G, R_QK_DIM), lambda b, h, g: (b * ng + g, h)),
            pl.BlockSpec((G, R_QK_DIM), lambda b, h, g: (b * ng + g, h)),
            pl.BlockSpec((G, R_V_DIM), lambda b, h, g: (b * ng + g, h)),
            pl.BlockSpec((G, R_V_DIM), lambda b, h, g: (b * ng + g, h)),
            pl.BlockSpec((1, G, G), lambda b, h, g: (h, 0, 0)),
            pl.BlockSpec((1, G, 1), lambda b, h, g: (h, 0, 0)),
            pl.BlockSpec((1, G, 1), lambda b, h, g: (h, 0, 0)),
            pl.BlockSpec((1, R_V_DIM), lambda b, h, g: (0, h)),
            pl.BlockSpec((1, R_V_DIM), lambda b, h, g: (0, h)),
        ],
        out_specs=pl.BlockSpec((G, R_V_DIM), lambda b, h, g: (b * ng + g, h)),
        out_shape=jax.ShapeDtypeStruct((B * S, R_HEADS * R_V_DIM), BF16),
        scratch_shapes=[pltpu.VMEM((R_QK_DIM, R_V_DIM), F32)],
        compiler_params=_cparams("parallel", "parallel", "arbitrary"),
        name="retention",
    )(gtot, q_rot, k_rot, v, g_r, d_mat, xi, zeta, gn_w, gn_b)


def _retention_consts():
    G = RET_GROUP
    log_g = jnp.log1p(-jnp.exp2(-5.0 - jnp.arange(R_HEADS, dtype=F32)))
    pos = jnp.arange(G, dtype=F32)
    diff = pos[:, None] - pos[None, :]
    ci = jnp.arange(G)[:, None] // CHUNK
    cj = jnp.arange(G)[None, :] // CHUNK
    same = jnp.exp(log_g[:, None, None] * jnp.abs(diff))
    earlier = jnp.exp(log_g[:, None, None] * diff)
    d_mat = jnp.where(ci == cj, same, jnp.where(cj < ci, earlier, 0.0))
    xi = jnp.exp(log_g[:, None] * (pos[None, :] + 1.0))[..., None]
    zeta = jnp.exp(log_g[:, None] * (G - 1.0 - pos[None, :]))[..., None]
    gtot = jnp.exp(log_g * G)
    return d_mat, xi, zeta, gtot


def _rank_rows(v):
    n = v.shape[0]
    ridx = lax.broadcasted_iota(jnp.int32, v.shape, 0)
    rank = jnp.zeros(v.shape, jnp.int32)
    for j in range(n):
        rj = v[j:j + 1, :]
        rank = rank + jnp.where(ridx > j, jnp.where(rj >= v, 1, 0), jnp.where(rj > v, 1, 0))
    return rank


def _router_kernel(x_ref, wr_ref, b_ref, gt_ref, rk_ref):
    st = _dot_nt(wr_ref[...], x_ref[...])
    sig = _sigmoid(st)
    biased = sig + b_ref[...]
    per = N_EXPERTS // N_GROUPS
    blocks = [biased[g * per:(g + 1) * per, :] for g in range(N_GROUPS)]
    gscore = []
    for blk in blocks:
        top2 = jnp.where(_rank_rows(blk) < 2, blk, 0.0)
        gscore.append(jnp.sum(top2, axis=0, keepdims=True))
    masked = []
    for g in range(N_GROUPS):
        grank = jnp.zeros(gscore[g].shape, jnp.int32)
        for g2 in range(N_GROUPS):
            if g2 == g:
                continue
            beats = (gscore[g2] >= gscore[g]) if g2 < g else (gscore[g2] > gscore[g])
            grank = grank + jnp.where(beats, 1, 0)
        keep = jnp.broadcast_to(grank, blocks[g].shape) < TOPK_GROUPS
        masked.append(jnp.where(keep, blocks[g], -jnp.inf))
    cand = jnp.concatenate(masked, axis=0)
    sel = _rank_rows(cand) < TOP_K
    gates = jnp.where(sel, sig, 0.0)
    denom = jnp.sum(gates, axis=0, keepdims=True)
    gt_ref[...] = gates / denom * ROUTED_SCALE
    tm = sel.shape[1]
    upper = lax.broadcasted_iota(jnp.int32, (tm, tm), 0) <= lax.broadcasted_iota(jnp.int32, (tm, tm), 1)
    prefix = _dot(jnp.where(sel, 1.0, 0.0).astype(BF16), jnp.where(upper, 1.0, 0.0).astype(BF16))
    rk_ref[...] = jnp.where(sel, prefix.astype(jnp.int32) - 1, -1)


def _router(xb, wr_t, b_col):
    N = xb.shape[0]
    tm = MOE_WINDOW
    tile = pl.BlockSpec((N_EXPERTS, tm), lambda i: (0, i))
    return pl.pallas_call(
        _router_kernel,
        grid=(N // tm,),
        in_specs=[
            pl.BlockSpec((tm, xb.shape[1]), lambda i: (i, 0)),
            pl.BlockSpec(wr_t.shape, lambda i: (0, 0)),
            pl.BlockSpec(b_col.shape, lambda i: (0, 0)),
        ],
        out_specs=[tile, tile],
        out_shape=[jax.ShapeDtypeStruct((N_EXPERTS, N), F32), jax.ShapeDtypeStruct((N_EXPERTS, N), jnp.int32)],
        compiler_params=_cparams("parallel"),
        name="router",
    )(xb, wr_t, b_col)


def _swiglu(xb, wg, wu, wd):
    hg = _dot(xb, wg)
    hidden = (hg * _sigmoid(hg) * _dot(xb, wu)).astype(BF16)
    return _dot(hidden, wd)


def _moe_kernel(nsub_ref, x_ref, rk_ref, g_ref, wg_ref, wu_ref, wd_ref, sg_ref, su_ref, sd_ref,
                lnw_ref, lnb_ref, of_ref, ob_ref, xb_ref, acc_ref, *, alpha):
    i = pl.program_id(0)
    s = pl.program_id(1)
    n_pairs = pl.num_programs(1) - MOE_SUPER
    W = MOE_WINDOW

    @pl.when(s == 0)
    def _():
        xb_ref[...] = x_ref[...].astype(BF16)
        acc_ref[...] = jnp.zeros(acc_ref.shape, F32)

    @pl.when(s < n_pairs)
    def _():
        row_id = lax.broadcasted_iota(jnp.int32, (MOE_ROWS, W), 0)

        def sub_tile(k, j):
            win = slice(k * W, (k + 1) * W)
            picks, outs = [], []
            for q in range(2):
                hit = (row_id + j * MOE_ROWS) == rk_ref[k, q:q + 1, :]
                onehot = jnp.where(hit, 1.0, 0.0)
                pick = onehot.astype(BF16)
                xs = _dot(pick, xb_ref[win, :]).astype(BF16)
                y = _swiglu(xs, wg_ref[q], wu_ref[q], wd_ref[q])
                gate = jnp.sum(onehot * g_ref[k, q:q + 1, :], axis=1, keepdims=True)
                picks.append(pick)
                outs.append((y * gate).astype(BF16))
            acc_ref[win, :] += lax.dot_general(
                jnp.concatenate(picks, axis=0), jnp.concatenate(outs, axis=0),
                (((0,), (0,)), ((), ())), preferred_element_type=F32)

        for k in range(MOE_SUPER):
            sub_tile(k, 0)
        for k in range(MOE_SUPER):
            lax.fori_loop(1, nsub_ref[(i * MOE_SUPER + k) * n_pairs + s],
                          lambda j, c, k=k: (sub_tile(k, j), c)[1], 0)

    for k in range(MOE_SUPER):
        @pl.when(s == n_pairs + k)
        def _(k=k):
            win = slice(k * W, (k + 1) * W)
            shared = _swiglu(xb_ref[win, :], sg_ref[...], su_ref[...], sd_ref[...])
            y = _layer_norm_rows(alpha * x_ref[win, :] + acc_ref[win, :] + shared, lnw_ref[...], lnb_ref[...])
            of_ref[...] = y
            ob_ref[...] = y.astype(BF16)


def _moe(x1, gt, rk, p, *, alpha):
    N, D = x1.shape
    W = MOE_WINDOW
    nw, n_pairs = N // W, N_EXPERTS // 2
    rank = rk.reshape(N_EXPERTS, nw, W)
    sel = rank >= 0
    SW = MOE_SUPER
    per_pair = lambda a: jnp.transpose(a.reshape(n_pairs, 2, nw // SW, SW, W), (2, 0, 3, 1, 4))
    n_sub = (jnp.sum(sel, axis=2, dtype=jnp.int32) + MOE_ROWS - 1) // MOE_ROWS
    n_sub = jnp.max(n_sub.T.reshape(nw, n_pairs, 2), axis=2).reshape(-1)
    pair = lambda i, s, n: (jnp.minimum(s, n_pairs - 1), 0, 0)
    resident = dict(pipeline_mode=pl.Buffered(1))
    const2 = lambda i, s, n: (0, 0)
    routed = lambda i, s, n: (i, jnp.minimum(s, n_pairs - 1), 0, 0, 0)
    out_win = lambda i, s, n: (i * SW + jnp.clip(s - n_pairs, 0, SW - 1), 0)
    grid_spec = pltpu.PrefetchScalarGridSpec(
        num_scalar_prefetch=1,
        grid=(nw // SW, n_pairs + SW),
        in_specs=[
            pl.BlockSpec((SW * W, D), lambda i, s, n: (i, 0), **resident),
            pl.BlockSpec((None, None, SW, 2, W), routed),
            pl.BlockSpec((None, None, SW, 2, W), routed),
            pl.BlockSpec((2, D, EXPERT_DIM), pair),
            pl.BlockSpec((2, D, EXPERT_DIM), pair),
            pl.BlockSpec((2, EXPERT_DIM, D), pair),
            pl.BlockSpec((D, EXPERT_DIM), const2, **resident),
            pl.BlockSpec((D, EXPERT_DIM), const2, **resident),
            pl.BlockSpec((EXPERT_DIM, D), const2, **resident),
            pl.BlockSpec((1, D), const2),
            pl.BlockSpec((1, D), const2),
        ],
        out_specs=[pl.BlockSpec((W, D), out_win), pl.BlockSpec((W, D), out_win)],
        scratch_shapes=[pltpu.VMEM((SW * W, D), BF16), pltpu.VMEM((SW * W, D), F32)],
    )
    return pl.pallas_call(
        functools.partial(_moe_kernel, alpha=alpha),
        grid_spec=grid_spec,
        out_shape=[jax.ShapeDtypeStruct((N, D), F32), jax.ShapeDtypeStruct((N, D), BF16)],
        compiler_params=_cparams("parallel", "arbitrary"),
        name="moe",
    )(n_sub, x1, per_pair(rank), per_pair(gt.reshape(N_EXPERTS, nw, W)),
      p['we_gate'].astype(BF16), p['we_up'].astype(BF16), p['we_down'].astype(BF16),
      p['ws_gate'].astype(BF16), p['ws_up'].astype(BF16), p['ws_down'].astype(BF16),
      p['ln2_w'][None, :], p['ln2_b'][None, :])


def _t5_bucket(rel):
    half = REL_BUCKETS // 2
    max_exact = half // 2
    ret = jnp.where(rel > 0, half, 0)
    n = jnp.abs(rel)
    nf = jnp.maximum(n, 1).astype(F32)
    large = max_exact + (jnp.log(nf / max_exact) / math.log(REL_MAX_DIST / max_exact)
                         * (half - max_exact)).astype(jnp.int32)
    large = jnp.minimum(large, half - 1)
    return ret + jnp.where(n < max_exact, n, large)


def _bias_tables(rel_bias):
    i = jnp.arange(Q_BLOCK)[:, None]
    u = jnp.arange(2 * Q_BLOCK)[None, :]
    toep = jnp.transpose(rel_bias[_t5_bucket(u - Q_BLOCK - i)], (2, 0, 1))
    far = rel_bias[_t5_bucket(jnp.asarray(-(Q_BLOCK + 1)))]
    return toep.astype(F32) * LOG2E, far.astype(F32) * LOG2E


def _rope_tables(S):
    half = R_QK_DIM // 2
    inv = ROPE_BASE ** (-jnp.arange(half, dtype=F32) / half)
    ang = jnp.arange(S, dtype=F32)[:, None] * inv[None, :]
    return jnp.cos(ang), jnp.sin(ang)


def _layer(x, xb, tables, p, *, B, S, alpha):
    N, D = x.shape
    cos, sin, toep, bias15, ret_consts = tables
    TM = 512 if N % 512 == 0 else 256
    a_w = A_HEADS * A_HEAD_DIM
    cuts = np.cumsum([0, a_w, A_LATENT, IDX_HEADS * IDX_DIM, IDX_DIM, IDX_HEADS,
                      R_HEADS * R_QK_DIM, R_HEADS * R_QK_DIM, R_HEADS * R_V_DIM, R_HEADS * R_V_DIM, D, D])
    w_in = p['w_in']
    piece = lambda k: w_in[:, int(cuts[k]):int(cuts[k + 1])].astype(BF16)
    row_spec = lambda w: pl.BlockSpec((1, w), lambda i, j: (0, 0))

    TMB = 1024 if (N % 1024 == 0 and S % 1024 == 0) else TM
    big = dict(tm=TMB, tn=1024, epilogue=_epi_plain)
    q_a = _mm(xb, piece(0), out_dtypes=[BF16], name="proj_qa", **big)
    c_kv = _mm(xb, piece(1), tm=TMB, tn=A_LATENT, out_dtypes=[BF16], epilogue=_epi_rms,
               extras=[p['ckv_norm'][None, :]], extra_specs=[row_spec(A_LATENT)], name="proj_ckv")
    q_idx = _mm(xb, piece(2), out_dtypes=[BF16], name="proj_qidx", **big)
    k_idx = _mm(xb, piece(3), tm=TMB, tn=IDX_DIM, out_dtypes=[BF16], epilogue=_epi_rms,
                extras=[p['kidx_norm'][None, :]], extra_specs=[row_spec(IDX_DIM)], name="proj_kidx")
    w_widx = jnp.pad(piece(4), ((0, 0), (0, LANES - IDX_HEADS)))
    w_idx = _mm(xb, w_widx, tm=TMB, tn=LANES, out_dtypes=[F32],
                epilogue=functools.partial(_epi_scale, scale=IDX_HEADS ** -0.5 * IDX_DIM ** -0.5),
                name="proj_widx")
    pos_spec = pl.BlockSpec((TMB, R_QK_DIM // 2), lambda i, j: (i % (S // TMB), 0))
    q_r = _mm(xb, piece(5), tm=TMB, tn=2 * R_QK_DIM, out_dtypes=[BF16],
              epilogue=functools.partial(_epi_rope, scale=1.0),
              extras=[cos, sin], extra_specs=[pos_spec, pos_spec], name="proj_qr")
    k_r = _mm(xb, piece(6), tm=TMB, tn=2 * R_QK_DIM, out_dtypes=[BF16],
              epilogue=functools.partial(_epi_rope, scale=R_QK_DIM ** -0.5),
              extras=[cos, sin], extra_specs=[pos_spec, pos_spec], name="proj_kr")
    v_r = _mm(xb, piece(7), out_dtypes=[BF16], name="proj_vr", **big)
    g_r = _mm(xb, piece(8), out_dtypes=[F32], name="proj_gr", **big)
    g_a = _mm(xb, piece(9), out_dtypes=[F32], name="proj_ga", **big)
    g_b = _mm(xb, piece(10), out_dtypes=[F32], name="proj_gb", **big)

    w_uk_t = jnp.transpose(p['w_uk'], (0, 2, 1)).astype(BF16)
    pad = lambda a: jnp.pad(a.reshape(B, S, -1), ((0, 0), (Q_BLOCK, _dsa_key_rows(S) - Q_BLOCK - S), (0, 0)))
    y_a = _dsa(q_a, q_idx, w_idx, pad(k_idx), pad(c_kv), toep, bias15, w_uk_t, p['w_uv'].astype(BF16),
               B=B, S=S)

    y_r = _retention(q_r, k_r, v_r, g_r, p['gn_w'][None, :], p['gn_b'][None, :], ret_consts, B=B, S=S)

    tn_m = 512
    tile_spec = pl.BlockSpec((TM, tn_m), lambda i, j: (i, j))
    merged = _mm(y_r, p['w_pb'].astype(BF16), tm=TM, tn=tn_m, out_dtypes=[BF16], epilogue=_epi_merge,
                 extras=[g_b, y_a, p['w_pa'].astype(BF16), g_a],
                 extra_specs=[tile_spec, pl.BlockSpec((TM, a_w), lambda i, j: (i, 0)),
                              pl.BlockSpec((a_w, tn_m), lambda i, j: (0, j)), tile_spec],
                 name="proj_merge")
    full_row = pl.BlockSpec((TM, D), lambda i, j: (i, 0))
    x1, x1b = _mm(merged, p['w_o'].astype(BF16), tm=TM, tn=D, out_dtypes=[F32, BF16],
                  epilogue=functools.partial(_epi_ln, alpha=alpha),
                  extras=[x, p['ln1_w'][None, :], p['ln1_b'][None, :]],
                  extra_specs=[full_row, row_spec(D), row_spec(D)], name="out_ln1")

    gt, rk = _router(x1b, p['w_router'].T.astype(BF16), p['b_router'][:, None])
    return _moe(x1, gt, rk, p, alpha=alpha)


def kernel(x, rel_bias, w_in, ckv_norm, kidx_norm, w_uk, w_uv, gn_w, gn_b, w_pa, w_pb, w_o, ln1_w, ln1_b,
           w_router, b_router, we_gate, we_up, we_down, ws_gate, ws_up, ws_down, ln2_w, ln2_b):
    B, S, D = x.shape
    depth = w_in.shape[0]
    alpha = (2 * depth) ** 0.25
    cos, sin = _rope_tables(S)
    toep, bias15 = _bias_tables(rel_bias)
    tables = (cos, sin, toep, bias15, _retention_consts())
    params = dict(w_in=w_in, ckv_norm=ckv_norm, kidx_norm=kidx_norm, w_uk=w_uk, w_uv=w_uv, gn_w=gn_w,
                  gn_b=gn_b, w_pa=w_pa, w_pb=w_pb, w_o=w_o, ln1_w=ln1_w, ln1_b=ln1_b, w_router=w_router,
                  b_router=b_router, we_gate=we_gate, we_up=we_up, we_down=we_down, ws_gate=ws_gate,
                  ws_up=ws_up, ws_down=ws_down, ln2_w=ln2_w, ln2_b=ln2_b)
    xf = x.reshape(B * S, D)
    xb = xf.astype(BF16)
    for l in range(depth):
        xf, xb = _layer(xf, xb, tables, {k: v[l] for k, v in params.items()}, B=B, S=S, alpha=alpha)
    return xf.reshape(B, S, D)
```

```python
import functools
import math

import jax
import jax.numpy as jnp
import numpy as np
from jax import lax
from jax.experimental import pallas as pl
from jax.experimental.pallas import tpu as pltpu

CHUNK = 64
Q_BLOCK = 128
A_HEADS = 16
A_HEAD_DIM = 128
A_LATENT = 256
IDX_HEADS = 16
IDX_DIM = 128
TOPK_MAX = 256
REL_BUCKETS = 32
REL_MAX_DIST = 128
R_HEADS = 8
R_QK_DIM = 256
R_V_DIM = 512
ROPE_BASE = 10000.0
N_EXPERTS = 64
EXPERT_DIM = 256
TOP_K = 8
N_GROUPS = 8
TOPK_GROUPS = 4
ROUTED_SCALE = 2.5
LN_EPS = 1e-5
RMS_EPS = 1e-6

LANES = 128
FAR_TILE = 512
HEAD_GROUP = 4
ATTN_TILES = 1
RADIX_PLAIN_BITS = 20
RET_GROUP = 512
RET_HEADS = 2
MOE_WINDOW = 512
MOE_SUPER = 2
MOE_ROWS = 128
VMEM_LIMIT = 56 * 1024 * 1024
INT_MIN = -2 ** 31
NEG_BIG = -1e30
LOG2E = 1.4426950408889634

F32 = jnp.float32
BF16 = jnp.bfloat16


def _cparams(*sem):
    return pltpu.CompilerParams(dimension_semantics=sem, vmem_limit_bytes=VMEM_LIMIT)


def _sigmoid(x):
    return 1.0 / (1.0 + jnp.exp(-x))


def _dot(a, b):
    return jnp.dot(a, b, preferred_element_type=F32)


def _dot_nt(a, b):
    return lax.dot_general(a, b, (((1,), (1,)), ((), ())), preferred_element_type=F32)


def _layer_norm_rows(v, w, b):
    mu = jnp.mean(v, axis=-1, keepdims=True)
    d = v - mu
    var = jnp.mean(d * d, axis=-1, keepdims=True)
    return d * lax.rsqrt(var + LN_EPS) * w + b


def _mm_kernel(*refs, epilogue, n_extra, n_out):
    a_ref, b_ref = refs[0], refs[1]
    extra = refs[2:2 + n_extra]
    outs = refs[2 + n_extra:2 + n_extra + n_out]
    acc = _dot(a_ref[...], b_ref[...])
    res = epilogue(acc, *extra)
    if not isinstance(res, tuple):
        res = (res,)
    for o_ref, r in zip(outs, res):
        o_ref[...] = r.astype(o_ref.dtype)


def _mm(a, b, *, tm, tn, out_dtypes, epilogue, extras=(), extra_specs=(), name):
    M = a.shape[0]
    K, n_cols = b.shape
    grid = (M // tm, n_cols // tn)
    a_spec = pl.BlockSpec((tm, K), lambda i, j: (i, 0))
    b_spec = pl.BlockSpec((K, tn), lambda i, j: (0, j))
    out_shape = [jax.ShapeDtypeStruct((M, n_cols), dt) for dt in out_dtypes]
    out_specs = [pl.BlockSpec((tm, tn), lambda i, j: (i, j)) for _ in out_dtypes]
    res = pl.pallas_call(
        functools.partial(_mm_kernel, epilogue=epilogue, n_extra=len(extras), n_out=len(out_dtypes)),
        grid=grid,
        in_specs=[a_spec, b_spec, *extra_specs],
        out_specs=out_specs,
        out_shape=out_shape,
        compiler_params=_cparams("parallel", "arbitrary"),
        name=name,
    )(a, b, *extras)
    return res[0] if len(res) == 1 else res


def _epi_plain(acc):
    return acc


def _epi_scale(acc, *, scale):
    return acc * scale


def _epi_rms(acc, w_ref):
    return acc * lax.rsqrt(jnp.mean(acc * acc, axis=-1, keepdims=True) + RMS_EPS) * w_ref[...]


def _epi_rope(acc, cos_ref, sin_ref, *, scale):
    half = R_QK_DIM // 2
    c, s = cos_ref[...] * scale, sin_ref[...] * scale
    out = []
    for h in range(acc.shape[-1] // R_QK_DIM):
        x1 = acc[:, h * R_QK_DIM:h * R_QK_DIM + half]
        x2 = acc[:, h * R_QK_DIM + half:(h + 1) * R_QK_DIM]
        out += [x1 * c - x2 * s, x1 * s + x2 * c]
    return jnp.concatenate(out, axis=-1)


def _epi_merge(acc_b, gb_ref, ya_ref, wpa_ref, ga_ref):
    return _sigmoid(gb_ref[...]) * acc_b + _sigmoid(ga_ref[...]) * _dot(ya_ref[...], wpa_ref[...])


def _epi_ln(acc, res_ref, w_ref, b_ref, *, alpha):
    y = _layer_norm_rows(alpha * res_ref[...] + acc, w_ref[...], b_ref[...])
    return y, y


def _float_key(x):
    bits = pltpu.bitcast(x, jnp.int32)
    return bits ^ ((bits >> 31) & jnp.int32(0x7FFFFFFF))


def _dsa_kernel(bias15_ref, qa_ref, qidx_ref, widx_ref, kidx_ref, ckv_ref, toep_ref, wuk_ref, wuv_ref,
                out_ref, scf_ref, scn_ref, tmp0_ref, tmp1_ref, zb0_ref, zb1_ref, madd_ref, qa2_ref, qi2_ref,
                wb_ref, p0_ref, p1_ref, acc_ref, m_ref, l_ref, al0_ref, al1_ref, *, topk, pos_bits):
    i = pl.program_id(1)
    t0 = i * Q_BLOCK
    far_end = t0 - Q_BLOCK
    n_far = (jnp.maximum(i - 1, 0) * Q_BLOCK + FAR_TILE - 1) // FAR_TILE
    n_ch = FAR_TILE // LANES
    H = A_HEADS
    QB = Q_BLOCK

    for h in range(H):
        q_abs = _dot(qa_ref[:, h * A_HEAD_DIM:(h + 1) * A_HEAD_DIM], wuk_ref[h])
        qa2_ref[h * QB:(h + 1) * QB, :] = (q_abs * (A_HEAD_DIM ** -0.5 * LOG2E)).astype(BF16)
        qi2_ref[h * QB:(h + 1) * QB, :] = qidx_ref[:, h * IDX_DIM:(h + 1) * IDX_DIM]
        wb_ref[h] = jnp.broadcast_to(widx_ref[:, h:h + 1], (QB, LANES))

    lane = lax.broadcasted_iota(jnp.int32, (QB, LANES), 1)
    row = lax.broadcasted_iota(jnp.int32, (QB, LANES), 0)

    tmp = (tmp0_ref, tmp1_ref)
    zbuf = (zb0_ref, zb1_ref)
    pbuf = (p0_ref, p1_ref)
    albuf = (al0_ref, al1_ref)
    n_pairs = (n_far + 1) // 2
    last_start = kidx_ref.shape[0] - FAR_TILE

    def far_rows(kt):
        return pl.ds(pl.multiple_of(jnp.clip(Q_BLOCK + kt * FAR_TILE, Q_BLOCK, last_start), LANES), FAR_TILE)

    def head_sum(z_ref, c):
        acc = jnp.zeros((QB, LANES), F32)
        for h in range(H):
            z = z_ref[h * QB:(h + 1) * QB, c * LANES:(c + 1) * LANES]
            acc = acc + wb_ref[h] * jnp.maximum(z, 0.0)
        return acc

    def far_scores(z_ref, kt):
        for c in range(n_ch):
            key = _float_key(head_sum(z_ref, c))
            s_pos = kt * FAR_TILE + c * LANES + lane
            scf_ref[kt, :, c * LANES:(c + 1) * LANES] = jnp.where(s_pos < far_end, key, INT_MIN)

    def idx_matmul(z_ref, kt):
        z_ref[...] = _dot_nt(qi2_ref[...], kidx_ref[far_rows(kt), :])

    idx_matmul(tmp[0], 0)

    def score_pair(j, carry):
        kt = 2 * j
        idx_matmul(tmp[1], kt + 1)
        far_scores(tmp[0], kt)
        idx_matmul(tmp[0], kt + 2)
        far_scores(tmp[1], kt + 1)
        return carry

    lax.fori_loop(0, n_pairs, score_pair, 0)

    near_start = pl.multiple_of(t0, LANES)
    tmp[0][:, :2 * LANES] = _dot_nt(qi2_ref[...], kidx_ref[pl.ds(near_start, 2 * LANES), :])
    u_lo = jnp.where(i == 0, QB, 0)
    u_hi = jnp.where(row < CHUNK, QB + CHUNK, 2 * QB)
    for c in range(2):
        key = _float_key(head_sum(tmp[0], c))
        u = c * LANES + lane
        vis = jnp.logical_and(u >= u_lo, u < u_hi)
        scn_ref[:, c * LANES:(c + 1) * LANES] = jnp.where(vis, key, INT_MIN)

    def bit_body(bi, carry):
        p, kept = carry
        bit = jnp.left_shift(jnp.int32(1), 31 - bi)
        p_try = p | bit
        t_try = p_try ^ jnp.int32(INT_MIN)

        def cnt_body(j, cnt):
            for kt in (2 * j, 2 * j + 1):
                for c in range(n_ch):
                    cnt = cnt + jnp.where(scf_ref[kt, :, c * LANES:(c + 1) * LANES] >= t_try, 1, 0)
            return cnt

        cnt = lax.fori_loop(0, n_pairs, cnt_body, jnp.zeros((QB, LANES), jnp.int32))
        for c in range(2):
            cnt = cnt + jnp.where(scn_ref[:, c * LANES:(c + 1) * LANES] >= t_try, 1, 0)
        total = jnp.broadcast_to(jnp.sum(cnt, axis=1, keepdims=True), (QB, LANES))
        accept = total >= topk
        return jnp.where(accept, p_try, p), jnp.where(accept, total, kept)

    start = (jnp.zeros((QB, LANES), jnp.int32), jnp.full((QB, LANES), 2 ** 30, jnp.int32))
    carry = lax.fori_loop(0, RADIX_PLAIN_BITS, bit_body, start)

    def pending(kept):
        return jnp.max(jnp.where(kept == topk, 0, 1))

    def tail_cond(state):
        bi, more, _, _ = state
        return jnp.logical_and(bi < 32, more > 0)

    def tail_body(state):
        bi, _, p, kept = state
        p, kept = bit_body(bi + 1, bit_body(bi, (p, kept)))
        return bi + 2, pending(kept), p, kept

    _, _, p_fin, kept_fin = lax.while_loop(
        tail_cond, tail_body, (jnp.int32(RADIX_PLAIN_BITS), pending(carry[1]), *carry))
    t_raw = p_fin ^ jnp.int32(INT_MIN)
    thr = jnp.maximum(t_raw, jnp.int32(INT_MIN + 1))

    m_ref[...] = jnp.full(m_ref.shape, NEG_BIG, F32)
    l_ref[...] = jnp.zeros(l_ref.shape, F32)
    acc_ref[...] = jnp.zeros(acc_ref.shape, F32)

    n_wide = (n_far + ATTN_TILES - 1) // ATTN_TILES
    n_mask = n_wide * ATTN_TILES

    def write_masks(keep):
        def far_tile(kt, carry):
            for c in range(n_ch):
                cols = slice(c * LANES, (c + 1) * LANES)
                pos = kt * FAR_TILE + c * LANES + lane
                mask = jnp.where(keep(scf_ref[kt, :, cols], pos), 0.0, -jnp.inf)
                scf_ref[kt, :, cols] = pltpu.bitcast(mask, jnp.int32)
            return carry

        lax.fori_loop(0, n_mask, far_tile, 0)
        for c in range(2):
            cols = slice(c * LANES, (c + 1) * LANES)
            pos = far_end + c * LANES + lane
            madd_ref[:, cols] = jnp.where(keep(scn_ref[:, cols], pos), 0.0, -jnp.inf)

    tied = jnp.where(kept_fin > topk, jnp.where(kept_fin < 2 ** 30, 1, 0), 0)
    any_tie = jnp.max(tied)

    @pl.when(any_tie == 0)
    def _():
        write_masks(lambda key, pos: key >= thr)

    @pl.when(any_tie > 0)
    def _():
        def count(pred):
            def far_tile(kt, cnt):
                for c in range(n_ch):
                    pos = kt * FAR_TILE + c * LANES + lane
                    cnt = cnt + pred(scf_ref[kt, :, c * LANES:(c + 1) * LANES], pos)
                return cnt

            cnt = lax.fori_loop(0, n_mask, far_tile, jnp.zeros((QB, LANES), jnp.int32))
            for c in range(2):
                cnt = cnt + pred(scn_ref[:, c * LANES:(c + 1) * LANES], far_end + c * LANES + lane)
            return jnp.broadcast_to(jnp.sum(cnt, axis=1, keepdims=True), (QB, LANES))

        need = topk - count(lambda key, pos: jnp.where(key > t_raw, 1, 0))

        def pos_bit(bi, low):
            bit = jnp.left_shift(jnp.int32(1), pos_bits - 1 - bi)
            probe = low + bit - 1
            below = count(lambda key, pos: jnp.where(key == t_raw, jnp.where(pos <= probe, 1, 0), 0))
            return jnp.where(below >= need, low, low + bit)

        cut = lax.fori_loop(0, pos_bits, pos_bit, jnp.zeros((QB, LANES), jnp.int32))
        cut = jnp.where(p_fin == 0, INT_MIN, cut)
        write_masks(lambda key, pos: jnp.where(key == t_raw, jnp.where(pos <= cut, 1, 0),
                                               jnp.where(key > t_raw, 1, 0)) > 0)

    GR = HEAD_GROUP * QB
    n_grp = H // HEAD_GROUP

    def rows_at(start, size):
        return pl.ds(start if isinstance(start, int) else pl.multiple_of(start, size), size)

    def group_rows(hg):
        return rows_at(hg * GR, GR)

    def softmax_group(width, near, z_ref, p_ref, al_ref, kt, hg):
        chunks = [slice(c * LANES, (c + 1) * LANES) for c in range(width // LANES)]
        for hh in range(HEAD_GROUP):
            h = hg * HEAD_GROUP + hh
            rows = slice(hh * QB, (hh + 1) * QB)
            stat = rows_at(h * QB, QB)

            def masked_logits(ch):
                if near:
                    v = z_ref[rows, ch] + madd_ref[:, ch] + toep_ref[h, :, ch]
                else:
                    tile, off = divmod(ch.start, FAR_TILE)
                    v = z_ref[rows, ch] + pltpu.bitcast(scf_ref[ATTN_TILES * kt + tile, :, off:off + LANES], F32)
                z_ref[rows, ch] = v
                return v

            mx = masked_logits(chunks[0])
            for ch in chunks[1:]:
                mx = jnp.maximum(mx, masked_logits(ch))
            mx = jnp.max(mx, axis=1, keepdims=True)
            bias = 0.0 if near else bias15_ref[h]
            m_old = m_ref[stat, :]
            m_new = jnp.maximum(m_old, mx + bias)
            shift = m_new - bias
            psum = jnp.zeros((QB, LANES), F32)
            for ch in chunks:
                p = jnp.exp2(z_ref[rows, ch] - shift)
                psum = psum + p
                p_ref[rows, ch] = p.astype(BF16)
            alpha = jnp.exp2(m_old - m_new)
            l_ref[stat, :] = alpha * l_ref[stat, :] + jnp.sum(psum, axis=1, keepdims=True)
            m_ref[stat, :] = m_new
            al_ref[rows, :] = alpha

    def pv_group(p_ref, al_ref, c_tile, width, hg):
        al = al_ref[...]
        rows = group_rows(hg)
        acc_ref[rows, :] = (acc_ref[rows, :] * jnp.concatenate([al] * (A_LATENT // LANES), axis=1)
                            + _dot(p_ref[:, :width], c_tile))

    AW = ATTN_TILES * FAR_TILE

    def wide_rows(kw):
        start = jnp.clip(Q_BLOCK + kw * AW, Q_BLOCK, ckv_ref.shape[0] - AW)
        return pl.ds(pl.multiple_of(start, LANES), AW)

    def far_logits(z_ref, kw, hg):
        z_ref[...] = _dot_nt(qa2_ref[group_rows(hg), :], ckv_ref[wide_rows(kw), :])

    def far_pv(slot, kw, hg):
        pv_group(pbuf[slot], albuf[slot], ckv_ref[wide_rows(kw), :], AW, hg)

    p1_ref[...] = jnp.zeros(p1_ref.shape, BF16)
    al1_ref[...] = jnp.ones(al1_ref.shape, F32)
    far_logits(zbuf[0], 0, 0)

    def attn_pair(j, carry):
        unit = lambda g: (g // n_grp, g % n_grp)
        g = 2 * j
        far_logits(zbuf[1], *unit(g + 1))
        softmax_group(AW, False, zbuf[0], pbuf[0], albuf[0], *unit(g))
        far_pv(1, *unit(g - 1))
        far_logits(zbuf[0], *unit(g + 2))
        softmax_group(AW, False, zbuf[1], pbuf[1], albuf[1], *unit(g + 1))
        far_pv(0, *unit(g))
        return carry

    lax.fori_loop(0, n_wide * (n_grp // 2), attn_pair, 0)
    far_pv(1, n_wide - 1, n_grp - 1)

    c_near = ckv_ref[pl.ds(near_start, 2 * LANES), :]

    for hg in range(n_grp):
        s = hg % 2
        zbuf[s][:, :2 * LANES] = _dot_nt(qa2_ref[group_rows(hg), :], c_near)
        softmax_group(2 * LANES, True, zbuf[s], pbuf[s], albuf[s], 0, hg)
        pv_group(pbuf[s], albuf[s], c_near, 2 * LANES, hg)

    for h in range(H):
        rows = slice(h * QB, (h + 1) * QB)
        inv = 1.0 / l_ref[rows, :]
        o_lat = jnp.concatenate([acc_ref[rows, c * LANES:(c + 1) * LANES] * inv
                                 for c in range(A_LATENT // LANES)], axis=1).astype(BF16)
        out_ref[:, h * A_HEAD_DIM:(h + 1) * A_HEAD_DIM] = _dot(o_lat, wuv_ref[h]).astype(out_ref.dtype)


def _dsa_far_tiles(S):
    return max(1, -(-(S - 2 * Q_BLOCK) // FAR_TILE))


def _dsa_key_rows(S):
    n_wide = -(-_dsa_far_tiles(S) // ATTN_TILES)
    return Q_BLOCK + max(S, n_wide * ATTN_TILES * FAR_TILE)


def _dsa(q_a, qidx, widx, kidx_pad, ckv_pad, toep, bias15, w_uk_t, w_uv, *, B, S):
    nqb = S // Q_BLOCK
    topk = min(TOPK_MAX, S // 4)
    n_far_max = _dsa_far_tiles(S)
    key_rows = _dsa_key_rows(S)
    AW = ATTN_TILES * FAR_TILE
    HQ = A_HEADS * Q_BLOCK
    GR = HEAD_GROUP * Q_BLOCK
    resident = dict(pipeline_mode=pl.Buffered(1))
    return pl.pallas_call(
        functools.partial(_dsa_kernel, topk=topk, pos_bits=(S - 1).bit_length()),
        grid=(B, nqb),
        in_specs=[
            pl.BlockSpec(memory_space=pltpu.SMEM),
            pl.BlockSpec((Q_BLOCK, A_HEADS * A_HEAD_DIM), lambda b, i: (b * nqb + i, 0)),
            pl.BlockSpec((Q_BLOCK, IDX_HEADS * IDX_DIM), lambda b, i: (b * nqb + i, 0)),
            pl.BlockSpec((Q_BLOCK, LANES), lambda b, i: (b * nqb + i, 0)),
            pl.BlockSpec((None, key_rows, IDX_DIM), lambda b, i: (b, 0, 0), **resident),
            pl.BlockSpec((None, key_rows, A_LATENT), lambda b, i: (b, 0, 0), **resident),
            pl.BlockSpec((A_HEADS, Q_BLOCK, 2 * LANES), lambda b, i: (0, 0, 0), **resident),
            pl.BlockSpec((A_HEADS, A_HEAD_DIM, A_LATENT), lambda b, i: (0, 0, 0), **resident),
            pl.BlockSpec((A_HEADS, A_LATENT, A_HEAD_DIM), lambda b, i: (0, 0, 0), **resident),
        ],
        out_specs=pl.BlockSpec((Q_BLOCK, A_HEADS * A_HEAD_DIM), lambda b, i: (b * nqb + i, 0)),
        out_shape=jax.ShapeDtypeStruct((B * S, A_HEADS * A_HEAD_DIM), BF16),
        scratch_shapes=[
            pltpu.VMEM((n_far_max + 1, Q_BLOCK, FAR_TILE), jnp.int32),
            pltpu.VMEM((Q_BLOCK, 2 * LANES), jnp.int32),
            pltpu.VMEM((HQ, FAR_TILE), F32),
            pltpu.VMEM((HQ, FAR_TILE), F32),
            pltpu.VMEM((GR, AW), F32),
            pltpu.VMEM((GR, AW), F32),
            pltpu.VMEM((Q_BLOCK, 2 * LANES), F32),
            pltpu.VMEM((HQ, A_LATENT), BF16),
            pltpu.VMEM((HQ, IDX_DIM), BF16),
            pltpu.VMEM((IDX_HEADS, Q_BLOCK, LANES), F32),
            pltpu.VMEM((GR, AW), BF16),
            pltpu.VMEM((GR, AW), BF16),
            pltpu.VMEM((HQ, A_LATENT), F32),
            pltpu.VMEM((HQ, LANES), F32),
            pltpu.VMEM((HQ, LANES), F32),
            pltpu.VMEM((GR, LANES), F32),
            pltpu.VMEM((GR, LANES), F32),
        ],
        compiler_params=_cparams("parallel", "arbitrary"),
        name="dsa",
    )(bias15, q_a, qidx, widx, kidx_pad, ckv_pad, toep, w_uk_t, w_uv)


def _ret_kernel(gtot_ref, q_ref, k_ref, v_ref, g_ref, d_ref, xi_ref, zeta_ref, gnw_ref, gnb_ref,
                o_ref, state_ref):
    @pl.when(pl.program_id(2) == 0)
    def _():
        state_ref[...] = jnp.zeros(state_ref.shape, F32)

    for j in range(RET_HEADS):
        h = pl.program_id(1) * RET_HEADS + j
        qk = slice(j * R_QK_DIM, (j + 1) * R_QK_DIM)
        vv = slice(j * R_V_DIM, (j + 1) * R_V_DIM)
        q, k, v = q_ref[:, qk], k_ref[:, qk], v_ref[:, vv]
        s = _dot_nt(q, k) * d_ref[j]
        o = _dot(s.astype(BF16), v) + _dot(q, state_ref[j].astype(BF16)) * xi_ref[j]
        kz = (k.astype(F32) * zeta_ref[j]).astype(BF16)
        upd = lax.dot_general(kz, v, (((0,), (0,)), ((), ())), preferred_element_type=F32)
        state_ref[j] = state_ref[j] * gtot_ref[h] + upd

        mu = jnp.mean(o, axis=-1, keepdims=True)
        d = o - mu
        var = jnp.mean(d * d, axis=-1, keepdims=True)
        y = d * lax.rsqrt(var + LN_EPS) * gnw_ref[:, vv] + gnb_ref[:, vv]
        g = g_ref[:, vv]
        o_ref[:, vv] = (g * _sigmoid(g) * y).astype(o_ref.dtype)


def _retention(q_rot, k_rot, v, g_r, gn_w, gn_b, consts, *, B, S):
    d_mat, xi, zeta, gtot = consts
    ng = S // RET_GROUP
    G = RET_GROUP
    HS = RET_HEADS
    rows = lambda b, h, g: (b * ng + g, h)
    per_head = lambda b, h, g: (h, 0, 0)
    return pl.pallas_call(
        _ret_kernel,
        grid=(B, R_HEADS // HS, ng),
        in_specs=[
            pl.BlockSpec(memory_space=pltpu.SMEM),
            pl.BlockSpec((G, HS * R_QK_DIM), rows),
            pl.BlockSpec((G, HS * R_QK_DIM), rows),
            pl.BlockSpec((G, HS * R_V_DIM), rows),
            pl.BlockSpec((G, HS * R_V_DIM), rows),
            pl.BlockSpec((HS, G, G), per_head),
            pl.BlockSpec((HS, G, 1), per_head),
            pl.BlockSpec((HS, G, 1), per_head),
            pl.BlockSpec((1, HS * R_V_DIM), lambda b, h, g: (0, h)),
            pl.BlockSpec((1, HS * R_V_DIM), lambda b, h, g: (0, h)),
        ],
        out_specs=pl.BlockSpec((G, HS * R_V_DIM), rows),
        out_shape=jax.ShapeDtypeStruct((B * S, R_HEADS * R_V_DIM), BF16),
        scratch_shapes=[pltpu.VMEM((HS, R_QK_DIM, R_V_DIM), F32)],
        compiler_params=_cparams("parallel", "parallel", "arbitrary"),
        name="retention",
    )(gtot, q_rot, k_rot, v, g_r, d_mat, xi, zeta, gn_w, gn_b)


def _retention_consts():
    G = RET_GROUP
    log_g = jnp.log1p(-jnp.exp2(-5.0 - jnp.arange(R_HEADS, dtype=F32)))
    pos = jnp.arange(G, dtype=F32)
    diff = pos[:, None] - pos[None, :]
    ci = jnp.arange(G)[:, None] // CHUNK
    cj = jnp.arange(G)[None, :] // CHUNK
    same = jnp.exp(log_g[:, None, None] * jnp.abs(diff))
    earlier = jnp.exp(log_g[:, None, None] * diff)
    d_mat = jnp.where(ci == cj, same, jnp.where(cj < ci, earlier, 0.0))
    xi = jnp.exp(log_g[:, None] * (pos[None, :] + 1.0))[..., None]
    zeta = jnp.exp(log_g[:, None] * (G - 1.0 - pos[None, :]))[..., None]
    gtot = jnp.exp(log_g * G)
    return d_mat, xi, zeta, gtot


def _rank_rows(v):
    n = v.shape[0]
    ridx = lax.broadcasted_iota(jnp.int32, v.shape, 0)
    rank = jnp.zeros(v.shape, jnp.int32)
    for j in range(n):
        rj = v[j:j + 1, :]
        rank = rank + jnp.where(ridx > j, jnp.where(rj >= v, 1, 0), jnp.where(rj > v, 1, 0))
    return rank


def _router_kernel(x_ref, wr_ref, b_ref, gt_ref, rk_ref):
    st = _dot_nt(wr_ref[...], x_ref[...])
    sig = _sigmoid(st)
    biased = sig + b_ref[...]
    per = N_EXPERTS // N_GROUPS
    blocks = [biased[g * per:(g + 1) * per, :] for g in range(N_GROUPS)]
    gscore = []
    for blk in blocks:
        top2 = jnp.where(_rank_rows(blk) < 2, blk, 0.0)
        gscore.append(jnp.sum(top2, axis=0, keepdims=True))
    masked = []
    for g in range(N_GROUPS):
        grank = jnp.zeros(gscore[g].shape, jnp.int32)
        for g2 in range(N_GROUPS):
            if g2 == g:
                continue
            beats = (gscore[g2] >= gscore[g]) if g2 < g else (gscore[g2] > gscore[g])
            grank = grank + jnp.where(beats, 1, 0)
        keep = jnp.broadcast_to(grank, blocks[g].shape) < TOPK_GROUPS
        masked.append(jnp.where(keep, blocks[g], -jnp.inf))
    cand = jnp.concatenate(masked, axis=0)
    sel = _rank_rows(cand) < TOP_K
    gates = jnp.where(sel, sig, 0.0)
    denom = jnp.sum(gates, axis=0, keepdims=True)
    gt_ref[...] = gates / denom * ROUTED_SCALE
    tm = sel.shape[1]
    upper = lax.broadcasted_iota(jnp.int32, (tm, tm), 0) <= lax.broadcasted_iota(jnp.int32, (tm, tm), 1)
    prefix = _dot(jnp.where(sel, 1.0, 0.0).astype(BF16), jnp.where(upper, 1.0, 0.0).astype(BF16))
    rk_ref[...] = jnp.where(sel, prefix.astype(jnp.int32) - 1, -1)


def _router(xb, wr_t, b_col):
    N = xb.shape[0]
    tm = MOE_WINDOW
    tile = pl.BlockSpec((N_EXPERTS, tm), lambda i: (0, i))
    return pl.pallas_call(
        _router_kernel,
        grid=(N // tm,),
        in_specs=[
            pl.BlockSpec((tm, xb.shape[1]), lambda i: (i, 0)),
            pl.BlockSpec(wr_t.shape, lambda i: (0, 0)),
            pl.BlockSpec(b_col.shape, lambda i: (0, 0)),
        ],
        out_specs=[tile, tile],
        out_shape=[jax.ShapeDtypeStruct((N_EXPERTS, N), F32), jax.ShapeDtypeStruct((N_EXPERTS, N), jnp.int32)],
        compiler_params=_cparams("parallel"),
        name="router",
    )(xb, wr_t, b_col)


def _swiglu(xb, wg, wu, wd):
    hg = _dot(xb, wg)
    hidden = (hg * _sigmoid(hg) * _dot(xb, wu)).astype(BF16)
    return _dot(hidden, wd)


def _moe_kernel(nsub_ref, x_ref, rk_ref, g_ref, wg_ref, wu_ref, wd_ref, sg_ref, su_ref, sd_ref,
                lnw_ref, lnb_ref, of_ref, ob_ref, xb_ref, acc_ref, *, alpha):
    i = pl.program_id(0)
    s = pl.program_id(1)
    n_pairs = pl.num_programs(1) - MOE_SUPER
    W = MOE_WINDOW

    @pl.when(s == 0)
    def _():
        xb_ref[...] = x_ref[...].astype(BF16)
        acc_ref[...] = jnp.zeros(acc_ref.shape, F32)

    @pl.when(s < n_pairs)
    def _():
        row_id = lax.broadcasted_iota(jnp.int32, (MOE_ROWS, W), 0)

        def sub_tile(k, j):
            win = slice(k * W, (k + 1) * W)
            picks, outs = [], []
            for q in range(2):
                hit = (row_id + j * MOE_ROWS) == rk_ref[k, q:q + 1, :]
                onehot = jnp.where(hit, 1.0, 0.0)
                pick = onehot.astype(BF16)
                xs = _dot(pick, xb_ref[win, :]).astype(BF16)
                y = _swiglu(xs, wg_ref[q], wu_ref[q], wd_ref[q])
                gate = jnp.sum(onehot * g_ref[k, q:q + 1, :], axis=1, keepdims=True)
                picks.append(pick)
                outs.append((y * gate).astype(BF16))
            acc_ref[win, :] += lax.dot_general(
                jnp.concatenate(picks, axis=0), jnp.concatenate(outs, axis=0),
                (((0,), (0,)), ((), ())), preferred_element_type=F32)

        for k in range(MOE_SUPER):
            sub_tile(k, 0)
        for k in range(MOE_SUPER):
            lax.fori_loop(1, nsub_ref[(i * MOE_SUPER + k) * n_pairs + s],
                          lambda j, c, k=k: (sub_tile(k, j), c)[1], 0)

    for k in range(MOE_SUPER):
        @pl.when(s == n_pairs + k)
        def _(k=k):
            win = slice(k * W, (k + 1) * W)
            shared = _swiglu(xb_ref[win, :], sg_ref[...], su_ref[...], sd_ref[...])
            y = _layer_norm_rows(alpha * x_ref[win, :] + acc_ref[win, :] + shared, lnw_ref[...], lnb_ref[...])
            of_ref[...] = y
            ob_ref[...] = y.astype(BF16)


def _moe(x1, gt, rk, p, *, alpha):
    N, D = x1.shape
    W = MOE_WINDOW
    nw, n_pairs = N // W, N_EXPERTS // 2
    rank = rk.reshape(N_EXPERTS, nw, W)
    sel = rank >= 0
    SW = MOE_SUPER
    per_pair = lambda a: jnp.transpose(a.reshape(n_pairs, 2, nw // SW, SW, W), (2, 0, 3, 1, 4))
    n_sub = (jnp.sum(sel, axis=2, dtype=jnp.int32) + MOE_ROWS - 1) // MOE_ROWS
    n_sub = jnp.max(n_sub.T.reshape(nw, n_pairs, 2), axis=2).reshape(-1)
    pair = lambda i, s, n: (jnp.minimum(s, n_pairs - 1), 0, 0)
    resident = dict(pipeline_mode=pl.Buffered(1))
    const2 = lambda i, s, n: (0, 0)
    routed = lambda i, s, n: (i, jnp.minimum(s, n_pairs - 1), 0, 0, 0)
    out_win = lambda i, s, n: (i * SW + jnp.clip(s - n_pairs, 0, SW - 1), 0)
    grid_spec = pltpu.PrefetchScalarGridSpec(
        num_scalar_prefetch=1,
        grid=(nw // SW, n_pairs + SW),
        in_specs=[
            pl.BlockSpec((SW * W, D), lambda i, s, n: (i, 0), **resident),
            pl.BlockSpec((None, None, SW, 2, W), routed),
            pl.BlockSpec((None, None, SW, 2, W), routed),
            pl.BlockSpec((2, D, EXPERT_DIM), pair),
            pl.BlockSpec((2, D, EXPERT_DIM), pair),
            pl.BlockSpec((2, EXPERT_DIM, D), pair),
            pl.BlockSpec((D, EXPERT_DIM), const2, **resident),
            pl.BlockSpec((D, EXPERT_DIM), const2, **resident),
            pl.BlockSpec((EXPERT_DIM, D), const2, **resident),
            pl.BlockSpec((1, D), const2),
            pl.BlockSpec((1, D), const2),
        ],
        out_specs=[pl.BlockSpec((W, D), out_win), pl.BlockSpec((W, D), out_win)],
        scratch_shapes=[pltpu.VMEM((SW * W, D), BF16), pltpu.VMEM((SW * W, D), F32)],
    )
    return pl.pallas_call(
        functools.partial(_moe_kernel, alpha=alpha),
        grid_spec=grid_spec,
        out_shape=[jax.ShapeDtypeStruct((N, D), F32), jax.ShapeDtypeStruct((N, D), BF16)],
        compiler_params=_cparams("parallel", "arbitrary"),
        name="moe",
    )(n_sub, x1, per_pair(rank), per_pair(gt.reshape(N_EXPERTS, nw, W)),
      p['we_gate'].astype(BF16), p['we_up'].astype(BF16), p['we_down'].astype(BF16),
      p['ws_gate'].astype(BF16), p['ws_up'].astype(BF16), p['ws_down'].astype(BF16),
      p['ln2_w'][None, :], p['ln2_b'][None, :])


def _t5_bucket(rel):
    half = REL_BUCKETS // 2
    max_exact = half // 2
    ret = jnp.where(rel > 0, half, 0)
    n = jnp.abs(rel)
    nf = jnp.maximum(n, 1).astype(F32)
    large = max_exact + (jnp.log(nf / max_exact) / math.log(REL_MAX_DIST / max_exact)
                         * (half - max_exact)).astype(jnp.int32)
    large = jnp.minimum(large, half - 1)
    return ret + jnp.where(n < max_exact, n, large)


def _bias_tables(rel_bias):
    i = jnp.arange(Q_BLOCK)[:, None]
    u = jnp.arange(2 * Q_BLOCK)[None, :]
    toep = jnp.transpose(rel_bias[_t5_bucket(u - Q_BLOCK - i)], (2, 0, 1))
    far = rel_bias[_t5_bucket(jnp.asarray(-(Q_BLOCK + 1)))]
    return toep.astype(F32) * LOG2E, far.astype(F32) * LOG2E


def _rope_tables(S):
    half = R_QK_DIM // 2
    inv = ROPE_BASE ** (-jnp.arange(half, dtype=F32) / half)
    ang = jnp.arange(S, dtype=F32)[:, None] * inv[None, :]
    return jnp.cos(ang), jnp.sin(ang)


def _layer(x, xb, tables, p, *, B, S, alpha):
    N, D = x.shape
    cos, sin, toep, bias15, ret_consts = tables
    TM = 512 if N % 512 == 0 else 256
    a_w = A_HEADS * A_HEAD_DIM
    cuts = np.cumsum([0, a_w, A_LATENT, IDX_HEADS * IDX_DIM, IDX_DIM, IDX_HEADS,
                      R_HEADS * R_QK_DIM, R_HEADS * R_QK_DIM, R_HEADS * R_V_DIM, R_HEADS * R_V_DIM, D, D])
    w_in = p['w_in']
    piece = lambda k: w_in[:, int(cuts[k]):int(cuts[k + 1])].astype(BF16)
    row_spec = lambda w: pl.BlockSpec((1, w), lambda i, j: (0, 0))

    TMB = 1024 if (N % 1024 == 0 and S % 1024 == 0) else TM
    big = dict(tm=TMB, tn=1024, epilogue=_epi_plain)
    q_a = _mm(xb, piece(0), out_dtypes=[BF16], name="proj_qa", **big)
    c_kv = _mm(xb, piece(1), tm=TMB, tn=A_LATENT, out_dtypes=[BF16], epilogue=_epi_rms,
               extras=[p['ckv_norm'][None, :]], extra_specs=[row_spec(A_LATENT)], name="proj_ckv")
    q_idx = _mm(xb, piece(2), out_dtypes=[BF16], name="proj_qidx", **big)
    k_idx = _mm(xb, piece(3), tm=TMB, tn=IDX_DIM, out_dtypes=[BF16], epilogue=_epi_rms,
                extras=[p['kidx_norm'][None, :]], extra_specs=[row_spec(IDX_DIM)], name="proj_kidx")
    w_widx = jnp.pad(piece(4), ((0, 0), (0, LANES - IDX_HEADS)))
    w_idx = _mm(xb, w_widx, tm=TMB, tn=LANES, out_dtypes=[F32],
                epilogue=functools.partial(_epi_scale, scale=IDX_HEADS ** -0.5 * IDX_DIM ** -0.5),
                name="proj_widx")
    pos_spec = pl.BlockSpec((TMB, R_QK_DIM // 2), lambda i, j: (i % (S // TMB), 0))
    q_r = _mm(xb, piece(5), tm=TMB, tn=2 * R_QK_DIM, out_dtypes=[BF16],
              epilogue=functools.partial(_epi_rope, scale=1.0),
              extras=[cos, sin], extra_specs=[pos_spec, pos_spec], name="proj_qr")
    k_r = _mm(xb, piece(6), tm=TMB, tn=2 * R_QK_DIM, out_dtypes=[BF16],
              epilogue=functools.partial(_epi_rope, scale=R_QK_DIM ** -0.5),
              extras=[cos, sin], extra_specs=[pos_spec, pos_spec], name="proj_kr")
    v_r = _mm(xb, piece(7), out_dtypes=[BF16], name="proj_vr", **big)
    g_r = _mm(xb, piece(8), out_dtypes=[F32], name="proj_gr", **big)
    g_a = _mm(xb, piece(9), out_dtypes=[F32], name="proj_ga", **big)
    g_b = _mm(xb, piece(10), out_dtypes=[F32], name="proj_gb", **big)

    w_uk_t = jnp.transpose(p['w_uk'], (0, 2, 1)).astype(BF16)
    pad = lambda a: jnp.pad(a.reshape(B, S, -1), ((0, 0), (Q_BLOCK, _dsa_key_rows(S) - Q_BLOCK - S), (0, 0)))
    y_a = _dsa(q_a, q_idx, w_idx, pad(k_idx), pad(c_kv), toep, bias15, w_uk_t, p['w_uv'].astype(BF16),
               B=B, S=S)

    y_r = _retention(q_r, k_r, v_r, g_r, p['gn_w'][None, :], p['gn_b'][None, :], ret_consts, B=B, S=S)

    tn_m = 512
    tile_spec = pl.BlockSpec((TM, tn_m), lambda i, j: (i, j))
    merged = _mm(y_r, p['w_pb'].astype(BF16), tm=TM, tn=tn_m, out_dtypes=[BF16], epilogue=_epi_merge,
                 extras=[g_b, y_a, p['w_pa'].astype(BF16), g_a],
                 extra_specs=[tile_spec, pl.BlockSpec((TM, a_w), lambda i, j: (i, 0)),
                              pl.BlockSpec((a_w, tn_m), lambda i, j: (0, j)), tile_spec],
                 name="proj_merge")
    full_row = pl.BlockSpec((TM, D), lambda i, j: (i, 0))
    x1, x1b = _mm(merged, p['w_o'].astype(BF16), tm=TM, tn=D, out_dtypes=[F32, BF16],
                  epilogue=functools.partial(_epi_ln, alpha=alpha),
                  extras=[x, p['ln1_w'][None, :], p['ln1_b'][None, :]],
                  extra_specs=[full_row, row_spec(D), row_spec(D)], name="out_ln1")

    gt, rk = _router(x1b, p['w_router'].T.astype(BF16), p['b_router'][:, None])
    return _moe(x1, gt, rk, p, alpha=alpha)


def kernel(x, rel_bias, w_in, ckv_norm, kidx_norm, w_uk, w_uv, gn_w, gn_b, w_pa, w_pb, w_o, ln1_w, ln1_b,
           w_router, b_router, we_gate, we_up, we_down, ws_gate, ws_up, ws_down, ln2_w, ln2_b):
    B, S, D = x.shape
    depth = w_in.shape[0]
    alpha = (2 * depth) ** 0.25
    cos, sin = _rope_tables(S)
    toep, bias15 = _bias_tables(rel_bias)
    tables = (cos, sin, toep, bias15, _retention_consts())
    params = dict(w_in=w_in, ckv_norm=ckv_norm, kidx_norm=kidx_norm, w_uk=w_uk, w_uv=w_uv, gn_w=gn_w,
                  gn_b=gn_b, w_pa=w_pa, w_pb=w_pb, w_o=w_o, ln1_w=ln1_w, ln1_b=ln1_b, w_router=w_router,
                  b_router=b_router, we_gate=we_gate, we_up=we_up, we_down=we_down, ws_gate=ws_gate,
                  ws_up=ws_up, ws_down=ws_down, ln2_w=ln2_w, ln2_b=ln2_b)
    xf = x.reshape(B * S, D)
    xb = xf.astype(BF16)
    for l in range(depth):
        xf, xb = _layer(xf, xb, tables, {k: v[l] for k, v in params.items()}, B=B, S=S, alpha=alpha)
    return xf.reshape(B, S, D)
```
